```python
import math
import jax, jax.numpy as jnp
from jax import lax
import numpy as np

D_MODEL = 1024
BATCH = 8
SEQ = 4096
DEPTH = 1
DEC_BATCH = 2
DEC_SEQ = 8192
PAST_LEN = 128

GRID_W = 64
HEAD_DIM = 64
N_Q_HEADS = 8
N_KV_HEADS = 2
Q_PER_KV = N_Q_HEADS // N_KV_HEADS
D_ATTN = N_Q_HEADS * HEAD_DIM
D_KV = N_KV_HEADS * HEAD_DIM
Q_BLOCK = 128
ROPE_THETA = 10000.0
D_SSM = D_MODEL // 2
SSM_GROUP = 16
N_SSM_GROUPS = D_SSM // SSM_GROUP
SSM_STATE = 64
DT_MIN = 0.001
DT_MAX = 0.1
N_EXPERT_GROUPS = 4
EXPERTS_PER_GROUP = 4
N_EXPERTS = N_EXPERT_GROUPS * EXPERTS_PER_GROUP
TOP_K_INNER = 2
D_FF_EXPERT = D_MODEL // 2
ALPHA = (2.0 * DEPTH) ** 0.25
BETA = (8.0 * DEPTH) ** -0.25
EPS = 1e-6
SPLITS = (D_ATTN, D_ATTN + D_KV, D_ATTN + 2 * D_KV, D_ATTN + 2 * D_KV + D_SSM, D_ATTN + 2 * D_KV + D_SSM + D_MODEL)
D_IN_PROJ = D_ATTN + 2 * D_KV + D_SSM + 2 * D_MODEL

kernel_name = 'hybrid_gqa_s5_hiermoe_encoder'


def _layer_norm(x, g, b):
    xf = x.astype(jnp.float32)
    mu = jnp.mean(xf, axis=-1, keepdims=True)
    var = jnp.mean(jnp.square(xf - mu), axis=-1, keepdims=True)
    y = (xf - mu) * lax.rsqrt(var + EPS) * g.astype(jnp.float32) + b.astype(jnp.float32)
    return y.astype(x.dtype)


def _rms_heads(x, g):
    xf = x.astype(jnp.float32)
    y = xf * lax.rsqrt(jnp.mean(jnp.square(xf), axis=-1, keepdims=True) + EPS) * g.astype(jnp.float32)
    return y.astype(x.dtype)


def _axial_rope(t):
    rows = t // GRID_W
    row_pos = jnp.repeat(jnp.arange(rows, dtype=jnp.float32), GRID_W)
    col_pos = jnp.tile(jnp.arange(GRID_W, dtype=jnp.float32), rows)
    n_freq = HEAD_DIM // 4
    inv_freq = ROPE_THETA ** (-jnp.arange(n_freq, dtype=jnp.float32) / n_freq)
    ang = jnp.concatenate([row_pos[:, None] * inv_freq, col_pos[:, None] * inv_freq], axis=-1)
    return jnp.cos(ang), jnp.sin(ang)


def _apply_rope(x, cos, sin):
    xf = x.astype(jnp.float32)
    half = HEAD_DIM // 2
    x1, x2 = xf[..., :half], xf[..., half:]
    c = cos[None, :, None, :]
    s = sin[None, :, None, :]
    return jnp.concatenate([x1 * c - x2 * s, x1 * s + x2 * c], axis=-1).astype(x.dtype)


def _block_attention(q, k, v):
    bsz, t = q.shape[0], q.shape[1]
    nblk = t // Q_BLOCK
    qb = q.reshape(bsz, nblk, Q_BLOCK, N_KV_HEADS, Q_PER_KV, HEAD_DIM).transpose(1, 0, 2, 3, 4, 5)
    scale = HEAD_DIM ** -0.5

    def one_block(qi):
        s = jnp.einsum('bqkgd,bskd->bkgqs', qi, k, preferred_element_type=jnp.float32) * scale
        p = jax.nn.softmax(s, axis=-1)
        return jnp.einsum('bkgqs,bskd->bqkgd', p.astype(v.dtype), v)

    out = lax.map(one_block, qb)
    return out.transpose(1, 0, 2, 3, 4, 5).reshape(bsz, t, D_ATTN)


def _complex_affine_combine(e1, e2):
    a1r, a1i, b1r, b1i = e1
    a2r, a2i, b2r, b2i = e2
    ar = a2r * a1r - a2i * a1i
    ai = a2r * a1i + a2i * a1r
    br = a2r * b1r - a2i * b1i + b2r
    bi = a2r * b1i + a2i * b1r + b2i
    return (ar, ai, br, bi)


def _s5_branch(u, a_re, a_im, log_dt, b_re, b_im, c_re, c_im, d_skip, w_glu, b_glu):
    bsz, t = u.shape[0], u.shape[1]
    uf = u.astype(jnp.float32).reshape(bsz, t, N_SSM_GROUPS, SSM_GROUP)
    y = uf * d_skip.astype(jnp.float32).reshape(N_SSM_GROUPS, SSM_GROUP)
    for direction in range(2):
        dt = jnp.exp(log_dt[direction].astype(jnp.float32))[:, None]
        lr = a_re[direction].astype(jnp.float32)
        li = a_im[direction].astype(jnp.float32)
        mag = jnp.exp(lr * dt)
        ar = mag * jnp.cos(li * dt)
        ai = mag * jnp.sin(li * dt)
        den = lr * lr + li * li
        kr = ((ar - 1.0) * lr + ai * li) / den
        ki = (ai * lr - (ar - 1.0) * li) / den
        br = b_re[direction].astype(jnp.float32)
        bi = b_im[direction].astype(jnp.float32)
        bbar_r = kr[..., None] * br - ki[..., None] * bi
        bbar_i = kr[..., None] * bi + ki[..., None] * br
        bu_r = jnp.einsum('btgh,gph->btgp', uf, bbar_r)
        bu_i = jnp.einsum('btgh,gph->btgp', uf, bbar_i)
        a_seq_r = jnp.broadcast_to(ar[None, None], (1, t, N_SSM_GROUPS, SSM_STATE))
        a_seq_i = jnp.broadcast_to(ai[None, None], (1, t, N_SSM_GROUPS, SSM_STATE))
        _, _, sr, si = lax.associative_scan(_complex_affine_combine, (a_seq_r, a_seq_i, bu_r, bu_i), reverse=bool(direction), axis=1)
        y = y + jnp.einsum('btgp,ghp->btgh', sr, c_re[direction].astype(jnp.float32)) - jnp.einsum('btgp,ghp->btgh', si, c_im[direction].astype(jnp.float32))
    y = y.reshape(bsz, t, D_SSM)
    z = jax.nn.gelu(y)
    z = z * jax.nn.sigmoid(z @ w_glu.astype(jnp.float32) + b_glu.astype(jnp.float32))
    return z.astype(u.dtype)


def _token_mixer(x, w_in, q_norm_g, k_norm_g, ssm_a_re, ssm_a_im, ssm_log_dt, ssm_b_re, ssm_b_im, ssm_c_re, ssm_c_im, ssm_d, w_glu, b_glu, w_up_attn, w_up_ssm, w_out):
    bsz, t = x.shape[0], x.shape[1]
    proj = x @ w_in
    q, k, v, u, g_attn, g_ssm = jnp.split(proj, SPLITS, axis=-1)
    q = _rms_heads(q.reshape(bsz, t, N_Q_HEADS, HEAD_DIM), q_norm_g)
    k = _rms_heads(k.reshape(bsz, t, N_KV_HEADS, HEAD_DIM), k_norm_g)
    v = v.reshape(bsz, t, N_KV_HEADS, HEAD_DIM)
    cos, sin = _axial_rope(t)
    q = _apply_rope(q, cos, sin)
    k = _apply_rope(k, cos, sin)
    attn = _block_attention(q, k, v)
    ssm = _s5_branch(u, ssm_a_re, ssm_a_im, ssm_log_dt, ssm_b_re, ssm_b_im, ssm_c_re, ssm_c_im, ssm_d, w_glu, b_glu)
    merged = jax.nn.sigmoid(g_attn) * (attn @ w_up_attn) + jax.nn.sigmoid(g_ssm) * (ssm @ w_up_ssm)
    return merged @ w_out


def _hier_moe(h, w_router_group, b_router_group, w_router_expert, b_router_expert, w_exp_gate, w_exp_up, w_exp_down):
    bsz, t, d = h.shape
    xt = h.reshape(bsz * t, d)
    n_tok = bsz * t
    g_logits = (xt @ w_router_group + b_router_group).astype(jnp.float32)
    g_prob = jax.nn.softmax(g_logits, axis=-1)
    g_idx = jnp.argmax(g_logits, axis=-1)
    g_w = jnp.max(g_prob, axis=-1, keepdims=True)
    e_logits = (xt @ w_router_expert + b_router_expert).astype(jnp.float32).reshape(n_tok, N_EXPERT_GROUPS, EXPERTS_PER_GROUP)
    g_onehot = jax.nn.one_hot(g_idx, N_EXPERT_GROUPS, dtype=jnp.float32)
    e_in_group = jnp.einsum('ng,nge->ne', g_onehot, e_logits)
    top_v, top_i = lax.top_k(e_in_group, TOP_K_INNER)
    top_w = jax.nn.softmax(top_v, axis=-1) * g_w
    expert_id = g_idx[:, None] * EXPERTS_PER_GROUP + top_i
    combine = jnp.sum(jax.nn.one_hot(expert_id, N_EXPERTS, dtype=jnp.float32) * top_w[..., None], axis=1)
    out = jnp.zeros((n_tok, d), jnp.float32)
    for e in range(N_EXPERTS):
        he = jax.nn.silu(xt @ w_exp_gate[e]) * (xt @ w_exp_up[e])
        out = out + combine[:, e:e + 1] * (he @ w_exp_down[e]).astype(jnp.float32)
    return out.astype(h.dtype).reshape(bsz, t, d)


def _trunk(x, w_in, q_norm_g, k_norm_g, ssm_a_re, ssm_a_im, ssm_log_dt, ssm_b_re, ssm_b_im, ssm_c_re, ssm_c_im, ssm_d, w_glu, b_glu, w_up_attn, w_up_ssm, w_out, ln1_g, ln1_b, w_router_group, b_router_group, w_router_expert, b_router_expert, w_exp_gate, w_exp_up, w_exp_down, ln2_g, ln2_b):
    for l in range(DEPTH):
        mix = _token_mixer(x, w_in[l], q_norm_g[l], k_norm_g[l], ssm_a_re[l], ssm_a_im[l], ssm_log_dt[l], ssm_b_re[l], ssm_b_im[l], ssm_c_re[l], ssm_c_im[l], ssm_d[l], w_glu[l], b_glu[l], w_up_attn[l], w_up_ssm[l], w_out[l])
        x = _layer_norm(ALPHA * x + mix, ln1_g[l], ln1_b[l])
        ffn = _hier_moe(x, w_router_group[l], b_router_group[l], w_router_expert[l], b_router_expert[l], w_exp_gate[l], w_exp_up[l], w_exp_down[l])
        x = _layer_norm(ALPHA * x + ffn, ln2_g[l], ln2_b[l])
    return x


def setup_inputs(seed: int = 0) -> dict:
    key = jax.random.key(seed)
    keys = jax.random.split(key, 32)

    def nrm(k, shape, scale):
        return jax.random.normal(k, shape, jnp.float32) * scale

    G, P, H = N_SSM_GROUPS, SSM_STATE, SSM_GROUP
    n_idx = jnp.arange(P, dtype=jnp.float32)
    return {
        'x_prompt': nrm(keys[0], (BATCH, SEQ, D_MODEL), 1.0),
        'x_sample': nrm(keys[1], (DEC_BATCH, DEC_SEQ, D_MODEL), 1.0),
        'w_in': nrm(keys[2], (DEPTH, D_MODEL, D_IN_PROJ), D_MODEL ** -0.5),
        'q_norm_g': 1.0 + nrm(keys[3], (DEPTH, HEAD_DIM), 0.02),
        'k_norm_g': 1.0 + nrm(keys[4], (DEPTH, HEAD_DIM), 0.02),
        'ssm_a_re': -0.5 + nrm(keys[5], (DEPTH, 2, G, P), 0.01),
        'ssm_a_im': math.pi * n_idx + nrm(keys[6], (DEPTH, 2, G, P), 0.01),
        'ssm_log_dt': jax.random.uniform(keys[7], (DEPTH, 2, G), jnp.float32, minval=math.log(DT_MIN), maxval=math.log(DT_MAX)),
        'ssm_b_re': nrm(keys[8], (DEPTH, 2, G, P, H), (2.0 * H) ** -0.5),
        'ssm_b_im': nrm(keys[9], (DEPTH, 2, G, P, H), (2.0 * H) ** -0.5),
        'ssm_c_re': nrm(keys[10], (DEPTH, 2, G, H, P), 2.0 * P ** -0.5),
        'ssm_c_im': nrm(keys[11], (DEPTH, 2, G, H, P), 2.0 * P ** -0.5),
        'ssm_d': nrm(keys[12], (DEPTH, D_SSM), 1.0),
        'w_glu': nrm(keys[13], (DEPTH, D_SSM, D_SSM), D_SSM ** -0.5),
        'b_glu': nrm(keys[14], (DEPTH, D_SSM), 0.01),
        'w_up_attn': nrm(keys[15], (DEPTH, D_ATTN, D_MODEL), D_ATTN ** -0.5),
        'w_up_ssm': nrm(keys[16], (DEPTH, D_SSM, D_MODEL), D_SSM ** -0.5),
        'w_out': nrm(keys[17], (DEPTH, D_MODEL, D_MODEL), BETA * D_MODEL ** -0.5),
        'ln1_g': 1.0 + nrm(keys[18], (DEPTH, D_MODEL), 0.02),
        'ln1_b': nrm(keys[19], (DEPTH, D_MODEL), 0.02),
        'w_router_group': nrm(keys[20], (DEPTH, D_MODEL, N_EXPERT_GROUPS), D_MODEL ** -0.5),
        'b_router_group': nrm(keys[21], (DEPTH, N_EXPERT_GROUPS), 0.01),
        'w_router_expert': nrm(keys[22], (DEPTH, D_MODEL, N_EXPERTS), D_MODEL ** -0.5),
        'b_router_expert': nrm(keys[23], (DEPTH, N_EXPERTS), 0.01),
        'w_exp_gate': nrm(keys[24], (DEPTH, N_EXPERTS, D_MODEL, D_FF_EXPERT), D_MODEL ** -0.5),
        'w_exp_up': nrm(keys[25], (DEPTH, N_EXPERTS, D_MODEL, D_FF_EXPERT), D_MODEL ** -0.5),
        'w_exp_down': nrm(keys[26], (DEPTH, N_EXPERTS, D_FF_EXPERT, D_MODEL), BETA * D_FF_EXPERT ** -0.5),
        'ln2_g': 1.0 + nrm(keys[27], (DEPTH, D_MODEL), 0.02),
        'ln2_b': nrm(keys[28], (DEPTH, D_MODEL), 0.02),
    }


def reference(x_prompt, x_sample, w_in, q_norm_g, k_norm_g, ssm_a_re, ssm_a_im, ssm_log_dt, ssm_b_re, ssm_b_im, ssm_c_re, ssm_c_im, ssm_d, w_glu, b_glu, w_up_attn, w_up_ssm, w_out, ln1_g, ln1_b, w_router_group, b_router_group, w_router_expert, b_router_expert, w_exp_gate, w_exp_up, w_exp_down, ln2_g, ln2_b):
    weights = (w_in, q_norm_g, k_norm_g, ssm_a_re, ssm_a_im, ssm_log_dt, ssm_b_re, ssm_b_im, ssm_c_re, ssm_c_im, ssm_d, w_glu, b_glu, w_up_attn, w_up_ssm, w_out, ln1_g, ln1_b, w_router_group, b_router_group, w_router_expert, b_router_expert, w_exp_gate, w_exp_up, w_exp_down, ln2_g, ln2_b)
    y_prompt = _trunk(x_prompt, *weights)
    y_sample = _trunk(x_sample, *weights)
    return (y_prompt, y_sample)
```

```python
import functools
import math

import jax
import jax.numpy as jnp
from jax import lax
from jax.experimental import pallas as pl
from jax.experimental.pallas import tpu as pltpu

F32 = jnp.float32
BF16 = jnp.bfloat16

D_MODEL = 1024
GRID_W = 64
HEAD_DIM = 64
N_Q_HEADS = 8
N_KV_HEADS = 2
Q_PER_KV = N_Q_HEADS // N_KV_HEADS
D_ATTN = N_Q_HEADS * HEAD_DIM
D_KV = N_KV_HEADS * HEAD_DIM
ROPE_THETA = 10000.0
D_SSM = D_MODEL // 2
SSM_GROUP = 16
N_SSM_GROUPS = D_SSM // SSM_GROUP
SSM_STATE = 64
N_EXPERT_GROUPS = 4
EXPERTS_PER_GROUP = 4
N_EXPERTS = N_EXPERT_GROUPS * EXPERTS_PER_GROUP
D_FF_EXPERT = D_MODEL // 2
DEPTH = 1
ALPHA = (2.0 * DEPTH) ** 0.25
EPS = 1e-6
D_IN_PROJ = D_ATTN + 2 * D_KV + D_SSM + 2 * D_MODEL
OFF_K = D_ATTN
OFF_V = OFF_K + D_KV
OFF_U = OFF_V + D_KV
OFF_GA = OFF_U + D_SSM
OFF_GS = OFF_GA + D_MODEL

LANES = 128
SSM_CHUNK = 16
CHUNK_W = SSM_CHUNK * SSM_GROUP
ROUTE_E0 = N_EXPERT_GROUPS
VMEM_LIMIT = 56 * 1024 * 1024


def _cparams(sem):
    return pltpu.CompilerParams(dimension_semantics=sem, vmem_limit_bytes=VMEM_LIMIT)


def _dot(a, b):
    return jnp.dot(a, b, preferred_element_type=F32)


def _split_bf16(x):
    hi = x.astype(BF16)
    lo = (x - hi.astype(F32)).astype(BF16)
    return hi, lo


def _layer_norm(h, g, b):
    mu = jnp.mean(h, axis=-1, keepdims=True)
    var = jnp.mean(jnp.square(h - mu), axis=-1, keepdims=True)
    return (h - mu) * lax.rsqrt(var + EPS) * g + b


def _inproj_body(x_ref, w_ref, gq_ref, gk_ref, cos_ref, sin_ref, seg_ref,
                 q_ref, k_ref, v_ref, u_ref, ga_ref, gs_ref):
    xb = x_ref[...].astype(BF16)
    cos = cos_ref[...]
    sin = sin_ref[...]
    seg = seg_ref[...]
    lane = lax.broadcasted_iota(jnp.int32, (1, LANES), 1)
    first_half = (lane % HEAD_DIM) < (HEAD_DIM // 2)

    def mm(a, b):
        return _dot(xb, w_ref[:, a:b])

    def norm_rope(y, g):
        hi, lo = _split_bf16(y * y)
        ss = _dot(hi, seg) + _dot(lo, seg)
        yn = y * lax.rsqrt(ss * (1.0 / HEAD_DIM) + EPS) * g
        sw = jnp.where(first_half, pltpu.roll(yn, LANES - HEAD_DIM // 2, 1), pltpu.roll(yn, HEAD_DIM // 2, 1))
        return yn * cos + sw * sin

    gq = gq_ref[...]
    for j in range(D_ATTN // LANES):
        q = norm_rope(mm(j * LANES, (j + 1) * LANES), gq)
        q_ref[:, j * LANES:(j + 1) * LANES] = (q * (HEAD_DIM ** -0.5)).astype(BF16)
    k_ref[...] = norm_rope(mm(OFF_K, OFF_V), gk_ref[...]).astype(BF16)
    v_ref[...] = mm(OFF_V, OFF_U).astype(BF16)
    u_ref[...] = mm(OFF_U, OFF_GA)
    ga_ref[...] = jax.nn.sigmoid(mm(OFF_GA, OFF_GS)).astype(BF16)
    gs_ref[...] = jax.nn.sigmoid(mm(OFF_GS, D_IN_PROJ)).astype(BF16)


def _rope_tables(t_max):
    t = jnp.arange(t_max, dtype=jnp.int32)
    row_pos = (t // GRID_W).astype(F32)
    col_pos = (t % GRID_W).astype(F32)
    n_freq = HEAD_DIM // 4
    inv_freq = ROPE_THETA ** (-jnp.arange(n_freq, dtype=F32) / n_freq)
    ang = jnp.concatenate([row_pos[:, None] * inv_freq, col_pos[:, None] * inv_freq], axis=-1)
    c, s = jnp.cos(ang), jnp.sin(ang)
    reps = LANES // HEAD_DIM
    cos_t = jnp.tile(jnp.concatenate([c, c], axis=-1), (1, reps))
    sin_t = jnp.tile(jnp.concatenate([-s, s], axis=-1), (1, reps))
    return cos_t, sin_t


def _in_projection(x, w_in, q_norm_g, k_norm_g, seqs, tm):
    n = x.shape[0]
    (bp, tp), (bs, ts) = seqs
    npt = bp * tp // tm
    tiles_p, tiles_s = tp // tm, ts // tm
    cos_t, sin_t = _rope_tables(max(tp, ts))
    reps = LANES // HEAD_DIM
    gq = jnp.tile(q_norm_g.astype(F32), reps)[None, :]
    gk = jnp.tile(k_norm_g.astype(F32), reps)[None, :]
    head_of_lane = jnp.arange(LANES) // HEAD_DIM
    seg = (head_of_lane[:, None] == head_of_lane[None, :]).astype(BF16)

    def pos_map(i):
        return (jnp.where(i < npt, i % tiles_p, (i - npt) % tiles_s), 0)

    row = lambda i: (i, 0)
    const = lambda i: (0, 0)
    return pl.pallas_call(
        _inproj_body,
        grid=(n // tm,),
        in_specs=[
            pl.BlockSpec((tm, D_MODEL), row),
            pl.BlockSpec((D_MODEL, D_IN_PROJ), const),
            pl.BlockSpec((1, LANES), const),
            pl.BlockSpec((1, LANES), const),
            pl.BlockSpec((tm, LANES), pos_map),
            pl.BlockSpec((tm, LANES), pos_map),
            pl.BlockSpec((LANES, LANES), const),
        ],
        out_specs=[
            pl.BlockSpec((tm, D_ATTN), row),
            pl.BlockSpec((tm, D_KV), row),
            pl.BlockSpec((tm, D_KV), row),
            pl.BlockSpec((tm, D_SSM), row),
            pl.BlockSpec((tm, D_MODEL), row),
            pl.BlockSpec((tm, D_MODEL), row),
        ],
        out_shape=[
            jax.ShapeDtypeStruct((n, D_ATTN), BF16),
            jax.ShapeDtypeStruct((n, D_KV), BF16),
            jax.ShapeDtypeStruct((n, D_KV), BF16),
            jax.ShapeDtypeStruct((n, D_SSM), F32),
            jax.ShapeDtypeStruct((n, D_MODEL), BF16),
            jax.ShapeDtypeStruct((n, D_MODEL), BF16),
        ],
        compiler_params=_cparams(("parallel",)),
        name="in_projection",
    )(x, w_in.astype(BF16), gq, gk, cos_t, sin_t, seg)


def _attn_body(q_ref, kt_ref, v_ref, o_ref, m_sc, l_sc, acc_sc, *, tq, tk, n_kv):
    hw = HEAD_DIM
    q4 = jnp.concatenate([q_ref[:, h * hw:(h + 1) * hw] for h in range(Q_PER_KV)], axis=0)
    m_sc[...] = jnp.full(m_sc.shape, -jnp.inf, F32)
    l_sc[...] = jnp.zeros(l_sc.shape, F32)
    acc_sc[...] = jnp.zeros(acc_sc.shape, F32)

    def body(j, carry):
        kt = kt_ref[0, j]
        vv = v_ref[0, pl.ds(pl.multiple_of(j * tk, tk), tk), :]
        s = _dot(q4, kt)
        m_old = m_sc[...]
        m_new = jnp.maximum(m_old, jnp.max(s, axis=-1, keepdims=True))
        alpha = jnp.exp(m_old - m_new)
        p = jnp.exp(s - m_new)
        l_sc[...] = alpha * l_sc[...] + jnp.sum(p, axis=-1, keepdims=True)
        acc_sc[...] = alpha * acc_sc[...] + _dot(p.astype(BF16), vv)
        m_sc[...] = m_new
        return carry

    lax.fori_loop(0, n_kv, body, 0)
    out = acc_sc[...] / l_sc[...]
    for h in range(Q_PER_KV):
        o_ref[:, h * hw:(h + 1) * hw] = out[h * tq:(h + 1) * tq].astype(o_ref.dtype)


def _attention(q, kt_blocks, v_heads, b, t, tok_off, tq, tk):
    n_qt = t // tq
    n_kv = t // tk
    qoff = tok_off // tq
    seq_off = tok_off // t
    gw = Q_PER_KV * HEAD_DIM
    body = functools.partial(_attn_body, tq=tq, tk=tk, n_kv=n_kv)
    return pl.pallas_call(
        body,
        grid=(b, N_KV_HEADS, n_qt),
        in_specs=[
            pl.BlockSpec((tq, gw), lambda bi, g, qi: (qoff + bi * n_qt + qi, g)),
            pl.BlockSpec((1, n_kv, HEAD_DIM, tk), lambda bi, g, qi: (g, seq_off + bi, 0, 0)),
            pl.BlockSpec((1, t, HEAD_DIM), lambda bi, g, qi: (g, seq_off + bi, 0)),
        ],
        out_specs=pl.BlockSpec((tq, gw), lambda bi, g, qi: (bi * n_qt + qi, g)),
        out_shape=jax.ShapeDtypeStruct((b * t, D_ATTN), BF16),
        scratch_shapes=[
            pltpu.VMEM((Q_PER_KV * tq, 1), F32),
            pltpu.VMEM((Q_PER_KV * tq, 1), F32),
            pltpu.VMEM((Q_PER_KV * tq, HEAD_DIM), F32),
        ],
        compiler_params=_cparams(("parallel", "parallel", "parallel")),
        name="attention",
    )(q, kt_blocks, v_heads)


def _s5_matrices(a_re, a_im, log_dt, b_re, b_im, c_re, c_im, d_skip):
    hp = lax.Precision.HIGHEST
    L, G, P, H = SSM_CHUNK, N_SSM_GROUPS, SSM_STATE, SSM_GROUP
    f = lambda v: v.astype(F32)
    lr, li = f(a_re), f(a_im)
    dt = jnp.exp(f(log_dt))[..., None]
    mag = jnp.exp(lr * dt)
    ar = mag * jnp.cos(li * dt)
    ai = mag * jnp.sin(li * dt)
    den = lr * lr + li * li
    kr = ((ar - 1.0) * lr + ai * li) / den
    ki = (ai * lr - (ar - 1.0) * li) / den
    br, bi = f(b_re), f(b_im)
    bbr = kr[..., None] * br - ki[..., None] * bi
    bbi = kr[..., None] * bi + ki[..., None] * br
    kk = jnp.arange(L + 1, dtype=F32)[:, None, None, None]
    pmag = jnp.exp(kk * (lr * dt)[None])
    ph = kk * (li * dt)[None]
    pr = pmag * jnp.cos(ph)
    pi = pmag * jnp.sin(ph)
    cr, ci = f(c_re), f(c_im)
    car = cr[None] * pr[:, :, :, None, :] - ci[None] * pi[:, :, :, None, :]
    cai = cr[None] * pi[:, :, :, None, :] + ci[None] * pr[:, :, :, None, :]
    kern = (jnp.einsum('kdgop,dgpi->kdgoi', car, bbr, precision=hp)
            - jnp.einsum('kdgop,dgpi->kdgoi', cai, bbi, precision=hp))
    jj = jnp.arange(L)[:, None]
    ii = jnp.arange(L)[None, :]

    def toeplitz(kd, lag, valid):
        blocks = jnp.where(valid[:, :, None, None, None], kd[jnp.clip(lag, 0, L)], 0.0)
        return blocks.transpose(2, 0, 4, 1, 3).reshape(G, CHUNK_W, CHUNK_W)

    msum = toeplitz(kern[:, 0], ii - jj, ii >= jj) + toeplitz(kern[:, 1], jj - ii, jj >= ii)

    def in_op(d, powers):
        prj, pij = pr[powers, d], pi[powers, d]
        re = prj[..., None] * bbr[d][None] - pij[..., None] * bbi[d][None]
        im = prj[..., None] * bbi[d][None] + pij[..., None] * bbr[d][None]
        to = lambda m: m.transpose(1, 0, 3, 2).reshape(G, CHUNK_W, P)
        return to(re), to(im)

    def out_op(d, powers):
        to = lambda m: m.transpose(1, 3, 0, 2).reshape(G, P, CHUNK_W)
        return to(car[powers, d]), to(-cai[powers, d])

    steps = jnp.arange(L)
    bm = jnp.stack(in_op(0, L - 1 - steps) + in_op(1, steps), axis=1)
    cm = jnp.stack(out_op(0, steps + 1) + out_op(1, L - steps), axis=1)
    al = jnp.stack([pr[L, 0], pi[L, 0], pr[L, 1], pi[L, 1]], axis=1)[:, :, None, :]
    dvec = jnp.tile(f(d_skip).reshape(G, 1, H), (1, 1, L))
    return msum.astype(BF16), bm.astype(BF16), cm.astype(BF16), al, dvec


def _s5_body(up_ref, us_ref, m_ref, bm_ref, cm_ref, al_ref, d_ref, yp_ref, ys_ref, sp_sc, ss_sc,
             *, bp, ncp, bs, ncs):
    upb = up_ref[0].astype(BF16)
    usb = us_ref[0].astype(BF16)
    for d in range(4):
        sp_sc[d] = _dot(upb, bm_ref[0, d])
        ss_sc[d] = _dot(usb, bm_ref[0, d])
    al = [al_ref[0, d] for d in range(4)]

    def advance(sc, rows, d, sr, si):
        lr = sc[d, rows, :]
        li = sc[d + 1, rows, :]
        sc[d, rows, :] = sr
        sc[d + 1, rows, :] = si
        ar, ai = al[d], al[d + 1]
        return ar * sr - ai * si + lr, ar * si + ai * sr + li

    def prompt_step(k, st):
        fr, fi, br, bi = st
        rf = pl.ds(pl.multiple_of(k * bp, bp), bp)
        rb = pl.ds(pl.multiple_of((ncp - 1 - k) * bp, bp), bp)
        fr, fi = advance(sp_sc, rf, 0, fr, fi)
        br, bi = advance(sp_sc, rb, 2, br, bi)
        return fr, fi, br, bi

    def sample_step(k, st):
        out = []
        for b in range(bs):
            fr, fi, br, bi = st[4 * b:4 * b + 4]
            fr, fi = advance(ss_sc, pl.ds(b * ncs + k, 1), 0, fr, fi)
            br, bi = advance(ss_sc, pl.ds(b * ncs + ncs - 1 - k, 1), 2, br, bi)
            out += [fr, fi, br, bi]
        return tuple(out)

    zp = tuple(jnp.zeros((bp, SSM_STATE), F32) for _ in range(4))
    zs = tuple(jnp.zeros((1, SSM_STATE), F32) for _ in range(4 * bs))
    n_both = min(ncp, ncs)

    def both(k, st):
        return prompt_step(k, st[0]), sample_step(k, st[1])

    stp, sts = lax.fori_loop(0, n_both, both, (zp, zs))
    if ncp > n_both:
        lax.fori_loop(n_both, ncp, prompt_step, stp)
    if ncs > n_both:
        lax.fori_loop(n_both, ncs, sample_step, sts)

    dvec = d_ref[0]
    for u_ref, ub, sc, y_ref in ((up_ref, upb, sp_sc, yp_ref), (us_ref, usb, ss_sc, ys_ref)):
        y = _dot(ub, m_ref[0]) + u_ref[0] * dvec
        for d in range(4):
            y = y + _dot(sc[d].astype(BF16), cm_ref[0, d])
        y_ref[0] = y


def _s5_scan(u_p, u_s, mats, bp, ncp, bs, ncs):
    msum, bm, cm, al, dvec = mats
    rp, rs = ncp * bp, bs * ncs
    g3 = lambda g: (g, 0, 0)
    g4 = lambda g: (g, 0, 0, 0)
    body = functools.partial(_s5_body, bp=bp, ncp=ncp, bs=bs, ncs=ncs)
    return pl.pallas_call(
        body,
        grid=(N_SSM_GROUPS,),
        in_specs=[
            pl.BlockSpec((1, rp, CHUNK_W), g3),
            pl.BlockSpec((1, rs, CHUNK_W), g3),
            pl.BlockSpec((1, CHUNK_W, CHUNK_W), g3),
            pl.BlockSpec((1, 4, CHUNK_W, SSM_STATE), g4),
            pl.BlockSpec((1, 4, SSM_STATE, CHUNK_W), g4),
            pl.BlockSpec((1, 4, 1, SSM_STATE), g4),
            pl.BlockSpec((1, 1, CHUNK_W), g3),
        ],
        out_specs=[pl.BlockSpec((1, rp, CHUNK_W), g3), pl.BlockSpec((1, rs, CHUNK_W), g3)],
        out_shape=[jax.ShapeDtypeStruct((N_SSM_GROUPS, rp, CHUNK_W), F32),
                   jax.ShapeDtypeStruct((N_SSM_GROUPS, rs, CHUNK_W), F32)],
        scratch_shapes=[pltpu.VMEM((4, rp, SSM_STATE), F32), pltpu.VMEM((4, rs, SSM_STATE), F32)],
        compiler_params=_cparams(("parallel",)),
        name="s5_scan",
    )(u_p, u_s, msum, bm, cm, al, dvec)


def _route(logits):
    lane = lax.broadcasted_iota(jnp.int32, logits.shape, 1)
    neg = -jnp.inf
    big = jnp.int32(LANES)
    is_g = lane < N_EXPERT_GROUPS
    gl = jnp.where(is_g, logits, neg)
    gmax = jnp.max(gl, axis=-1, keepdims=True)
    gidx = jnp.min(jnp.where(gl == gmax, lane, big), axis=-1, keepdims=True)
    g_w = 1.0 / jnp.sum(jnp.where(is_g, jnp.exp(logits - gmax), 0.0), axis=-1, keepdims=True)
    e_lane = lane - ROUTE_E0
    in_group = (e_lane >= 0) & (e_lane < N_EXPERTS) & ((e_lane // EXPERTS_PER_GROUP) == gidx)
    el = jnp.where(in_group, logits, neg)
    v1 = jnp.max(el, axis=-1, keepdims=True)
    i1 = jnp.min(jnp.where(el == v1, lane, big), axis=-1, keepdims=True)
    el2 = jnp.where(lane == i1, neg, el)
    v2 = jnp.max(el2, axis=-1, keepdims=True)
    i2 = jnp.min(jnp.where(el2 == v2, lane, big), axis=-1, keepdims=True)
    e2 = jnp.exp(v2 - v1)
    inv = g_w / (1.0 + e2)
    return jnp.where(lane == i1, inv, 0.0) + jnp.where(lane == i2, e2 * inv, 0.0)


def _merge_body(x_ref, o_ref, y_ref, ga_ref, gs_ref, wua_ref, wglu_ref, bglu_ref, wus_ref, wout_ref,
                g1_ref, b1_ref, wrh_ref, wrl_ref, br_ref, x1_ref, route_ref):
    z = jax.nn.gelu(y_ref[...])
    z = z * jax.nn.sigmoid(_dot(z.astype(BF16), wglu_ref[...]) + bglu_ref[...])
    ssm_up = _dot(z.astype(BF16), wus_ref[...])
    att_up = _dot(o_ref[...], wua_ref[...])
    merged = ga_ref[...].astype(F32) * att_up + gs_ref[...].astype(F32) * ssm_up
    mix = _dot(merged.astype(BF16), wout_ref[...])
    x1 = _layer_norm(ALPHA * x_ref[...] + mix, g1_ref[...], b1_ref[...])
    x1_ref[...] = x1
    xh, xl = _split_bf16(x1)
    logits = _dot(xh, wrh_ref[...]) + (_dot(xh, wrl_ref[...]) + _dot(xl, wrh_ref[...])) + br_ref[...]
    route_ref[...] = _route(logits)


def _merge(x, o, y, ga, gs, w_up_attn, w_glu, b_glu, w_up_ssm, w_out, ln1_g, ln1_b,
           w_router_group, b_router_group, w_router_expert, b_router_expert, tm):
    n = x.shape[0]
    n_r = N_EXPERT_GROUPS + N_EXPERTS
    wr = jnp.zeros((D_MODEL, LANES), F32).at[:, :n_r].set(
        jnp.concatenate([w_router_group, w_router_expert], axis=1).astype(F32))
    br = jnp.zeros((1, LANES), F32).at[0, :n_r].set(
        jnp.concatenate([b_router_group, b_router_expert]).astype(F32))
    wrh, wrl = _split_bf16(wr)
    row = lambda i: (i, 0)
    const = lambda i: (0, 0)
    full = lambda a: pl.BlockSpec(a.shape, const)
    weights = [w_up_attn.astype(BF16), w_glu.astype(BF16), b_glu.astype(F32)[None, :], w_up_ssm.astype(BF16),
               w_out.astype(BF16), ln1_g.astype(F32)[None, :], ln1_b.astype(F32)[None, :], wrh, wrl, br]
    return pl.pallas_call(
        _merge_body,
        grid=(n // tm,),
        in_specs=[pl.BlockSpec((tm, D_MODEL), row), pl.BlockSpec((tm, D_ATTN), row), pl.BlockSpec((tm, D_SSM), row),
                  pl.BlockSpec((tm, D_MODEL), row), pl.BlockSpec((tm, D_MODEL), row)] + [full(w) for w in weights],
        out_specs=[pl.BlockSpec((tm, D_MODEL), row), pl.BlockSpec((tm, LANES), row)],
        out_shape=[jax.ShapeDtypeStruct((n, D_MODEL), F32), jax.ShapeDtypeStruct((n, LANES), F32)],
        compiler_params=_cparams(("parallel",)),
        name="merge_ln1_router",
    )(x, o, y, ga, gs, *weights)


def _moe_body(x_ref, route_ref, wg_ref, wu_ref, wd_ref, g2_ref, b2_ref, o_ref, acc_sc, xb_sc):
    e = pl.program_id(1)

    @pl.when(e == 0)
    def _():
        acc_sc[...] = jnp.zeros(acc_sc.shape, F32)
        xb_sc[...] = x_ref[...].astype(BF16)

    xb = xb_sc[...]
    h = jax.nn.silu(_dot(xb, wg_ref[0])) * _dot(xb, wu_ref[0])
    y = _dot(h.astype(BF16), wd_ref[0])
    route = route_ref[...]
    lane = lax.broadcasted_iota(jnp.int32, route.shape, 1)
    w_e = jnp.sum(jnp.where(lane == e + ROUTE_E0, route, 0.0), axis=-1, keepdims=True)
    acc_sc[...] += w_e * y

    @pl.when(e == N_EXPERTS - 1)
    def _():
        o_ref[...] = _layer_norm(ALPHA * x_ref[...] + acc_sc[...], g2_ref[...], b2_ref[...])


def _moe(x1, route, w_gate, w_up, w_down, ln2_g, ln2_b, tm):
    n = x1.shape[0]
    row = lambda i, e: (i, 0)
    const = lambda i, e: (0, 0)
    exp3 = lambda i, e: (e, 0, 0)
    return pl.pallas_call(
        _moe_body,
        grid=(n // tm, N_EXPERTS),
        in_specs=[pl.BlockSpec((tm, D_MODEL), row), pl.BlockSpec((tm, LANES), row),
                  pl.BlockSpec((1, D_MODEL, D_FF_EXPERT), exp3), pl.BlockSpec((1, D_MODEL, D_FF_EXPERT), exp3),
                  pl.BlockSpec((1, D_FF_EXPERT, D_MODEL), exp3),
                  pl.BlockSpec((1, D_MODEL), const), pl.BlockSpec((1, D_MODEL), const)],
        out_specs=pl.BlockSpec((tm, D_MODEL), row),
        out_shape=jax.ShapeDtypeStruct((n, D_MODEL), F32),
        scratch_shapes=[pltpu.VMEM((tm, D_MODEL), F32), pltpu.VMEM((tm, D_MODEL), BF16)],
        compiler_params=_cparams(("parallel", "arbitrary")),
        name="experts_ln2",
    )(x1, route, w_gate.astype(BF16), w_up.astype(BF16), w_down.astype(BF16),
      ln2_g.astype(F32)[None, :], ln2_b.astype(F32)[None, :])


def _layer(x_prompt, x_sample, w, *, tm, tq, tk, tm_moe):
    bp, tp, _ = x_prompt.shape
    bs, ts, _ = x_sample.shape
    n_p, n_s = bp * tp, bs * ts
    n = n_p + n_s
    x = jnp.concatenate([x_prompt.reshape(n_p, D_MODEL), x_sample.reshape(n_s, D_MODEL)], axis=0).astype(F32)

    q, k, v, u, ga, gs = _in_projection(x, w['w_in'], w['q_norm_g'], w['k_norm_g'], ((bp, tp), (bs, ts)), tm)

    kt_blocks = k.reshape(n // tk, tk, N_KV_HEADS, HEAD_DIM).transpose(2, 0, 3, 1)
    v_heads = v.reshape(n, N_KV_HEADS, HEAD_DIM).transpose(1, 0, 2)
    o = jnp.concatenate([_attention(q, kt_blocks, v_heads, bp, tp, 0, tq, tk),
                         _attention(q, kt_blocks, v_heads, bs, ts, n_p, tq, tk)], axis=0)

    L, G, H = SSM_CHUNK, N_SSM_GROUPS, SSM_GROUP
    ncp, ncs = tp // L, ts // L
    u_p = u[:n_p].reshape(bp, ncp, L, G, H).transpose(3, 1, 0, 2, 4).reshape(G, ncp * bp, CHUNK_W)
    u_s = u[n_p:].reshape(bs, ncs, L, G, H).transpose(3, 0, 1, 2, 4).reshape(G, bs * ncs, CHUNK_W)
    mats = _s5_matrices(w['ssm_a_re'], w['ssm_a_im'], w['ssm_log_dt'], w['ssm_b_re'], w['ssm_b_im'],
                        w['ssm_c_re'], w['ssm_c_im'], w['ssm_d'])
    y_p, y_s = _s5_scan(u_p, u_s, mats, bp, ncp, bs, ncs)
    y = jnp.concatenate([
        y_p.reshape(G, ncp, bp, L, H).transpose(2, 1, 3, 0, 4).reshape(n_p, D_SSM),
        y_s.reshape(G, bs, ncs, L, H).transpose(1, 2, 3, 0, 4).reshape(n_s, D_SSM)], axis=0)

    x1, route = _merge(x, o, y, ga, gs, w['w_up_attn'], w['w_glu'], w['b_glu'], w['w_up_ssm'], w['w_out'],
                       w['ln1_g'], w['ln1_b'], w['w_router_group'], w['b_router_group'],
                       w['w_router_expert'], w['b_router_expert'], tm)
    out = _moe(x1, route, w['w_exp_gate'], w['w_exp_up'], w['w_exp_down'], w['ln2_g'], w['ln2_b'], tm_moe)
    return out[:n_p].reshape(bp, tp, D_MODEL), out[n_p:].reshape(bs, ts, D_MODEL)


def _tile(limit, *sizes):
    return math.gcd(limit, *sizes)


def kernel(x_prompt, x_sample, w_in, q_norm_g, k_norm_g, ssm_a_re, ssm_a_im, ssm_log_dt, ssm_b_re, ssm_b_im, ssm_c_re, ssm_c_im, ssm_d, w_glu, b_glu, w_up_attn, w_up_ssm, w_out, ln1_g, ln1_b, w_router_group, b_router_group, w_router_expert, b_router_expert, w_exp_gate, w_exp_up, w_exp_down, ln2_g, ln2_b):
    w = dict(w_in=w_in, q_norm_g=q_norm_g, k_norm_g=k_norm_g, ssm_a_re=ssm_a_re, ssm_a_im=ssm_a_im,
             ssm_log_dt=ssm_log_dt, ssm_b_re=ssm_b_re, ssm_b_im=ssm_b_im, ssm_c_re=ssm_c_re, ssm_c_im=ssm_c_im,
             ssm_d=ssm_d, w_glu=w_glu, b_glu=b_glu, w_up_attn=w_up_attn, w_up_ssm=w_up_ssm, w_out=w_out,
             ln1_g=ln1_g, ln1_b=ln1_b, w_router_group=w_router_group, b_router_group=b_router_group,
             w_router_expert=w_router_expert, b_router_expert=b_router_expert, w_exp_gate=w_exp_gate,
             w_exp_up=w_exp_up, w_exp_down=w_exp_down, ln2_g=ln2_g, ln2_b=ln2_b)
    w = {name: val[0] for name, val in w.items()}
    tp, ts = x_prompt.shape[1], x_sample.shape[1]
    return _layer(x_prompt, x_sample, w,
                  tm=_tile(512, tp, ts), tq=_tile(256, tp, ts), tk=_tile(512, tp, ts), tm_moe=_tile(1024, tp, ts))
```

```python
import functools
import math

import jax
import jax.numpy as jnp
from jax import lax
from jax.experimental import pallas as pl
from jax.experimental.pallas import tpu as pltpu

F32 = jnp.float32
BF16 = jnp.bfloat16

D_MODEL = 1024
GRID_W = 64
HEAD_DIM = 64
N_Q_HEADS = 8
N_KV_HEADS = 2
Q_PER_KV = N_Q_HEADS // N_KV_HEADS
D_ATTN = N_Q_HEADS * HEAD_DIM
D_KV = N_KV_HEADS * HEAD_DIM
ROPE_THETA = 10000.0
D_SSM = D_MODEL // 2
SSM_GROUP = 16
N_SSM_GROUPS = D_SSM // SSM_GROUP
SSM_STATE = 64
N_EXPERT_GROUPS = 4
EXPERTS_PER_GROUP = 4
N_EXPERTS = N_EXPERT_GROUPS * EXPERTS_PER_GROUP
D_FF_EXPERT = D_MODEL // 2
DEPTH = 1
ALPHA = (2.0 * DEPTH) ** 0.25
EPS = 1e-6
D_IN_PROJ = D_ATTN + 2 * D_KV + D_SSM + 2 * D_MODEL
OFF_K = D_ATTN
OFF_V = OFF_K + D_KV
OFF_U = OFF_V + D_KV
OFF_GA = OFF_U + D_SSM
OFF_GS = OFF_GA + D_MODEL

LANES = 128
SSM_CHUNK = 16
CHUNK_W = SSM_CHUNK * SSM_GROUP
ROUTE_E0 = N_EXPERT_GROUPS
VMEM_LIMIT = 56 * 1024 * 1024


def _cparams(sem):
    return pltpu.CompilerParams(dimension_semantics=sem, vmem_limit_bytes=VMEM_LIMIT)


def _dot(a, b):
    return jnp.dot(a, b, preferred_element_type=F32)


def _split_bf16(x):
    hi = x.astype(BF16)
    lo = (x - hi.astype(F32)).astype(BF16)
    return hi, lo


def _layer_norm(h, g, b):
    mu = jnp.mean(h, axis=-1, keepdims=True)
    var = jnp.mean(jnp.square(h - mu), axis=-1, keepdims=True)
    return (h - mu) * lax.rsqrt(var + EPS) * g + b


def _inproj_body(x_ref, w_ref, gq_ref, gk_ref, cos_ref, sin_ref, seg_ref,
                 q_ref, k_ref, v_ref, u_ref, ga_ref, gs_ref):
    xb = x_ref[...].astype(BF16)
    cos = cos_ref[...]
    sin = sin_ref[...]
    seg = seg_ref[...]
    lane = lax.broadcasted_iota(jnp.int32, (1, LANES), 1)
    first_half = (lane % HEAD_DIM) < (HEAD_DIM // 2)

    def mm(a, b):
        return _dot(xb, w_ref[:, a:b])

    def norm_rope(y, g):
        hi, lo = _split_bf16(y * y)
        ss = _dot(hi, seg) + _dot(lo, seg)
        yn = y * lax.rsqrt(ss * (1.0 / HEAD_DIM) + EPS) * g
        sw = jnp.where(first_half, pltpu.roll(yn, LANES - HEAD_DIM // 2, 1), pltpu.roll(yn, HEAD_DIM // 2, 1))
        return yn * cos + sw * sin

    gq = gq_ref[...]
    for j in range(D_ATTN // LANES):
        q = norm_rope(mm(j * LANES, (j + 1) * LANES), gq)
        q_ref[:, j * LANES:(j + 1) * LANES] = (q * (HEAD_DIM ** -0.5)).astype(BF16)
    k_ref[...] = norm_rope(mm(OFF_K, OFF_V), gk_ref[...]).astype(BF16)
    v_ref[...] = mm(OFF_V, OFF_U).astype(BF16)
    u_ref[...] = mm(OFF_U, OFF_GA)
    ga_ref[...] = jax.nn.sigmoid(mm(OFF_GA, OFF_GS)).astype(BF16)
    gs_ref[...] = jax.nn.sigmoid(mm(OFF_GS, D_IN_PROJ)).astype(BF16)


def _rope_tables(t_max):
    t = jnp.arange(t_max, dtype=jnp.int32)
    row_pos = (t // GRID_W).astype(F32)
    col_pos = (t % GRID_W).astype(F32)
    n_freq = HEAD_DIM // 4
    inv_freq = ROPE_THETA ** (-jnp.arange(n_freq, dtype=F32) / n_freq)
    ang = jnp.concatenate([row_pos[:, None] * inv_freq, col_pos[:, None] * inv_freq], axis=-1)
    c, s = jnp.cos(ang), jnp.sin(ang)
    reps = LANES // HEAD_DIM
    cos_t = jnp.tile(jnp.concatenate([c, c], axis=-1), (1, reps))
    sin_t = jnp.tile(jnp.concatenate([-s, s], axis=-1), (1, reps))
    return cos_t, sin_t


def _in_projection(x, w_in, q_norm_g, k_norm_g, seqs, tm):
    n = x.shape[0]
    (bp, tp), (bs, ts) = seqs
    npt = bp * tp // tm
    tiles_p, tiles_s = tp // tm, ts // tm
    cos_t, sin_t = _rope_tables(max(tp, ts))
    reps = LANES // HEAD_DIM
    gq = jnp.tile(q_norm_g.astype(F32), reps)[None, :]
    gk = jnp.tile(k_norm_g.astype(F32), reps)[None, :]
    head_of_lane = jnp.arange(LANES) // HEAD_DIM
    seg = (head_of_lane[:, None] == head_of_lane[None, :]).astype(BF16)

    def pos_map(i):
        return (jnp.where(i < npt, i % tiles_p, (i - npt) % tiles_s), 0)

    row = lambda i: (i, 0)
    const = lambda i: (0, 0)
    return pl.pallas_call(
        _inproj_body,
        grid=(n // tm,),
        in_specs=[
            pl.BlockSpec((tm, D_MODEL), row),
            pl.BlockSpec((D_MODEL, D_IN_PROJ), const),
            pl.BlockSpec((1, LANES), const),
            pl.BlockSpec((1, LANES), const),
            pl.BlockSpec((tm, LANES), pos_map),
            pl.BlockSpec((tm, LANES), pos_map),
            pl.BlockSpec((LANES, LANES), const),
        ],
        out_specs=[
            pl.BlockSpec((tm, D_ATTN), row),
            pl.BlockSpec((tm, D_KV), row),
            pl.BlockSpec((tm, D_KV), row),
            pl.BlockSpec((tm, D_SSM), row),
            pl.BlockSpec((tm, D_MODEL), row),
            pl.BlockSpec((tm, D_MODEL), row),
        ],
        out_shape=[
            jax.ShapeDtypeStruct((n, D_ATTN), BF16),
            jax.ShapeDtypeStruct((n, D_KV), BF16),
            jax.ShapeDtypeStruct((n, D_KV), BF16),
            jax.ShapeDtypeStruct((n, D_SSM), F32),
            jax.ShapeDtypeStruct((n, D_MODEL), BF16),
            jax.ShapeDtypeStruct((n, D_MODEL), BF16),
        ],
        compiler_params=_cparams(("parallel",)),
        name="in_projection",
    )(x, w_in.astype(BF16), gq, gk, cos_t, sin_t, seg)


V_ROWS = HEAD_DIM + 16


def _attn_body(qt_ref, k_ref, vt_ref, o_ref, m_sc, acc_sc, s_sc, *, tq, tk, n_kv):
    q4t = jnp.concatenate([qt_ref[h] for h in range(Q_PER_KV)], axis=1)
    m_sc[...] = jnp.full(m_sc.shape, -jnp.inf, F32)
    acc_sc[...] = jnp.zeros(acc_sc.shape, F32)

    def scores(j):
        kk = k_ref[0, pl.ds(pl.multiple_of(j * tk, tk), tk), :]
        return _dot(kk, q4t)

    def softmax_pv(s, j):
        m_old = m_sc[...]
        m_new = jnp.maximum(m_old, jnp.max(s, axis=0, keepdims=True))
        p = jnp.exp(s - m_new).astype(BF16)
        acc_sc[...] = jnp.exp(m_old - m_new) * acc_sc[...] + _dot(vt_ref[0, j], p)
        m_sc[...] = m_new

    s_sc[0] = scores(0)

    def body(jj, carry):
        j = 2 * jj
        s_sc[1] = scores(j + 1)
        softmax_pv(s_sc[0], j)
        s_sc[0] = scores(jnp.minimum(j + 2, n_kv - 1))
        softmax_pv(s_sc[1], j + 1)
        return carry

    lax.fori_loop(0, n_kv // 2, body, 0)
    out = acc_sc[0:HEAD_DIM, :] / acc_sc[HEAD_DIM:HEAD_DIM + 1, :]
    for h in range(Q_PER_KV):
        o_ref[h] = out[:, h * tq:(h + 1) * tq].astype(o_ref.dtype)


def _attention(qt, k_heads, vt_blocks, b, t, tok_off, tq, tk):
    n_qt = t // tq
    n_kv = t // tk
    qoff = tok_off // tq
    seq_off = tok_off // t
    body = functools.partial(_attn_body, tq=tq, tk=tk, n_kv=n_kv)
    return pl.pallas_call(
        body,
        grid=(b, N_KV_HEADS, n_qt),
        in_specs=[
            pl.BlockSpec((Q_PER_KV, HEAD_DIM, tq), lambda bi, g, qi: (g, 0, qoff + bi * n_qt + qi)),
            pl.BlockSpec((1, t, HEAD_DIM), lambda bi, g, qi: (g, seq_off + bi, 0)),
            pl.BlockSpec((1, n_kv, V_ROWS, tk), lambda bi, g, qi: (g, seq_off + bi, 0, 0)),
        ],
        out_specs=pl.BlockSpec((Q_PER_KV, HEAD_DIM, tq), lambda bi, g, qi: (g, 0, bi * n_qt + qi)),
        out_shape=jax.ShapeDtypeStruct((N_Q_HEADS, HEAD_DIM, b * t), BF16),
        scratch_shapes=[
            pltpu.VMEM((1, Q_PER_KV * tq), F32),
            pltpu.VMEM((V_ROWS, Q_PER_KV * tq), F32),
            pltpu.VMEM((2, tk, Q_PER_KV * tq), F32),
        ],
        compiler_params=_cparams(("parallel", "parallel", "parallel")),
        name="attention",
    )(qt, k_heads, vt_blocks)


def _s5_matrices(a_re, a_im, log_dt, b_re, b_im, c_re, c_im, d_skip):
    hp = lax.Precision.HIGHEST
    L, G, P, H = SSM_CHUNK, N_SSM_GROUPS, SSM_STATE, SSM_GROUP
    f = lambda v: v.astype(F32)
    lr, li = f(a_re), f(a_im)
    dt = jnp.exp(f(log_dt))[..., None]
    mag = jnp.exp(lr * dt)
    ar = mag * jnp.cos(li * dt)
    ai = mag * jnp.sin(li * dt)
    den = lr * lr + li * li
    kr = ((ar - 1.0) * lr + ai * li) / den
    ki = (ai * lr - (ar - 1.0) * li) / den
    br, bi = f(b_re), f(b_im)
    bbr = kr[..., None] * br - ki[..., None] * bi
    bbi = kr[..., None] * bi + ki[..., None] * br
    kk = jnp.arange(L + 1, dtype=F32)[:, None, None, None]
    pmag = jnp.exp(kk * (lr * dt)[None])
    ph = kk * (li * dt)[None]
    pr = pmag * jnp.cos(ph)
    pi = pmag * jnp.sin(ph)
    cr, ci = f(c_re), f(c_im)
    car = cr[None] * pr[:, :, :, None, :] - ci[None] * pi[:, :, :, None, :]
    cai = cr[None] * pi[:, :, :, None, :] + ci[None] * pr[:, :, :, None, :]
    kern = (jnp.einsum('kdgop,dgpi->kdgoi', car, bbr, precision=hp)
            - jnp.einsum('kdgop,dgpi->kdgoi', cai, bbi, precision=hp))
    jj = jnp.arange(L)[:, None]
    ii = jnp.arange(L)[None, :]

    def toeplitz(kd, lag, valid):
        blocks = jnp.where(valid[:, :, None, None, None], kd[jnp.clip(lag, 0, L)], 0.0)
        return blocks.transpose(2, 0, 4, 1, 3).reshape(G, CHUNK_W, CHUNK_W)

    msum = toeplitz(kern[:, 0], ii - jj, ii >= jj) + toeplitz(kern[:, 1], jj - ii, jj >= ii)

    def in_op(d, powers):
        prj, pij = pr[powers, d], pi[powers, d]
        re = prj[..., None] * bbr[d][None] - pij[..., None] * bbi[d][None]
        im = prj[..., None] * bbi[d][None] + pij[..., None] * bbr[d][None]
        to = lambda m: m.transpose(1, 0, 3, 2).reshape(G, CHUNK_W, P)
        return to(re), to(im)

    def out_op(d, powers):
        to = lambda m: m.transpose(1, 3, 0, 2).reshape(G, P, CHUNK_W)
        return to(car[powers, d]), to(-cai[powers, d])

    steps = jnp.arange(L)
    bm = jnp.stack(in_op(0, L - 1 - steps) + in_op(1, steps), axis=1)
    cm = jnp.stack(out_op(0, steps + 1) + out_op(1, L - steps), axis=1)
    al = jnp.stack([pr[L, 0], pi[L, 0], pr[L, 1], pi[L, 1]], axis=1)[:, :, None, :]
    dvec = jnp.tile(f(d_skip).reshape(G, 1, H), (1, 1, L))
    return msum.astype(BF16), bm.astype(BF16), cm.astype(BF16), al, dvec


def _s5_body(up_ref, us_ref, m_ref, bm_ref, cm_ref, al_ref, d_ref, yp_ref, ys_ref, sp_sc, ss_sc,
             *, bp, ncp, bs, ncs):
    upb = up_ref[0].astype(BF16)
    usb = us_ref[0].astype(BF16)
    for d in range(4):
        sp_sc[d] = _dot(upb, bm_ref[0, d])
        ss_sc[d] = _dot(usb, bm_ref[0, d])
    al = [al_ref[0, d] for d in range(4)]

    def advance(sc, rows, d, sr, si):
        lr = sc[d, rows, :]
        li = sc[d + 1, rows, :]
        sc[d, rows, :] = sr
        sc[d + 1, rows, :] = si
        ar, ai = al[d], al[d + 1]
        return ar * sr - ai * si + lr, ar * si + ai * sr + li

    def prompt_step(k, st):
        fr, fi, br, bi = st
        rf = pl.ds(pl.multiple_of(k * bp, bp), bp)
        rb = pl.ds(pl.multiple_of((ncp - 1 - k) * bp, bp), bp)
        fr, fi = advance(sp_sc, rf, 0, fr, fi)
        br, bi = advance(sp_sc, rb, 2, br, bi)
        return fr, fi, br, bi

    def sample_step(k, st):
        out = []
        for b in range(bs):
            fr, fi, br, bi = st[4 * b:4 * b + 4]
            fr, fi = advance(ss_sc, pl.ds(b * ncs + k, 1), 0, fr, fi)
            br, bi = advance(ss_sc, pl.ds(b * ncs + ncs - 1 - k, 1), 2, br, bi)
            out += [fr, fi, br, bi]
        return tuple(out)

    zp = tuple(jnp.zeros((bp, SSM_STATE), F32) for _ in range(4))
    zs = tuple(jnp.zeros((1, SSM_STATE), F32) for _ in range(4 * bs))
    n_both = min(ncp, ncs)

    def both(k, st):
        return prompt_step(k, st[0]), sample_step(k, st[1])

    stp, sts = lax.fori_loop(0, n_both, both, (zp, zs))
    if ncp > n_both:
        lax.fori_loop(n_both, ncp, prompt_step, stp)
    if ncs > n_both:
        lax.fori_loop(n_both, ncs, sample_step, sts)

    dvec = d_ref[0]
    for u_ref, ub, sc, y_ref in ((up_ref, upb, sp_sc, yp_ref), (us_ref, usb, ss_sc, ys_ref)):
        y = _dot(ub, m_ref[0]) + u_ref[0] * dvec
        for d in range(4):
            y = y + _dot(sc[d].astype(BF16), cm_ref[0, d])
        y_ref[0] = y


def _s5_scan(u_p, u_s, mats, bp, ncp, bs, ncs):
    msum, bm, cm, al, dvec = mats
    rp, rs = ncp * bp, bs * ncs
    g3 = lambda g: (g, 0, 0)
    g4 = lambda g: (g, 0, 0, 0)
    body = functools.partial(_s5_body, bp=bp, ncp=ncp, bs=bs, ncs=ncs)
    return pl.pallas_call(
        body,
        grid=(N_SSM_GROUPS,),
        in_specs=[
            pl.BlockSpec((1, rp, CHUNK_W), g3),
            pl.BlockSpec((1, rs, CHUNK_W), g3),
            pl.BlockSpec((1, CHUNK_W, CHUNK_W), g3),
            pl.BlockSpec((1, 4, CHUNK_W, SSM_STATE), g4),
            pl.BlockSpec((1, 4, SSM_STATE, CHUNK_W), g4),
            pl.BlockSpec((1, 4, 1, SSM_STATE), g4),
            pl.BlockSpec((1, 1, CHUNK_W), g3),
        ],
        out_specs=[pl.BlockSpec((1, rp, CHUNK_W), g3), pl.BlockSpec((1, rs, CHUNK_W), g3)],
        out_shape=[jax.ShapeDtypeStruct((N_SSM_GROUPS, rp, CHUNK_W), F32),
                   jax.ShapeDtypeStruct((N_SSM_GROUPS, rs, CHUNK_W), F32)],
        scratch_shapes=[pltpu.VMEM((4, rp, SSM_STATE), F32), pltpu.VMEM((4, rs, SSM_STATE), F32)],
        compiler_params=_cparams(("parallel",)),
        name="s5_scan",
    )(u_p, u_s, msum, bm, cm, al, dvec)


def _route(logits):
    lane = lax.broadcasted_iota(jnp.int32, logits.shape, 1)
    neg = -jnp.inf
    big = jnp.int32(LANES)
    is_g = lane < N_EXPERT_GROUPS
    gl = jnp.where(is_g, logits, neg)
    gmax = jnp.max(gl, axis=-1, keepdims=True)
    gidx = jnp.min(jnp.where(gl == gmax, lane, big), axis=-1, keepdims=True)
    g_w = 1.0 / jnp.sum(jnp.where(is_g, jnp.exp(logits - gmax), 0.0), axis=-1, keepdims=True)
    e_lane = lane - ROUTE_E0
    in_group = (e_lane >= 0) & (e_lane < N_EXPERTS) & ((e_lane // EXPERTS_PER_GROUP) == gidx)
    el = jnp.where(in_group, logits, neg)
    v1 = jnp.max(el, axis=-1, keepdims=True)
    i1 = jnp.min(jnp.where(el == v1, lane, big), axis=-1, keepdims=True)
    el2 = jnp.where(lane == i1, neg, el)
    v2 = jnp.max(el2, axis=-1, keepdims=True)
    i2 = jnp.min(jnp.where(el2 == v2, lane, big), axis=-1, keepdims=True)
    e2 = jnp.exp(v2 - v1)
    inv = g_w / (1.0 + e2)
    return jnp.where(lane == i1, inv, 0.0) + jnp.where(lane == i2, e2 * inv, 0.0)


def _merge_body(x_ref, o_ref, y_ref, ga_ref, gs_ref, wua_ref, wglu_ref, bglu_ref, wus_ref, wout_ref,
                g1_ref, b1_ref, wrh_ref, wrl_ref, br_ref, x1_ref, route_ref):
    z = jax.nn.gelu(y_ref[...])
    z = z * jax.nn.sigmoid(_dot(z.astype(BF16), wglu_ref[...]) + bglu_ref[...])
    ssm_up = _dot(z.astype(BF16), wus_ref[...])
    att_up = _dot(o_ref[...], wua_ref[...])
    merged = ga_ref[...].astype(F32) * att_up + gs_ref[...].astype(F32) * ssm_up
    mix = _dot(merged.astype(BF16), wout_ref[...])
    x1 = _layer_norm(ALPHA * x_ref[...] + mix, g1_ref[...], b1_ref[...])
    x1_ref[...] = x1
    xh, xl = _split_bf16(x1)
    logits = _dot(xh, wrh_ref[...]) + (_dot(xh, wrl_ref[...]) + _dot(xl, wrh_ref[...])) + br_ref[...]
    route_ref[...] = _route(logits)


def _merge(x, o, y, ga, gs, w_up_attn, w_glu, b_glu, w_up_ssm, w_out, ln1_g, ln1_b,
           w_router_group, b_router_group, w_router_expert, b_router_expert, tm):
    n = x.shape[0]
    n_r = N_EXPERT_GROUPS + N_EXPERTS
    wr = jnp.zeros((D_MODEL, LANES), F32).at[:, :n_r].set(
        jnp.concatenate([w_router_group, w_router_expert], axis=1).astype(F32))
    br = jnp.zeros((1, LANES), F32).at[0, :n_r].set(
        jnp.concatenate([b_router_group, b_router_expert]).astype(F32))
    wrh, wrl = _split_bf16(wr)
    row = lambda i: (i, 0)
    const = lambda i: (0, 0)
    full = lambda a: pl.BlockSpec(a.shape, const)
    weights = [w_up_attn.astype(BF16), w_glu.astype(BF16), b_glu.astype(F32)[None, :], w_up_ssm.astype(BF16),
               w_out.astype(BF16), ln1_g.astype(F32)[None, :], ln1_b.astype(F32)[None, :], wrh, wrl, br]
    return pl.pallas_call(
        _merge_body,
        grid=(n // tm,),
        in_specs=[pl.BlockSpec((tm, D_MODEL), row), pl.BlockSpec((tm, D_ATTN), row), pl.BlockSpec((tm, D_SSM), row),
                  pl.BlockSpec((tm, D_MODEL), row), pl.BlockSpec((tm, D_MODEL), row)] + [full(w) for w in weights],
        out_specs=[pl.BlockSpec((tm, D_MODEL), row), pl.BlockSpec((tm, LANES), row)],
        out_shape=[jax.ShapeDtypeStruct((n, D_MODEL), F32), jax.ShapeDtypeStruct((n, LANES), F32)],
        compiler_params=_cparams(("parallel",)),
        name="merge_ln1_router",
    )(x, o, y, ga, gs, *weights)


def _moe_body(x_ref, route_ref, wg_ref, wu_ref, wd_ref, g2_ref, b2_ref, o_ref, acc_sc, xb_sc):
    e = pl.program_id(1)

    @pl.when(e == 0)
    def _():
        acc_sc[...] = jnp.zeros(acc_sc.shape, F32)
        xb_sc[...] = x_ref[...].astype(BF16)

    xb = xb_sc[...]
    h = jax.nn.silu(_dot(xb, wg_ref[0])) * _dot(xb, wu_ref[0])
    y = _dot(h.astype(BF16), wd_ref[0])
    route = route_ref[...]
    lane = lax.broadcasted_iota(jnp.int32, route.shape, 1)
    w_e = jnp.sum(jnp.where(lane == e + ROUTE_E0, route, 0.0), axis=-1, keepdims=True)
    acc_sc[...] += w_e * y

    @pl.when(e == N_EXPERTS - 1)
    def _():
        o_ref[...] = _layer_norm(ALPHA * x_ref[...] + acc_sc[...], g2_ref[...], b2_ref[...])


def _moe(x1, route, w_gate, w_up, w_down, ln2_g, ln2_b, tm):
    n = x1.shape[0]
    row = lambda i, e: (i, 0)
    const = lambda i, e: (0, 0)
    exp3 = lambda i, e: (e, 0, 0)
    return pl.pallas_call(
        _moe_body,
        grid=(n // tm, N_EXPERTS),
        in_specs=[pl.BlockSpec((tm, D_MODEL), row), pl.BlockSpec((tm, LANES), row),
                  pl.BlockSpec((1, D_MODEL, D_FF_EXPERT), exp3), pl.BlockSpec((1, D_MODEL, D_FF_EXPERT), exp3),
                  pl.BlockSpec((1, D_FF_EXPERT, D_MODEL), exp3),
                  pl.BlockSpec((1, D_MODEL), const), pl.BlockSpec((1, D_MODEL), const)],
        out_specs=pl.BlockSpec((tm, D_MODEL), row),
        out_shape=jax.ShapeDtypeStruct((n, D_MODEL), F32),
        scratch_shapes=[pltpu.VMEM((tm, D_MODEL), F32), pltpu.VMEM((tm, D_MODEL), BF16)],
        compiler_params=_cparams(("parallel", "arbitrary")),
        name="experts_ln2",
    )(x1, route, w_gate.astype(BF16), w_up.astype(BF16), w_down.astype(BF16),
      ln2_g.astype(F32)[None, :], ln2_b.astype(F32)[None, :])


def _layer(x_prompt, x_sample, w, *, tm, tq, tk, tm_moe):
    bp, tp, _ = x_prompt.shape
    bs, ts, _ = x_sample.shape
    n_p, n_s = bp * tp, bs * ts
    n = n_p + n_s
    x = jnp.concatenate([x_prompt.reshape(n_p, D_MODEL), x_sample.reshape(n_s, D_MODEL)], axis=0).astype(F32)

    q, k, v, u, ga, gs = _in_projection(x, w['w_in'], w['q_norm_g'], w['k_norm_g'], ((bp, tp), (bs, ts)), tm)

    qt = q.reshape(n, N_Q_HEADS, HEAD_DIM).transpose(1, 2, 0)
    k_heads = k.reshape(n, N_KV_HEADS, HEAD_DIM).transpose(1, 0, 2)
    vt = v.reshape(n, N_KV_HEADS, HEAD_DIM).transpose(1, 2, 0)
    ones_tile = jnp.zeros((N_KV_HEADS, V_ROWS - HEAD_DIM, n), BF16).at[:, 0, :].set(1.0)
    vt_blocks = jnp.concatenate([vt, ones_tile], axis=1).reshape(
        N_KV_HEADS, V_ROWS, n // tk, tk).transpose(0, 2, 1, 3)
    ot = jnp.concatenate([_attention(qt, k_heads, vt_blocks, bp, tp, 0, tq, tk),
                          _attention(qt, k_heads, vt_blocks, bs, ts, n_p, tq, tk)], axis=2)
    o = ot.transpose(2, 0, 1).reshape(n, D_ATTN)

    L, G, H = SSM_CHUNK, N_SSM_GROUPS, SSM_GROUP
    ncp, ncs = tp // L, ts // L
    u_p = u[:n_p].reshape(bp, ncp, L, G, H).transpose(3, 1, 0, 2, 4).reshape(G, ncp * bp, CHUNK_W)
    u_s = u[n_p:].reshape(bs, ncs, L, G, H).transpose(3, 0, 1, 2, 4).reshape(G, bs * ncs, CHUNK_W)
    mats = _s5_matrices(w['ssm_a_re'], w['ssm_a_im'], w['ssm_log_dt'], w['ssm_b_re'], w['ssm_b_im'],
                        w['ssm_c_re'], w['ssm_c_im'], w['ssm_d'])
    y_p, y_s = _s5_scan(u_p, u_s, mats, bp, ncp, bs, ncs)
    y = jnp.concatenate([
        y_p.reshape(G, ncp, bp, L, H).transpose(2, 1, 3, 0, 4).reshape(n_p, D_SSM),
        y_s.reshape(G, bs, ncs, L, H).transpose(1, 2, 3, 0, 4).reshape(n_s, D_SSM)], axis=0)

    x1, route = _merge(x, o, y, ga, gs, w['w_up_attn'], w['w_glu'], w['b_glu'], w['w_up_ssm'], w['w_out'],
                       w['ln1_g'], w['ln1_b'], w['w_router_group'], w['b_router_group'],
                       w['w_router_expert'], w['b_router_expert'], tm)
    out = _moe(x1, route, w['w_exp_gate'], w['w_exp_up'], w['w_exp_down'], w['ln2_g'], w['ln2_b'], tm_moe)
    return out[:n_p].reshape(bp, tp, D_MODEL), out[n_p:].reshape(bs, ts, D_MODEL)


def _tile(limit, *sizes):
    return math.gcd(limit, *sizes)


def kernel(x_prompt, x_sample, w_in, q_norm_g, k_norm_g, ssm_a_re, ssm_a_im, ssm_log_dt, ssm_b_re, ssm_b_im, ssm_c_re, ssm_c_im, ssm_d, w_glu, b_glu, w_up_attn, w_up_ssm, w_out, ln1_g, ln1_b, w_router_group, b_router_group, w_router_expert, b_router_expert, w_exp_gate, w_exp_up, w_exp_down, ln2_g, ln2_b):
    w = dict(w_in=w_in, q_norm_g=q_norm_g, k_norm_g=k_norm_g, ssm_a_re=ssm_a_re, ssm_a_im=ssm_a_im,
             ssm_log_dt=ssm_log_dt, ssm_b_re=ssm_b_re, ssm_b_im=ssm_b_im, ssm_c_re=ssm_c_re, ssm_c_im=ssm_c_im,
             ssm_d=ssm_d, w_glu=w_glu, b_glu=b_glu, w_up_attn=w_up_attn, w_up_ssm=w_up_ssm, w_out=w_out,
             ln1_g=ln1_g, ln1_b=ln1_b, w_router_group=w_router_group, b_router_group=b_router_group,
             w_router_expert=w_router_expert, b_router_expert=b_router_expert, w_exp_gate=w_exp_gate,
             w_exp_up=w_exp_up, w_exp_down=w_exp_down, ln2_g=ln2_g, ln2_b=ln2_b)
    w = {name: val[0] for name, val in w.items()}
    tp, ts = x_prompt.shape[1], x_sample.shape[1]
    return _layer(x_prompt, x_sample, w,
                  tm=_tile(512, tp, ts), tq=_tile(256, tp, ts), tk=_tile(512, tp // 2, ts // 2), tm_moe=_tile(1024, tp, ts))
```

```python
import functools
import math

import jax
import jax.numpy as jnp
from jax import lax
from jax.experimental import pallas as pl
from jax.experimental.pallas import tpu as pltpu

F32 = jnp.float32
BF16 = jnp.bfloat16

D_MODEL = 1024
GRID_W = 64
HEAD_DIM = 64
N_Q_HEADS = 8
N_KV_HEADS = 2
Q_PER_KV = N_Q_HEADS // N_KV_HEADS
D_ATTN = N_Q_HEADS * HEAD_DIM
D_KV = N_KV_HEADS * HEAD_DIM
ROPE_THETA = 10000.0
D_SSM = D_MODEL // 2
SSM_GROUP = 16
N_SSM_GROUPS = D_SSM // SSM_GROUP
SSM_STATE = 64
N_EXPERT_GROUPS = 4
EXPERTS_PER_GROUP = 4
N_EXPERTS = N_EXPERT_GROUPS * EXPERTS_PER_GROUP
D_FF_EXPERT = D_MODEL // 2
DEPTH = 1
ALPHA = (2.0 * DEPTH) ** 0.25
EPS = 1e-6
D_IN_PROJ = D_ATTN + 2 * D_KV + D_SSM + 2 * D_MODEL
OFF_K = D_ATTN
OFF_V = OFF_K + D_KV
OFF_U = OFF_V + D_KV
OFF_GA = OFF_U + D_SSM
OFF_GS = OFF_GA + D_MODEL

LANES = 128
SSM_CHUNK = 16
CHUNK_W = SSM_CHUNK * SSM_GROUP
ROUTE_E0 = N_EXPERT_GROUPS
VMEM_LIMIT = 56 * 1024 * 1024
EXPERT_CAP = 96


def _cparams(sem):
    return pltpu.CompilerParams(dimension_semantics=sem, vmem_limit_bytes=VMEM_LIMIT)


def _dot(a, b):
    return jnp.dot(a, b, preferred_element_type=F32)


def _split_bf16(x):
    hi = x.astype(BF16)
    lo = (x - hi.astype(F32)).astype(BF16)
    return hi, lo


def _layer_norm(h, g, b):
    mu = jnp.mean(h, axis=-1, keepdims=True)
    var = jnp.mean(jnp.square(h - mu), axis=-1, keepdims=True)
    return (h - mu) * lax.rsqrt(var + EPS) * g + b


def _inproj_body(x_ref, w_ref, gq_ref, gk_ref, cos_ref, sin_ref, seg_ref,
                 q_ref, k_ref, v_ref, u_ref, ga_ref, gs_ref):
    xb = x_ref[...].astype(BF16)
    cos = cos_ref[...]
    sin = sin_ref[...]
    seg = seg_ref[...]
    lane = lax.broadcasted_iota(jnp.int32, (1, LANES), 1)
    first_half = (lane % HEAD_DIM) < (HEAD_DIM // 2)

    def mm(a, b):
        return _dot(xb, w_ref[:, a:b])

    def norm_rope(y, g):
        hi, lo = _split_bf16(y * y)
        ss = _dot(hi, seg) + _dot(lo, seg)
        yn = y * lax.rsqrt(ss * (1.0 / HEAD_DIM) + EPS) * g
        sw = jnp.where(first_half, pltpu.roll(yn, LANES - HEAD_DIM // 2, 1), pltpu.roll(yn, HEAD_DIM // 2, 1))
        return yn * cos + sw * sin

    gq = gq_ref[...]
    for j in range(D_ATTN // LANES):
        q = norm_rope(mm(j * LANES, (j + 1) * LANES), gq)
        q_ref[:, j * LANES:(j + 1) * LANES] = (q * (HEAD_DIM ** -0.5)).astype(BF16)
    k_ref[...] = norm_rope(mm(OFF_K, OFF_V), gk_ref[...]).astype(BF16)
    v_ref[...] = mm(OFF_V, OFF_U).astype(BF16)
    u_ref[...] = mm(OFF_U, OFF_GA)
    ga_ref[...] = jax.nn.sigmoid(mm(OFF_GA, OFF_GS)).astype(BF16)
    gs_ref[...] = jax.nn.sigmoid(mm(OFF_GS, D_IN_PROJ)).astype(BF16)


def _rope_tables(t_max):
    t = jnp.arange(t_max, dtype=jnp.int32)
    row_pos = (t // GRID_W).astype(F32)
    col_pos = (t % GRID_W).astype(F32)
    n_freq = HEAD_DIM // 4
    inv_freq = ROPE_THETA ** (-jnp.arange(n_freq, dtype=F32) / n_freq)
    ang = jnp.concatenate([row_pos[:, None] * inv_freq, col_pos[:, None] * inv_freq], axis=-1)
    c, s = jnp.cos(ang), jnp.sin(ang)
    reps = LANES // HEAD_DIM
    cos_t = jnp.tile(jnp.concatenate([c, c], axis=-1), (1, reps))
    sin_t = jnp.tile(jnp.concatenate([-s, s], axis=-1), (1, reps))
    return cos_t, sin_t


def _in_projection(x, w_in, q_norm_g, k_norm_g, seqs, tm):
    n = x.shape[0]
    (bp, tp), (bs, ts) = seqs
    npt = bp * tp // tm
    tiles_p, tiles_s = tp // tm, ts // tm
    cos_t, sin_t = _rope_tables(max(tp, ts))
    reps = LANES // HEAD_DIM
    gq = jnp.tile(q_norm_g.astype(F32), reps)[None, :]
    gk = jnp.tile(k_norm_g.astype(F32), reps)[None, :]
    head_of_lane = jnp.arange(LANES) // HEAD_DIM
    seg = (head_of_lane[:, None] == head_of_lane[None, :]).astype(BF16)

    def pos_map(i):
        return (jnp.where(i < npt, i % tiles_p, (i - npt) % tiles_s), 0)

    row = lambda i: (i, 0)
    const = lambda i: (0, 0)
    return pl.pallas_call(
        _inproj_body,
        grid=(n // tm,),
        in_specs=[
            pl.BlockSpec((tm, D_MODEL), row),
            pl.BlockSpec((D_MODEL, D_IN_PROJ), const),
            pl.BlockSpec((1, LANES), const),
            pl.BlockSpec((1, LANES), const),
            pl.BlockSpec((tm, LANES), pos_map),
            pl.BlockSpec((tm, LANES), pos_map),
            pl.BlockSpec((LANES, LANES), const),
        ],
        out_specs=[
            pl.BlockSpec((tm, D_ATTN), row),
            pl.BlockSpec((tm, D_KV), row),
            pl.BlockSpec((tm, D_KV), row),
            pl.BlockSpec((tm, D_SSM), row),
            pl.BlockSpec((tm, D_MODEL), row),
            pl.BlockSpec((tm, D_MODEL), row),
        ],
        out_shape=[
            jax.ShapeDtypeStruct((n, D_ATTN), BF16),
            jax.ShapeDtypeStruct((n, D_KV), BF16),
            jax.ShapeDtypeStruct((n, D_KV), BF16),
            jax.ShapeDtypeStruct((n, D_SSM), F32),
            jax.ShapeDtypeStruct((n, D_MODEL), BF16),
            jax.ShapeDtypeStruct((n, D_MODEL), BF16),
        ],
        compiler_params=_cparams(("parallel",)),
        name="in_projection",
    )(x, w_in.astype(BF16), gq, gk, cos_t, sin_t, seg)


V_ROWS = HEAD_DIM + 16


def _attn_body(qt_ref, k_ref, vt_ref, o_ref, m_sc, acc_sc, s_sc, *, tq, tk, n_kv):
    q4t = jnp.concatenate([qt_ref[h] for h in range(Q_PER_KV)], axis=1)
    m_sc[...] = jnp.full(m_sc.shape, -jnp.inf, F32)
    acc_sc[...] = jnp.zeros(acc_sc.shape, F32)

    def scores(j):
        kk = k_ref[0, pl.ds(pl.multiple_of(j * tk, tk), tk), :]
        return _dot(kk, q4t)

    def softmax_pv(s, j):
        m_old = m_sc[...]
        m_new = jnp.maximum(m_old, jnp.max(s, axis=0, keepdims=True))
        p = jnp.exp(s - m_new).astype(BF16)
        acc_sc[...] = jnp.exp(m_old - m_new) * acc_sc[...] + _dot(vt_ref[0, j], p)
        m_sc[...] = m_new

    s_sc[0] = scores(0)

    def body(jj, carry):
        j = 2 * jj
        s_sc[1] = scores(j + 1)
        softmax_pv(s_sc[0], j)
        s_sc[0] = scores(jnp.minimum(j + 2, n_kv - 1))
        softmax_pv(s_sc[1], j + 1)
        return carry

    lax.fori_loop(0, n_kv // 2, body, 0)
    out = acc_sc[0:HEAD_DIM, :] / acc_sc[HEAD_DIM:HEAD_DIM + 1, :]
    for h in range(Q_PER_KV):
        o_ref[h] = out[:, h * tq:(h + 1) * tq].astype(o_ref.dtype)


def _attention(qt, k_heads, vt_blocks, b, t, tok_off, tq, tk):
    n_qt = t // tq
    n_kv = t // tk
    qoff = tok_off // tq
    seq_off = tok_off // t
    body = functools.partial(_attn_body, tq=tq, tk=tk, n_kv=n_kv)
    return pl.pallas_call(
        body,
        grid=(b, N_KV_HEADS, n_qt),
        in_specs=[
            pl.BlockSpec((Q_PER_KV, HEAD_DIM, tq), lambda bi, g, qi: (g, 0, qoff + bi * n_qt + qi)),
            pl.BlockSpec((1, t, HEAD_DIM), lambda bi, g, qi: (g, seq_off + bi, 0)),
            pl.BlockSpec((1, n_kv, V_ROWS, tk), lambda bi, g, qi: (g, seq_off + bi, 0, 0)),
        ],
        out_specs=pl.BlockSpec((Q_PER_KV, HEAD_DIM, tq), lambda bi, g, qi: (g, 0, bi * n_qt + qi)),
        out_shape=jax.ShapeDtypeStruct((N_Q_HEADS, HEAD_DIM, b * t), BF16),
        scratch_shapes=[
            pltpu.VMEM((1, Q_PER_KV * tq), F32),
            pltpu.VMEM((V_ROWS, Q_PER_KV * tq), F32),
            pltpu.VMEM((2, tk, Q_PER_KV * tq), F32),
        ],
        compiler_params=_cparams(("parallel", "parallel", "parallel")),
        name="attention",
    )(qt, k_heads, vt_blocks)


def _s5_matrices(a_re, a_im, log_dt, b_re, b_im, c_re, c_im, d_skip):
    hp = lax.Precision.HIGHEST
    L, G, P, H = SSM_CHUNK, N_SSM_GROUPS, SSM_STATE, SSM_GROUP
    f = lambda v: v.astype(F32)
    lr, li = f(a_re), f(a_im)
    dt = jnp.exp(f(log_dt))[..., None]
    mag = jnp.exp(lr * dt)
    ar = mag * jnp.cos(li * dt)
    ai = mag * jnp.sin(li * dt)
    den = lr * lr + li * li
    kr = ((ar - 1.0) * lr + ai * li) / den
    ki = (ai * lr - (ar - 1.0) * li) / den
    br, bi = f(b_re), f(b_im)
    bbr = kr[..., None] * br - ki[..., None] * bi
    bbi = kr[..., None] * bi + ki[..., None] * br
    kk = jnp.arange(L + 1, dtype=F32)[:, None, None, None]
    pmag = jnp.exp(kk * (lr * dt)[None])
    ph = kk * (li * dt)[None]
    pr = pmag * jnp.cos(ph)
    pi = pmag * jnp.sin(ph)
    cr, ci = f(c_re), f(c_im)
    car = cr[None] * pr[:, :, :, None, :] - ci[None] * pi[:, :, :, None, :]
    cai = cr[None] * pi[:, :, :, None, :] + ci[None] * pr[:, :, :, None, :]
    kern = (jnp.einsum('kdgop,dgpi->kdgoi', car, bbr, precision=hp)
            - jnp.einsum('kdgop,dgpi->kdgoi', cai, bbi, precision=hp))
    jj = jnp.arange(L)[:, None]
    ii = jnp.arange(L)[None, :]

    def toeplitz(kd, lag, valid):
        blocks = jnp.where(valid[:, :, None, None, None], kd[jnp.clip(lag, 0, L)], 0.0)
        return blocks.transpose(2, 0, 4, 1, 3).reshape(G, CHUNK_W, CHUNK_W)

    msum = toeplitz(kern[:, 0], ii - jj, ii >= jj) + toeplitz(kern[:, 1], jj - ii, jj >= ii)

    def in_op(d, powers):
        prj, pij = pr[powers, d], pi[powers, d]
        re = prj[..., None] * bbr[d][None] - pij[..., None] * bbi[d][None]
        im = prj[..., None] * bbi[d][None] + pij[..., None] * bbr[d][None]
        to = lambda m: m.transpose(1, 0, 3, 2).reshape(G, CHUNK_W, P)
        return to(re), to(im)

    def out_op(d, powers):
        to = lambda m: m.transpose(1, 3, 0, 2).reshape(G, P, CHUNK_W)
        return to(car[powers, d]), to(-cai[powers, d])

    steps = jnp.arange(L)
    bm = jnp.stack(in_op(0, L - 1 - steps) + in_op(1, steps), axis=1)
    cm = jnp.stack(out_op(0, steps + 1) + out_op(1, L - steps), axis=1)
    al = jnp.stack([pr[L, 0], pi[L, 0], pr[L, 1], pi[L, 1]], axis=1)[:, :, None, :]
    dvec = jnp.tile(f(d_skip).reshape(G, 1, H), (1, 1, L))
    return msum.astype(BF16), bm.astype(BF16), cm.astype(BF16), al, dvec


def _s5_body(up_ref, us_ref, m_ref, bm_ref, cm_ref, al_ref, d_ref, yp_ref, ys_ref, sp_sc, ss_sc,
             *, bp, ncp, bs, ncs):
    upb = up_ref[0].astype(BF16)
    usb = us_ref[0].astype(BF16)
    for d in range(4):
        sp_sc[d] = _dot(upb, bm_ref[0, d])
        ss_sc[d] = _dot(usb, bm_ref[0, d])
    al = [al_ref[0, d] for d in range(4)]

    def advance(sc, rows, d, sr, si):
        lr = sc[d, rows, :]
        li = sc[d + 1, rows, :]
        sc[d, rows, :] = sr
        sc[d + 1, rows, :] = si
        ar, ai = al[d], al[d + 1]
        return ar * sr - ai * si + lr, ar * si + ai * sr + li

    def prompt_step(k, st):
        fr, fi, br, bi = st
        rf = pl.ds(pl.multiple_of(k * bp, bp), bp)
        rb = pl.ds(pl.multiple_of((ncp - 1 - k) * bp, bp), bp)
        fr, fi = advance(sp_sc, rf, 0, fr, fi)
        br, bi = advance(sp_sc, rb, 2, br, bi)
        return fr, fi, br, bi

    def sample_step(k, st):
        out = []
        for b in range(bs):
            fr, fi, br, bi = st[4 * b:4 * b + 4]
            fr, fi = advance(ss_sc, pl.ds(b * ncs + k, 1), 0, fr, fi)
            br, bi = advance(ss_sc, pl.ds(b * ncs + ncs - 1 - k, 1), 2, br, bi)
            out += [fr, fi, br, bi]
        return tuple(out)

    zp = tuple(jnp.zeros((bp, SSM_STATE), F32) for _ in range(4))
    zs = tuple(jnp.zeros((1, SSM_STATE), F32) for _ in range(4 * bs))
    n_both = min(ncp, ncs)

    def both(k, st):
        return prompt_step(k, st[0]), sample_step(k, st[1])

    stp, sts = lax.fori_loop(0, n_both, both, (zp, zs))
    if ncp > n_both:
        lax.fori_loop(n_both, ncp, prompt_step, stp)
    if ncs > n_both:
        lax.fori_loop(n_both, ncs, sample_step, sts)

    dvec = d_ref[0]
    for u_ref, ub, sc, y_ref in ((up_ref, upb, sp_sc, yp_ref), (us_ref, usb, ss_sc, ys_ref)):
        y = _dot(ub, m_ref[0]) + u_ref[0] * dvec
        for d in range(4):
            y = y + _dot(sc[d].astype(BF16), cm_ref[0, d])
        y_ref[0] = y


def _s5_scan(u_p, u_s, mats, bp, ncp, bs, ncs):
    msum, bm, cm, al, dvec = mats
    rp, rs = ncp * bp, bs * ncs
    g3 = lambda g: (g, 0, 0)
    g4 = lambda g: (g, 0, 0, 0)
    body = functools.partial(_s5_body, bp=bp, ncp=ncp, bs=bs, ncs=ncs)
    return pl.pallas_call(
        body,
        grid=(N_SSM_GROUPS,),
        in_specs=[
            pl.BlockSpec((1, rp, CHUNK_W), g3),
            pl.BlockSpec((1, rs, CHUNK_W), g3),
            pl.BlockSpec((1, CHUNK_W, CHUNK_W), g3),
            pl.BlockSpec((1, 4, CHUNK_W, SSM_STATE), g4),
            pl.BlockSpec((1, 4, SSM_STATE, CHUNK_W), g4),
            pl.BlockSpec((1, 4, 1, SSM_STATE), g4),
            pl.BlockSpec((1, 1, CHUNK_W), g3),
        ],
        out_specs=[pl.BlockSpec((1, rp, CHUNK_W), g3), pl.BlockSpec((1, rs, CHUNK_W), g3)],
        out_shape=[jax.ShapeDtypeStruct((N_SSM_GROUPS, rp, CHUNK_W), F32),
                   jax.ShapeDtypeStruct((N_SSM_GROUPS, rs, CHUNK_W), F32)],
        scratch_shapes=[pltpu.VMEM((4, rp, SSM_STATE), F32), pltpu.VMEM((4, rs, SSM_STATE), F32)],
        compiler_params=_cparams(("parallel",)),
        name="s5_scan",
    )(u_p, u_s, msum, bm, cm, al, dvec)


def _route(logits):
    lane = lax.broadcasted_iota(jnp.int32, logits.shape, 1)
    neg = -jnp.inf
    big = jnp.int32(LANES)
    is_g = lane < N_EXPERT_GROUPS
    gl = jnp.where(is_g, logits, neg)
    gmax = jnp.max(gl, axis=-1, keepdims=True)
    gidx = jnp.min(jnp.where(gl == gmax, lane, big), axis=-1, keepdims=True)
    g_w = 1.0 / jnp.sum(jnp.where(is_g, jnp.exp(logits - gmax), 0.0), axis=-1, keepdims=True)
    e_lane = lane - ROUTE_E0
    in_group = (e_lane >= 0) & (e_lane < N_EXPERTS) & ((e_lane // EXPERTS_PER_GROUP) == gidx)
    el = jnp.where(in_group, logits, neg)
    v1 = jnp.max(el, axis=-1, keepdims=True)
    i1 = jnp.min(jnp.where(el == v1, lane, big), axis=-1, keepdims=True)
    el2 = jnp.where(lane == i1, neg, el)
    v2 = jnp.max(el2, axis=-1, keepdims=True)
    i2 = jnp.min(jnp.where(el2 == v2, lane, big), axis=-1, keepdims=True)
    e2 = jnp.exp(v2 - v1)
    inv = g_w / (1.0 + e2)
    combine = jnp.where(lane == i1, inv, 0.0) + jnp.where(lane == i2, e2 * inv, 0.0)
    return combine, (lane == i1) | (lane == i2)


AUX_LANE = LANES - 1


def _slot_tables(cap):
    slot = jnp.arange(N_EXPERTS * cap)
    lane = jnp.arange(LANES)[:, None]
    expand = (lane == ROUTE_E0 + slot // cap).astype(F32)
    expand_aug = jnp.where(lane == AUX_LANE, -(slot % cap).astype(F32), expand)
    return expand.astype(BF16), expand_aug.astype(BF16)


def _merge_body(x_ref, o_ref, y_ref, ga_ref, gs_ref, wua_ref, wglu_ref, bglu_ref, wus_ref, wout_ref,
                g1_ref, b1_ref, wrh_ref, wrl_ref, br_ref, tri_ref, eaugt_ref,
                x1_ref, route_ref, rank_ref, xs_ref, *, cap):
    z = jax.nn.gelu(y_ref[...])
    z = z * jax.nn.sigmoid(_dot(z.astype(BF16), wglu_ref[...]) + bglu_ref[...])
    ssm_up = _dot(z.astype(BF16), wus_ref[...])
    att_up = _dot(o_ref[...], wua_ref[...])
    merged = ga_ref[...].astype(F32) * att_up + gs_ref[...].astype(F32) * ssm_up
    mix = _dot(merged.astype(BF16), wout_ref[...])
    x1 = _layer_norm(ALPHA * x_ref[...] + mix, g1_ref[...], b1_ref[...])
    x1_ref[...] = x1
    xh, xl = _split_bf16(x1)
    logits = _dot(xh, wrh_ref[...]) + (_dot(xh, wrl_ref[...]) + _dot(xl, wrh_ref[...])) + br_ref[...]
    combine, sel = _route(logits)
    route_ref[...] = combine
    rank = _dot(tri_ref[...], jnp.where(sel, 1.0, 0.0).astype(BF16))
    lane = lax.broadcasted_iota(jnp.int32, rank.shape, 1)
    rank = jnp.where(sel, jnp.minimum(rank, float(cap)), float(cap))
    rank = jnp.where(lane == AUX_LANE, 1.0, rank).astype(BF16)
    rank_ref[...] = rank
    miss = lax.dot_general(eaugt_ref[...], rank, (((1,), (1,)), ((), ())), preferred_element_type=F32)
    gather = jnp.where(miss == 0.0, 1.0, 0.0).astype(BF16)
    xs_ref[0] = _dot(gather, xh).astype(BF16)


def _merge(x, o, y, ga, gs, w_up_attn, w_glu, b_glu, w_up_ssm, w_out, ln1_g, ln1_b,
           w_router_group, b_router_group, w_router_expert, b_router_expert, tm, cap):
    n = x.shape[0]
    n_slot = N_EXPERTS * cap
    n_r = N_EXPERT_GROUPS + N_EXPERTS
    wr = jnp.zeros((D_MODEL, LANES), F32).at[:, :n_r].set(
        jnp.concatenate([w_router_group, w_router_expert], axis=1).astype(F32))
    br = jnp.zeros((1, LANES), F32).at[0, :n_r].set(
        jnp.concatenate([b_router_group, b_router_expert]).astype(F32))
    wrh, wrl = _split_bf16(wr)
    tri = (jnp.arange(tm)[:, None] > jnp.arange(tm)[None, :]).astype(BF16)
    _, expand_aug = _slot_tables(cap)
    row = lambda i: (i, 0)
    const = lambda i: (0, 0)
    full = lambda a: pl.BlockSpec(a.shape, const)
    weights = [w_up_attn.astype(BF16), w_glu.astype(BF16), b_glu.astype(F32)[None, :], w_up_ssm.astype(BF16),
               w_out.astype(BF16), ln1_g.astype(F32)[None, :], ln1_b.astype(F32)[None, :], wrh, wrl, br,
               tri, expand_aug.T]
    return pl.pallas_call(
        functools.partial(_merge_body, cap=cap),
        grid=(n // tm,),
        in_specs=[pl.BlockSpec((tm, D_MODEL), row), pl.BlockSpec((tm, D_ATTN), row), pl.BlockSpec((tm, D_SSM), row),
                  pl.BlockSpec((tm, D_MODEL), row), pl.BlockSpec((tm, D_MODEL), row)] + [full(w) for w in weights],
        out_specs=[pl.BlockSpec((tm, D_MODEL), row), pl.BlockSpec((tm, LANES), row), pl.BlockSpec((tm, LANES), row),
                   pl.BlockSpec((1, n_slot, D_MODEL), lambda i: (i, 0, 0))],
        out_shape=[jax.ShapeDtypeStruct((n, D_MODEL), F32), jax.ShapeDtypeStruct((n, LANES), F32),
                   jax.ShapeDtypeStruct((n, LANES), BF16), jax.ShapeDtypeStruct((n // tm, n_slot, D_MODEL), BF16)],
        compiler_params=_cparams(("parallel",)),
        name="merge_ln1_router",
    )(x, o, y, ga, gs, *weights)


def _expert_mlp(xb, wg, wu, wd):
    h = jax.nn.silu(_dot(xb, wg)) * _dot(xb, wu)
    return _dot(h.astype(BF16), wd)


def _experts_body(xs_ref, wg_ref, wu_ref, wd_ref, ys_ref):
    shp = xs_ref.shape
    xb = xs_ref[...].reshape(shp[0] * shp[2], D_MODEL)
    ys_ref[...] = _expert_mlp(xb, wg_ref[0], wu_ref[0], wd_ref[0]).astype(BF16).reshape(shp)


def _experts(xs, w_gate, w_up, w_down, cap, tiles_per_step):
    n_tiles = xs.shape[0]
    xs4 = xs.reshape(n_tiles, N_EXPERTS, cap, D_MODEL)
    slots = pl.BlockSpec((tiles_per_step, 1, cap, D_MODEL), lambda e, t: (t, e, 0, 0))
    exp3 = lambda e, t: (e, 0, 0)
    ys = pl.pallas_call(
        _experts_body,
        grid=(N_EXPERTS, n_tiles // tiles_per_step),
        in_specs=[slots, pl.BlockSpec((1, D_MODEL, D_FF_EXPERT), exp3), pl.BlockSpec((1, D_MODEL, D_FF_EXPERT), exp3),
                  pl.BlockSpec((1, D_FF_EXPERT, D_MODEL), exp3)],
        out_specs=slots,
        out_shape=jax.ShapeDtypeStruct(xs4.shape, BF16),
        compiler_params=_cparams(("parallel", "parallel")),
        name="experts",
    )(xs4, w_gate, w_up, w_down)
    return ys.reshape(xs.shape)


def _combine_body(x1_ref, route_ref, rank_ref, ys_ref, eaug_ref, e_ref, g2_ref, b2_ref, wg_hbm, wu_hbm, wd_hbm,
                  o_ref, acc_sc, wg_sc, wu_sc, wd_sc, *, cap):
    route = route_ref[...]
    rank = rank_ref[...]
    hit = _dot(rank, eaug_ref[...]) == 0.0
    wh, wl = _split_bf16(route)
    spread = _dot(wh, e_ref[...]) + _dot(wl, e_ref[...])
    acc_sc[...] = _dot(jnp.where(hit, spread, 0.0).astype(BF16), ys_ref[0])
    lane = lax.broadcasted_iota(jnp.int32, route.shape, 1)
    spilled = jnp.where((rank.astype(F32) == float(cap)) & (lane != AUX_LANE), route, 0.0)

    @pl.when(jnp.max(jnp.abs(spilled)) > 0.0)
    def _():
        xb = x1_ref[...].astype(BF16)

        def one_expert(e, carry):
            pltpu.sync_copy(wg_hbm.at[e], wg_sc)
            pltpu.sync_copy(wu_hbm.at[e], wu_sc)
            pltpu.sync_copy(wd_hbm.at[e], wd_sc)
            w_e = jnp.sum(jnp.where(lane == e + ROUTE_E0, spilled, 0.0), axis=-1, keepdims=True)
            acc_sc[...] += w_e * _expert_mlp(xb, wg_sc[...], wu_sc[...], wd_sc[...])
            return carry

        lax.fori_loop(0, N_EXPERTS, one_expert, 0)

    o_ref[...] = _layer_norm(ALPHA * x1_ref[...] + acc_sc[...], g2_ref[...], b2_ref[...])


def _combine(x1, route, rank, ys, w_gate, w_up, w_down, ln2_g, ln2_b, tm, cap):
    n = x1.shape[0]
    n_slot = N_EXPERTS * cap
    expand, expand_aug = _slot_tables(cap)
    row = lambda i: (i, 0)
    const = lambda i: (0, 0)
    hbm = pl.BlockSpec(memory_space=pl.ANY)
    return pl.pallas_call(
        functools.partial(_combine_body, cap=cap),
        grid=(n // tm,),
        in_specs=[pl.BlockSpec((tm, D_MODEL), row), pl.BlockSpec((tm, LANES), row), pl.BlockSpec((tm, LANES), row),
                  pl.BlockSpec((1, n_slot, D_MODEL), lambda i: (i, 0, 0)),
                  pl.BlockSpec((LANES, n_slot), const), pl.BlockSpec((LANES, n_slot), const),
                  pl.BlockSpec((1, D_MODEL), const), pl.BlockSpec((1, D_MODEL), const), hbm, hbm, hbm],
        out_specs=pl.BlockSpec((tm, D_MODEL), row),
        out_shape=jax.ShapeDtypeStruct((n, D_MODEL), F32),
        scratch_shapes=[pltpu.VMEM((tm, D_MODEL), F32), pltpu.VMEM((D_MODEL, D_FF_EXPERT), BF16),
                        pltpu.VMEM((D_MODEL, D_FF_EXPERT), BF16), pltpu.VMEM((D_FF_EXPERT, D_MODEL), BF16)],
        compiler_params=_cparams(("parallel",)),
        name="combine_ln2",
    )(x1, route, rank, ys, expand_aug, expand, ln2_g.astype(F32)[None, :], ln2_b.astype(F32)[None, :],
      w_gate, w_up, w_down)


def _layer(x_prompt, x_sample, w, *, tm, tq, tk, cap):
    bp, tp, _ = x_prompt.shape
    bs, ts, _ = x_sample.shape
    n_p, n_s = bp * tp, bs * ts
    n = n_p + n_s
    x = jnp.concatenate([x_prompt.reshape(n_p, D_MODEL), x_sample.reshape(n_s, D_MODEL)], axis=0).astype(F32)

    q, k, v, u, ga, gs = _in_projection(x, w['w_in'], w['q_norm_g'], w['k_norm_g'], ((bp, tp), (bs, ts)), tm)

    qt = q.reshape(n, N_Q_HEADS, HEAD_DIM).transpose(1, 2, 0)
    k_heads = k.reshape(n, N_KV_HEADS, HEAD_DIM).transpose(1, 0, 2)
    vt = v.reshape(n, N_KV_HEADS, HEAD_DIM).transpose(1, 2, 0)
    ones_tile = jnp.zeros((N_KV_HEADS, V_ROWS - HEAD_DIM, n), BF16).at[:, 0, :].set(1.0)
    vt_blocks = jnp.concatenate([vt, ones_tile], axis=1).reshape(
        N_KV_HEADS, V_ROWS, n // tk, tk).transpose(0, 2, 1, 3)
    ot = jnp.concatenate([_attention(qt, k_heads, vt_blocks, bp, tp, 0, tq, tk),
                          _attention(qt, k_heads, vt_blocks, bs, ts, n_p, tq, tk)], axis=2)
    o = ot.transpose(2, 0, 1).reshape(n, D_ATTN)

    L, G, H = SSM_CHUNK, N_SSM_GROUPS, SSM_GROUP
    ncp, ncs = tp // L, ts // L
    u_p = u[:n_p].reshape(bp, ncp, L, G, H).transpose(3, 1, 0, 2, 4).reshape(G, ncp * bp, CHUNK_W)
    u_s = u[n_p:].reshape(bs, ncs, L, G, H).transpose(3, 0, 1, 2, 4).reshape(G, bs * ncs, CHUNK_W)
    mats = _s5_matrices(w['ssm_a_re'], w['ssm_a_im'], w['ssm_log_dt'], w['ssm_b_re'], w['ssm_b_im'],
                        w['ssm_c_re'], w['ssm_c_im'], w['ssm_d'])
    y_p, y_s = _s5_scan(u_p, u_s, mats, bp, ncp, bs, ncs)
    y = jnp.concatenate([
        y_p.reshape(G, ncp, bp, L, H).transpose(2, 1, 3, 0, 4).reshape(n_p, D_SSM),
        y_s.reshape(G, bs, ncs, L, H).transpose(1, 2, 3, 0, 4).reshape(n_s, D_SSM)], axis=0)

    x1, route, rank, xs = _merge(x, o, y, ga, gs, w['w_up_attn'], w['w_glu'], w['b_glu'], w['w_up_ssm'], w['w_out'],
                       w['ln1_g'], w['ln1_b'], w['w_router_group'], w['b_router_group'],
                       w['w_router_expert'], w['b_router_expert'], tm, cap)
    wg, wu, wd = (w[name].astype(BF16) for name in ('w_exp_gate', 'w_exp_up', 'w_exp_down'))
    ys = _experts(xs, wg, wu, wd, cap, math.gcd(8, n // tm))
    out = _combine(x1, route, rank, ys, wg, wu, wd, w['ln2_g'], w['ln2_b'], tm, cap)
    return out[:n_p].reshape(bp, tp, D_MODEL), out[n_p:].reshape(bs, ts, D_MODEL)


def _tile(limit, *sizes):
    return math.gcd(limit, *sizes)


def kernel(x_prompt, x_sample, w_in, q_norm_g, k_norm_g, ssm_a_re, ssm_a_im, ssm_log_dt, ssm_b_re, ssm_b_im, ssm_c_re, ssm_c_im, ssm_d, w_glu, b_glu, w_up_attn, w_up_ssm, w_out, ln1_g, ln1_b, w_router_group, b_router_group, w_router_expert, b_router_expert, w_exp_gate, w_exp_up, w_exp_down, ln2_g, ln2_b):
    w = dict(w_in=w_in, q_norm_g=q_norm_g, k_norm_g=k_norm_g, ssm_a_re=ssm_a_re, ssm_a_im=ssm_a_im,
             ssm_log_dt=ssm_log_dt, ssm_b_re=ssm_b_re, ssm_b_im=ssm_b_im, ssm_c_re=ssm_c_re, ssm_c_im=ssm_c_im,
             ssm_d=ssm_d, w_glu=w_glu, b_glu=b_glu, w_up_attn=w_up_attn, w_up_ssm=w_up_ssm, w_out=w_out,
             ln1_g=ln1_g, ln1_b=ln1_b, w_router_group=w_router_group, b_router_group=b_router_group,
             w_router_expert=w_router_expert, b_router_expert=b_router_expert, w_exp_gate=w_exp_gate,
             w_exp_up=w_exp_up, w_exp_down=w_exp_down, ln2_g=ln2_g, ln2_b=ln2_b)
    w = {name: val[0] for name, val in w.items()}
    tp, ts = x_prompt.shape[1], x_sample.shape[1]
    return _layer(x_prompt, x_sample, w,
                  tm=_tile(512, tp, ts), tq=_tile(256, tp, ts), tk=_tile(512, tp // 2, ts // 2), cap=EXPERT_CAP)
```

```python
import functools
import math

import jax
import jax.numpy as jnp
from jax import lax
from jax.experimental import pallas as pl
from jax.experimental.pallas import tpu as pltpu

F32 = jnp.float32
BF16 = jnp.bfloat16

D_MODEL = 1024
GRID_W = 64
HEAD_DIM = 64
N_Q_HEADS = 8
N_KV_HEADS = 2
Q_PER_KV = N_Q_HEADS // N_KV_HEADS
D_ATTN = N_Q_HEADS * HEAD_DIM
D_KV = N_KV_HEADS * HEAD_DIM
ROPE_THETA = 10000.0
D_SSM = D_MODEL // 2
SSM_GROUP = 16
N_SSM_GROUPS = D_SSM // SSM_GROUP
SSM_STATE = 64
N_EXPERT_GROUPS = 4
EXPERTS_PER_GROUP = 4
N_EXPERTS = N_EXPERT_GROUPS * EXPERTS_PER_GROUP
D_FF_EXPERT = D_MODEL // 2
DEPTH = 1
ALPHA = (2.0 * DEPTH) ** 0.25
EPS = 1e-6
D_IN_PROJ = D_ATTN + 2 * D_KV + D_SSM + 2 * D_MODEL
OFF_K = D_ATTN
OFF_V = OFF_K + D_KV
OFF_U = OFF_V + D_KV
OFF_GA = OFF_U + D_SSM
OFF_GS = OFF_GA + D_MODEL

LANES = 128
SSM_CHUNK = 16
CHUNK_W = SSM_CHUNK * SSM_GROUP
ROUTE_E0 = N_EXPERT_GROUPS
VMEM_LIMIT = 56 * 1024 * 1024
EXPERT_CAP = 96


def _cparams(sem):
    return pltpu.CompilerParams(dimension_semantics=sem, vmem_limit_bytes=VMEM_LIMIT)


def _dot(a, b):
    return jnp.dot(a, b, preferred_element_type=F32)


def _split_bf16(x):
    hi = x.astype(BF16)
    lo = (x - hi.astype(F32)).astype(BF16)
    return hi, lo


def _layer_norm(h, g, b):
    mu = jnp.mean(h, axis=-1, keepdims=True)
    var = jnp.mean(jnp.square(h - mu), axis=-1, keepdims=True)
    return (h - mu) * lax.rsqrt(var + EPS) * g + b


V_ROWS = HEAD_DIM + 16
LANE_BLOCKS = LANES // SSM_GROUP


def _block_transpose(blocks):
    lane = lax.broadcasted_iota(jnp.int32, blocks[0].shape, 1)
    v = list(blocks)
    for s in range(3):
        d = SSM_GROUP << s
        keep = ((lane // d) % 2) == 0
        nxt = list(v)
        for a in range(LANE_BLOCKS):
            if not a & (1 << s):
                a2 = a | (1 << s)
                nxt[a] = jnp.where(keep, v[a], pltpu.roll(v[a2], d, 1))
                nxt[a2] = jnp.where(keep, pltpu.roll(v[a], LANES - d, 1), v[a2])
        v = nxt
    return v


def _inproj_body(x_ref, w_ref, gq_ref, gk_ref, cos_ref, sin_ref, seg_ref,
                 qt_ref, kh_ref, vt_ref, ug_ref, ga_ref, gs_ref, u_sc, *, tk):
    xb = x_ref[...].astype(BF16)
    tm = xb.shape[0]
    cos = cos_ref[...]
    sin = sin_ref[...]
    seg = seg_ref[...]
    lane = lax.broadcasted_iota(jnp.int32, (1, LANES), 1)
    first_half = (lane % HEAD_DIM) < (HEAD_DIM // 2)

    def mm(a, b):
        return _dot(xb, w_ref[:, a:b])

    def norm_rope(y, g):
        hi, lo = _split_bf16(y * y)
        ss = _dot(hi, seg) + _dot(lo, seg)
        yn = y * lax.rsqrt(ss * (1.0 / HEAD_DIM) + EPS) * g
        sw = jnp.where(first_half, pltpu.roll(yn, LANES - HEAD_DIM // 2, 1), pltpu.roll(yn, HEAD_DIM // 2, 1))
        return yn * cos + sw * sin

    gq = gq_ref[...]
    heads_per_slab = LANES // HEAD_DIM
    for j in range(D_ATTN // LANES):
        q = norm_rope(mm(j * LANES, (j + 1) * LANES), gq)
        qt = (q * (HEAD_DIM ** -0.5)).T
        for h in range(heads_per_slab):
            qt_ref[heads_per_slab * j + h] = qt[h * HEAD_DIM:(h + 1) * HEAD_DIM].astype(BF16)
    k = norm_rope(mm(OFF_K, OFF_V), gk_ref[...])
    vt = mm(OFF_V, OFF_U).astype(BF16).astype(F32).T
    ones_tile = jnp.where(lax.broadcasted_iota(jnp.int32, (V_ROWS - HEAD_DIM, tk), 0) == 0, 1.0, 0.0).astype(BF16)
    for g in range(N_KV_HEADS):
        kh_ref[g] = k[:, g * HEAD_DIM:(g + 1) * HEAD_DIM].astype(BF16)
        for jb in range(tm // tk):
            vt_ref[g, jb, 0:HEAD_DIM] = vt[g * HEAD_DIM:(g + 1) * HEAD_DIM, jb * tk:(jb + 1) * tk].astype(BF16)
            vt_ref[g, jb, HEAD_DIM:V_ROWS] = ones_tile
    u = mm(OFF_U, OFF_GA)
    n_tiles = D_SSM // LANES
    for gh in range(n_tiles):
        u_sc[gh] = u[:, gh * LANES:(gh + 1) * LANES]
    n_chunks = tm // SSM_CHUNK
    for gh in range(n_tiles):
        for ih in range(SSM_CHUNK // LANE_BLOCKS):
            rows = [u_sc[gh, pl.ds(LANE_BLOCKS * ih + a, n_chunks, stride=SSM_CHUNK), :] for a in range(LANE_BLOCKS)]
            for gl, blk in enumerate(_block_transpose(rows)):
                ug_ref[LANE_BLOCKS * gh + gl, :, ih * LANES:(ih + 1) * LANES] = blk
    ga_ref[...] = jax.nn.sigmoid(mm(OFF_GA, OFF_GS)).astype(BF16)
    gs_ref[...] = jax.nn.sigmoid(mm(OFF_GS, D_IN_PROJ)).astype(BF16)


def _rope_tables(t_max):
    t = jnp.arange(t_max, dtype=jnp.int32)
    row_pos = (t // GRID_W).astype(F32)
    col_pos = (t % GRID_W).astype(F32)
    n_freq = HEAD_DIM // 4
    inv_freq = ROPE_THETA ** (-jnp.arange(n_freq, dtype=F32) / n_freq)
    ang = jnp.concatenate([row_pos[:, None] * inv_freq, col_pos[:, None] * inv_freq], axis=-1)
    c, s = jnp.cos(ang), jnp.sin(ang)
    reps = LANES // HEAD_DIM
    cos_t = jnp.tile(jnp.concatenate([c, c], axis=-1), (1, reps))
    sin_t = jnp.tile(jnp.concatenate([-s, s], axis=-1), (1, reps))
    return cos_t, sin_t


def _in_projection(x, t, w_in_bf16, q_norm_g, k_norm_g, tm, tk):
    n = x.shape[0]
    tiles_per_seq = t // tm
    cos_t, sin_t = _rope_tables(t)
    reps = LANES // HEAD_DIM
    gq = jnp.tile(q_norm_g.astype(F32), reps)[None, :]
    gk = jnp.tile(k_norm_g.astype(F32), reps)[None, :]
    head_of_lane = jnp.arange(LANES) // HEAD_DIM
    seg = (head_of_lane[:, None] == head_of_lane[None, :]).astype(BF16)
    pos_map = lambda i: (i % tiles_per_seq, 0)
    row = lambda i: (i, 0)
    const = lambda i: (0, 0)
    return pl.pallas_call(
        functools.partial(_inproj_body, tk=tk),
        grid=(n // tm,),
        in_specs=[
            pl.BlockSpec((tm, D_MODEL), row),
            pl.BlockSpec((D_MODEL, D_IN_PROJ), const),
            pl.BlockSpec((1, LANES), const),
            pl.BlockSpec((1, LANES), const),
            pl.BlockSpec((tm, LANES), pos_map),
            pl.BlockSpec((tm, LANES), pos_map),
            pl.BlockSpec((LANES, LANES), const),
        ],
        out_specs=[
            pl.BlockSpec((N_Q_HEADS, HEAD_DIM, tm), lambda i: (0, 0, i)),
            pl.BlockSpec((N_KV_HEADS, tm, HEAD_DIM), lambda i: (0, i, 0)),
            pl.BlockSpec((N_KV_HEADS, tm // tk, V_ROWS, tk), lambda i: (0, i, 0, 0)),
            pl.BlockSpec((N_SSM_GROUPS, tm // SSM_CHUNK, CHUNK_W), lambda i: (0, i, 0)),
            pl.BlockSpec((tm, D_MODEL), row),
            pl.BlockSpec((tm, D_MODEL), row),
        ],
        out_shape=[
            jax.ShapeDtypeStruct((N_Q_HEADS, HEAD_DIM, n), BF16),
            jax.ShapeDtypeStruct((N_KV_HEADS, n, HEAD_DIM), BF16),
            jax.ShapeDtypeStruct((N_KV_HEADS, n // tk, V_ROWS, tk), BF16),
            jax.ShapeDtypeStruct((N_SSM_GROUPS, n // SSM_CHUNK, CHUNK_W), F32),
            jax.ShapeDtypeStruct((n, D_MODEL), BF16),
            jax.ShapeDtypeStruct((n, D_MODEL), BF16),
        ],
        scratch_shapes=[pltpu.VMEM((D_SSM // LANES, tm, LANES), F32)],
        compiler_params=_cparams(("parallel",)),
        name="in_projection",
    )(x, w_in_bf16, gq, gk, cos_t, sin_t, seg)


def _attn_body(qt_ref, k_ref, vt_ref, o_ref, m_sc, acc_sc, s_sc, *, tq, tk, n_kv):
    q4t = jnp.concatenate([qt_ref[h] for h in range(Q_PER_KV)], axis=1)
    m_sc[...] = jnp.full(m_sc.shape, -jnp.inf, F32)
    acc_sc[...] = jnp.zeros(acc_sc.shape, F32)

    def scores(j):
        kk = k_ref[0, pl.ds(pl.multiple_of(j * tk, tk), tk), :]
        return _dot(kk, q4t)

    def softmax_pv(s, j):
        m_old = m_sc[...]
        m_new = jnp.maximum(m_old, jnp.max(s, axis=0, keepdims=True))
        p = jnp.exp(s - m_new).astype(BF16)
        acc_sc[...] = jnp.exp(m_old - m_new) * acc_sc[...] + _dot(vt_ref[0, j], p)
        m_sc[...] = m_new

    s_sc[0] = scores(0)

    def body(jj, carry):
        j = 2 * jj
        s_sc[1] = scores(j + 1)
        softmax_pv(s_sc[0], j)
        s_sc[0] = scores(jnp.minimum(j + 2, n_kv - 1))
        softmax_pv(s_sc[1], j + 1)
        return carry

    lax.fori_loop(0, n_kv // 2, body, 0)
    out = acc_sc[0:HEAD_DIM, :] / acc_sc[HEAD_DIM:HEAD_DIM + 1, :]
    for h in range(Q_PER_KV):
        o_ref[h] = out[:, h * tq:(h + 1) * tq].astype(o_ref.dtype)


def _attention(qt, k_heads, vt_blocks, b, t, tq, tk):
    n_qt = t // tq
    n_kv = t // tk
    body = functools.partial(_attn_body, tq=tq, tk=tk, n_kv=n_kv)
    return pl.pallas_call(
        body,
        grid=(b, N_KV_HEADS, n_qt),
        in_specs=[
            pl.BlockSpec((Q_PER_KV, HEAD_DIM, tq), lambda bi, g, qi: (g, 0, bi * n_qt + qi)),
            pl.BlockSpec((1, t, HEAD_DIM), lambda bi, g, qi: (g, bi, 0)),
            pl.BlockSpec((1, n_kv, V_ROWS, tk), lambda bi, g, qi: (g, bi, 0, 0)),
        ],
        out_specs=pl.BlockSpec((Q_PER_KV, HEAD_DIM, tq), lambda bi, g, qi: (g, 0, bi * n_qt + qi)),
        out_shape=jax.ShapeDtypeStruct((N_Q_HEADS, HEAD_DIM, b * t), BF16),
        scratch_shapes=[
            pltpu.VMEM((1, Q_PER_KV * tq), F32),
            pltpu.VMEM((V_ROWS, Q_PER_KV * tq), F32),
            pltpu.VMEM((2, tk, Q_PER_KV * tq), F32),
        ],
        compiler_params=_cparams(("parallel", "parallel", "parallel")),
        name="attention",
    )(qt, k_heads, vt_blocks)


def _s5_matrices(a_re, a_im, log_dt, b_re, b_im, c_re, c_im, d_skip):
    hp = lax.Precision.HIGHEST
    L, G, P, H = SSM_CHUNK, N_SSM_GROUPS, SSM_STATE, SSM_GROUP
    f = lambda v: v.astype(F32)
    lr, li = f(a_re), f(a_im)
    dt = jnp.exp(f(log_dt))[..., None]
    mag = jnp.exp(lr * dt)
    ar = mag * jnp.cos(li * dt)
    ai = mag * jnp.sin(li * dt)
    den = lr * lr + li * li
    kr = ((ar - 1.0) * lr + ai * li) / den
    ki = (ai * lr - (ar - 1.0) * li) / den
    br, bi = f(b_re), f(b_im)
    bbr = kr[..., None] * br - ki[..., None] * bi
    bbi = kr[..., None] * bi + ki[..., None] * br
    kk = jnp.arange(L + 1, dtype=F32)[:, None, None, None]
    pmag = jnp.exp(kk * (lr * dt)[None])
    ph = kk * (li * dt)[None]
    pr = pmag * jnp.cos(ph)
    pi = pmag * jnp.sin(ph)
    cr, ci = f(c_re), f(c_im)
    car = cr[None] * pr[:, :, :, None, :] - ci[None] * pi[:, :, :, None, :]
    cai = cr[None] * pi[:, :, :, None, :] + ci[None] * pr[:, :, :, None, :]
    kern = (jnp.einsum('kdgop,dgpi->kdgoi', car, bbr, precision=hp)
            - jnp.einsum('kdgop,dgpi->kdgoi', cai, bbi, precision=hp))
    jj = jnp.arange(L)[:, None]
    ii = jnp.arange(L)[None, :]

    def toeplitz(kd, lag, valid):
        blocks = jnp.where(valid[:, :, None, None, None], kd[jnp.clip(lag, 0, L)], 0.0)
        return blocks.transpose(2, 0, 4, 1, 3).reshape(G, CHUNK_W, CHUNK_W)

    msum = toeplitz(kern[:, 0], ii - jj, ii >= jj) + toeplitz(kern[:, 1], jj - ii, jj >= ii)

    def in_op(d, powers):
        prj, pij = pr[powers, d], pi[powers, d]
        re = prj[..., None] * bbr[d][None] - pij[..., None] * bbi[d][None]
        im = prj[..., None] * bbi[d][None] + pij[..., None] * bbr[d][None]
        to = lambda m: m.transpose(1, 0, 3, 2).reshape(G, CHUNK_W, P)
        return to(re), to(im)

    def out_op(d, powers):
        to = lambda m: m.transpose(1, 3, 0, 2).reshape(G, P, CHUNK_W)
        return to(car[powers, d]), to(-cai[powers, d])

    steps = jnp.arange(L)
    bm = jnp.stack(in_op(0, L - 1 - steps) + in_op(1, steps), axis=1)
    cm = jnp.stack(out_op(0, steps + 1) + out_op(1, L - steps), axis=1)
    al = jnp.stack([pr[L, 0], pi[L, 0], pr[L, 1], pi[L, 1]], axis=1)[:, :, None, :]
    dvec = jnp.tile(f(d_skip).reshape(G, 1, H), (1, 1, L))
    z_in = jnp.zeros((G // 2, 4, CHUNK_W, P), F32)
    z_out = jnp.zeros((G // 2, 4, P, CHUNK_W), F32)
    bm_pair = jnp.concatenate([jnp.concatenate([bm[0::2], z_in], axis=3),
                               jnp.concatenate([z_in, bm[1::2]], axis=3)], axis=2)
    cm_pair = jnp.concatenate([jnp.concatenate([cm[0::2], z_out], axis=3),
                               jnp.concatenate([z_out, cm[1::2]], axis=3)], axis=2)
    al_pair = jnp.concatenate([al[0::2], al[1::2]], axis=3)
    return msum.astype(BF16), bm_pair.astype(BF16), cm_pair.astype(BF16), al_pair, dvec


def _s5_body(u_ref, m_ref, bm_ref, cm_ref, al_ref, d_ref, y_ref, st_sc, *, nb, nc):
    u0, u1 = u_ref[0], u_ref[1]
    ub = jnp.concatenate([u0.astype(BF16), u1.astype(BF16)], axis=1)
    for d in range(4):
        st_sc[d] = _dot(ub, bm_ref[0, d])
    al = [al_ref[0, d] for d in range(4)]

    def advance(rows, d, sr, si):
        lr = st_sc[d, rows, :]
        li = st_sc[d + 1, rows, :]
        st_sc[d, rows, :] = sr
        st_sc[d + 1, rows, :] = si
        ar, ai = al[d], al[d + 1]
        return ar * sr - ai * si + lr, ar * si + ai * sr + li

    def step(k, st):
        fr, fi, br, bi = st
        fr, fi = advance(pl.ds(k, nb, stride=nc), 0, fr, fi)
        br, bi = advance(pl.ds(nc - 1 - k, nb, stride=nc), 2, br, bi)
        return fr, fi, br, bi

    lax.fori_loop(0, nc, step, tuple(jnp.zeros((nb, LANES), F32) for _ in range(4)))

    y = _dot(st_sc[0].astype(BF16), cm_ref[0, 0])
    for d in range(1, 4):
        y = y + _dot(st_sc[d].astype(BF16), cm_ref[0, d])
    y_ref[0] = y[:, :CHUNK_W] + _dot(ub[:, :CHUNK_W], m_ref[0]) + u0 * d_ref[0]
    y_ref[1] = y[:, CHUNK_W:] + _dot(ub[:, CHUNK_W:], m_ref[1]) + u1 * d_ref[1]


def _s5_scan(ug, mats, nb, nc):
    msum, bm, cm, al, dvec = mats
    rows = nb * nc
    pair3 = lambda g: (g, 0, 0)
    pair4 = lambda g: (g, 0, 0, 0)
    return pl.pallas_call(
        functools.partial(_s5_body, nb=nb, nc=nc),
        grid=(N_SSM_GROUPS // 2,),
        in_specs=[
            pl.BlockSpec((2, rows, CHUNK_W), pair3),
            pl.BlockSpec((2, CHUNK_W, CHUNK_W), pair3),
            pl.BlockSpec((1, 4, 2 * CHUNK_W, LANES), pair4),
            pl.BlockSpec((1, 4, LANES, 2 * CHUNK_W), pair4),
            pl.BlockSpec((1, 4, 1, LANES), pair4),
            pl.BlockSpec((2, 1, CHUNK_W), pair3),
        ],
        out_specs=pl.BlockSpec((2, rows, CHUNK_W), pair3),
        out_shape=jax.ShapeDtypeStruct((N_SSM_GROUPS, rows, CHUNK_W), F32),
        scratch_shapes=[pltpu.VMEM((4, rows, LANES), F32)],
        compiler_params=_cparams(("parallel",)),
        name="s5_scan",
    )(ug, msum, bm, cm, al, dvec)


def _route(logits):
    lane = lax.broadcasted_iota(jnp.int32, logits.shape, 1)
    neg = -jnp.inf
    big = jnp.int32(LANES)
    is_g = lane < N_EXPERT_GROUPS
    gl = jnp.where(is_g, logits, neg)
    gmax = jnp.max(gl, axis=-1, keepdims=True)
    gidx = jnp.min(jnp.where(gl == gmax, lane, big), axis=-1, keepdims=True)
    g_w = 1.0 / jnp.sum(jnp.where(is_g, jnp.exp(logits - gmax), 0.0), axis=-1, keepdims=True)
    e_lane = lane - ROUTE_E0
    in_group = (e_lane >= 0) & (e_lane < N_EXPERTS) & ((e_lane // EXPERTS_PER_GROUP) == gidx)
    el = jnp.where(in_group, logits, neg)
    v1 = jnp.max(el, axis=-1, keepdims=True)
    i1 = jnp.min(jnp.where(el == v1, lane, big), axis=-1, keepdims=True)
    el2 = jnp.where(lane == i1, neg, el)
    v2 = jnp.max(el2, axis=-1, keepdims=True)
    i2 = jnp.min(jnp.where(el2 == v2, lane, big), axis=-1, keepdims=True)
    e2 = jnp.exp(v2 - v1)
    inv = g_w / (1.0 + e2)
    combine = jnp.where(lane == i1, inv, 0.0) + jnp.where(lane == i2, e2 * inv, 0.0)
    return combine, (lane == i1) | (lane == i2)


AUX_LANE = LANES - 1


def _slot_tables(cap):
    slot = jnp.arange(N_EXPERTS * cap)
    lane = jnp.arange(LANES)[:, None]
    expand = (lane == ROUTE_E0 + slot // cap).astype(F32)
    expand_aug = jnp.where(lane == AUX_LANE, -(slot % cap).astype(F32), expand)
    return expand.astype(BF16), expand_aug.astype(BF16)


def _merge_body(x_ref, ot_ref, yg_ref, ga_ref, gs_ref, wua_ref, wglu_ref, bglu_ref, wus_ref, wout_ref,
                g1_ref, b1_ref, wrh_ref, wrl_ref, br_ref, tri_ref, eaugt_ref,
                x1_ref, route_ref, rank_ref, xs_ref, y_sc, *, cap):
    tm = x_ref.shape[0]
    n_chunks = tm // SSM_CHUNK
    for gh in range(D_SSM // LANES):
        for ih in range(SSM_CHUNK // LANE_BLOCKS):
            blocks = [yg_ref[LANE_BLOCKS * gh + gl, :, ih * LANES:(ih + 1) * LANES] for gl in range(LANE_BLOCKS)]
            for a, blk in enumerate(_block_transpose(blocks)):
                y_sc[gh, pl.ds(LANE_BLOCKS * ih + a, n_chunks, stride=SSM_CHUNK), :] = blk
    y = jnp.concatenate([y_sc[gh] for gh in range(D_SSM // LANES)], axis=1)
    o = ot_ref[...].reshape(D_ATTN, tm).astype(F32).T.astype(BF16)
    z = jax.nn.gelu(y)
    z = z * jax.nn.sigmoid(_dot(z.astype(BF16), wglu_ref[...]) + bglu_ref[...])
    ssm_up = _dot(z.astype(BF16), wus_ref[...])
    att_up = _dot(o, wua_ref[...])
    merged = ga_ref[...].astype(F32) * att_up + gs_ref[...].astype(F32) * ssm_up
    mix = _dot(merged.astype(BF16), wout_ref[...])
    x1 = _layer_norm(ALPHA * x_ref[...] + mix, g1_ref[...], b1_ref[...])
    x1_ref[...] = x1
    xh, xl = _split_bf16(x1)
    logits = _dot(xh, wrh_ref[...]) + (_dot(xh, wrl_ref[...]) + _dot(xl, wrh_ref[...])) + br_ref[...]
    combine, sel = _route(logits)
    route_ref[...] = combine
    rank = _dot(tri_ref[...], jnp.where(sel, 1.0, 0.0).astype(BF16))
    lane = lax.broadcasted_iota(jnp.int32, rank.shape, 1)
    rank = jnp.where(sel, jnp.minimum(rank, float(cap)), float(cap))
    rank = jnp.where(lane == AUX_LANE, 1.0, rank).astype(BF16)
    rank_ref[...] = rank
    miss = lax.dot_general(eaugt_ref[...], rank, (((1,), (1,)), ((), ())), preferred_element_type=F32)
    gather = jnp.where(miss == 0.0, 1.0, 0.0).astype(BF16)
    xs_ref[0] = _dot(gather, xh).astype(BF16)


def _merge_weights(w, tm, cap):
    n_r = N_EXPERT_GROUPS + N_EXPERTS
    wr = jnp.zeros((D_MODEL, LANES), F32).at[:, :n_r].set(
        jnp.concatenate([w['w_router_group'], w['w_router_expert']], axis=1).astype(F32))
    br = jnp.zeros((1, LANES), F32).at[0, :n_r].set(
        jnp.concatenate([w['b_router_group'], w['b_router_expert']]).astype(F32))
    wrh, wrl = _split_bf16(wr)
    tri = (jnp.arange(tm)[:, None] > jnp.arange(tm)[None, :]).astype(BF16)
    _, expand_aug = _slot_tables(cap)
    return [w['w_up_attn'].astype(BF16), w['w_glu'].astype(BF16), w['b_glu'].astype(F32)[None, :],
            w['w_up_ssm'].astype(BF16), w['w_out'].astype(BF16), w['ln1_g'].astype(F32)[None, :],
            w['ln1_b'].astype(F32)[None, :], wrh, wrl, br, tri, expand_aug.T]


def _merge(x, ot, yg, ga, gs, weights, tm, cap):
    n = x.shape[0]
    n_slot = N_EXPERTS * cap
    row = lambda i: (i, 0)
    const = lambda i: (0, 0)
    full = lambda a: pl.BlockSpec(a.shape, const)
    return pl.pallas_call(
        functools.partial(_merge_body, cap=cap),
        grid=(n // tm,),
        in_specs=[pl.BlockSpec((tm, D_MODEL), row),
                  pl.BlockSpec((N_Q_HEADS, HEAD_DIM, tm), lambda i: (0, 0, i)),
                  pl.BlockSpec((N_SSM_GROUPS, tm // SSM_CHUNK, CHUNK_W), lambda i: (0, i, 0)),
                  pl.BlockSpec((tm, D_MODEL), row), pl.BlockSpec((tm, D_MODEL), row)] + [full(a) for a in weights],
        out_specs=[pl.BlockSpec((tm, D_MODEL), row), pl.BlockSpec((tm, LANES), row), pl.BlockSpec((tm, LANES), row),
                   pl.BlockSpec((1, n_slot, D_MODEL), lambda i: (i, 0, 0))],
        out_shape=[jax.ShapeDtypeStruct((n, D_MODEL), F32), jax.ShapeDtypeStruct((n, LANES), F32),
                   jax.ShapeDtypeStruct((n, LANES), BF16), jax.ShapeDtypeStruct((n // tm, n_slot, D_MODEL), BF16)],
        scratch_shapes=[pltpu.VMEM((D_SSM // LANES, tm, LANES), F32)],
        compiler_params=_cparams(("parallel",)),
        name="merge_ln1_router",
    )(x, ot, yg, ga, gs, *weights)


def _expert_mlp(xb, wg, wu, wd):
    h = jax.nn.silu(_dot(xb, wg)) * _dot(xb, wu)
    return _dot(h.astype(BF16), wd)


def _experts_body(xs_ref, wg_ref, wu_ref, wd_ref, ys_ref):
    shp = xs_ref.shape
    xb = xs_ref[...].reshape(shp[0] * shp[2], D_MODEL)
    ys_ref[...] = _expert_mlp(xb, wg_ref[0], wu_ref[0], wd_ref[0]).astype(BF16).reshape(shp)


def _experts(xs, w_gate, w_up, w_down, cap, tiles_per_step):
    n_tiles = xs.shape[0]
    xs4 = xs.reshape(n_tiles, N_EXPERTS, cap, D_MODEL)
    slots = pl.BlockSpec((tiles_per_step, 1, cap, D_MODEL), lambda e, t: (t, e, 0, 0))
    exp3 = lambda e, t: (e, 0, 0)
    ys = pl.pallas_call(
        _experts_body,
        grid=(N_EXPERTS, n_tiles // tiles_per_step),
        in_specs=[slots, pl.BlockSpec((1, D_MODEL, D_FF_EXPERT), exp3), pl.BlockSpec((1, D_MODEL, D_FF_EXPERT), exp3),
                  pl.BlockSpec((1, D_FF_EXPERT, D_MODEL), exp3)],
        out_specs=slots,
        out_shape=jax.ShapeDtypeStruct(xs4.shape, BF16),
        compiler_params=_cparams(("parallel", "parallel")),
        name="experts",
    )(xs4, w_gate, w_up, w_down)
    return ys.reshape(xs.shape)


def _combine_body(x1_ref, route_ref, rank_ref, ys_ref, eaug_ref, e_ref, g2_ref, b2_ref, wg_hbm, wu_hbm, wd_hbm,
                  o_ref, acc_sc, wg_sc, wu_sc, wd_sc, *, cap):
    route = route_ref[...]
    rank = rank_ref[...]
    hit = _dot(rank, eaug_ref[...]) == 0.0
    wh, wl = _split_bf16(route)
    spread = _dot(wh, e_ref[...]) + _dot(wl, e_ref[...])
    acc_sc[...] = _dot(jnp.where(hit, spread, 0.0).astype(BF16), ys_ref[0])
    lane = lax.broadcasted_iota(jnp.int32, route.shape, 1)
    spilled = jnp.where((rank.astype(F32) == float(cap)) & (lane != AUX_LANE), route, 0.0)

    @pl.when(jnp.max(jnp.abs(spilled)) > 0.0)
    def _():
        xb = x1_ref[...].astype(BF16)

        def one_expert(e, carry):
            pltpu.sync_copy(wg_hbm.at[e], wg_sc)
            pltpu.sync_copy(wu_hbm.at[e], wu_sc)
            pltpu.sync_copy(wd_hbm.at[e], wd_sc)
            w_e = jnp.sum(jnp.where(lane == e + ROUTE_E0, spilled, 0.0), axis=-1, keepdims=True)
            acc_sc[...] += w_e * _expert_mlp(xb, wg_sc[...], wu_sc[...], wd_sc[...])
            return carry

        lax.fori_loop(0, N_EXPERTS, one_expert, 0)

    o_ref[...] = _layer_norm(ALPHA * x1_ref[...] + acc_sc[...], g2_ref[...], b2_ref[...])


def _combine(x1, route, rank, ys, w_gate, w_up, w_down, ln2_g, ln2_b, tm, cap):
    n = x1.shape[0]
    n_slot = N_EXPERTS * cap
    expand, expand_aug = _slot_tables(cap)
    row = lambda i: (i, 0)
    const = lambda i: (0, 0)
    hbm = pl.BlockSpec(memory_space=pl.ANY)
    return pl.pallas_call(
        functools.partial(_combine_body, cap=cap),
        grid=(n // tm,),
        in_specs=[pl.BlockSpec((tm, D_MODEL), row), pl.BlockSpec((tm, LANES), row), pl.BlockSpec((tm, LANES), row),
                  pl.BlockSpec((1, n_slot, D_MODEL), lambda i: (i, 0, 0)),
                  pl.BlockSpec((LANES, n_slot), const), pl.BlockSpec((LANES, n_slot), const),
                  pl.BlockSpec((1, D_MODEL), const), pl.BlockSpec((1, D_MODEL), const), hbm, hbm, hbm],
        out_specs=pl.BlockSpec((tm, D_MODEL), row),
        out_shape=jax.ShapeDtypeStruct((n, D_MODEL), F32),
        scratch_shapes=[pltpu.VMEM((tm, D_MODEL), F32), pltpu.VMEM((D_MODEL, D_FF_EXPERT), BF16),
                        pltpu.VMEM((D_MODEL, D_FF_EXPERT), BF16), pltpu.VMEM((D_FF_EXPERT, D_MODEL), BF16)],
        compiler_params=_cparams(("parallel",)),
        name="combine_ln2",
    )(x1, route, rank, ys, expand_aug, expand, ln2_g.astype(F32)[None, :], ln2_b.astype(F32)[None, :],
      w_gate, w_up, w_down)


def _encode(x, w, prep, cap):
    b, t, _ = x.shape
    n = b * t
    tm, tq, tk = math.gcd(512, t), math.gcd(256, t), math.gcd(512, t // 2)
    x2 = x.reshape(n, D_MODEL).astype(F32)
    qt, kh, vt, ug, ga, gs = _in_projection(x2, t, prep['w_in'], w['q_norm_g'], w['k_norm_g'], tm, tk)
    ot = _attention(qt, kh, vt, b, t, tq, tk)
    yg = _s5_scan(ug, prep['s5'], b, t // SSM_CHUNK)
    x1, route, rank, xs = _merge(x2, ot, yg, ga, gs, _merge_weights(w, tm, cap), tm, cap)
    wg, wu, wd = prep['experts']
    ys = _experts(xs, wg, wu, wd, cap, math.gcd(8, n // tm))
    out = _combine(x1, route, rank, ys, wg, wu, wd, w['ln2_g'], w['ln2_b'], tm, cap)
    return out.reshape(b, t, D_MODEL)


def kernel(x_prompt, x_sample, w_in, q_norm_g, k_norm_g, ssm_a_re, ssm_a_im, ssm_log_dt, ssm_b_re, ssm_b_im, ssm_c_re, ssm_c_im, ssm_d, w_glu, b_glu, w_up_attn, w_up_ssm, w_out, ln1_g, ln1_b, w_router_group, b_router_group, w_router_expert, b_router_expert, w_exp_gate, w_exp_up, w_exp_down, ln2_g, ln2_b):
    w = dict(w_in=w_in, q_norm_g=q_norm_g, k_norm_g=k_norm_g, ssm_a_re=ssm_a_re, ssm_a_im=ssm_a_im,
             ssm_log_dt=ssm_log_dt, ssm_b_re=ssm_b_re, ssm_b_im=ssm_b_im, ssm_c_re=ssm_c_re, ssm_c_im=ssm_c_im,
             ssm_d=ssm_d, w_glu=w_glu, b_glu=b_glu, w_up_attn=w_up_attn, w_up_ssm=w_up_ssm, w_out=w_out,
             ln1_g=ln1_g, ln1_b=ln1_b, w_router_group=w_router_group, b_router_group=b_router_group,
             w_router_expert=w_router_expert, b_router_expert=b_router_expert, w_exp_gate=w_exp_gate,
             w_exp_up=w_exp_up, w_exp_down=w_exp_down, ln2_g=ln2_g, ln2_b=ln2_b)
    w = {name: val[0] for name, val in w.items()}
    prep = dict(
        w_in=w['w_in'].astype(BF16),
        s5=_s5_matrices(w['ssm_a_re'], w['ssm_a_im'], w['ssm_log_dt'], w['ssm_b_re'], w['ssm_b_im'],
                        w['ssm_c_re'], w['ssm_c_im'], w['ssm_d']),
        experts=tuple(w[name].astype(BF16) for name in ('w_exp_gate', 'w_exp_up', 'w_exp_down')))
    return _encode(x_prompt, w, prep, EXPERT_CAP), _encode(x_sample, w, prep, EXPERT_CAP)
```

```python
import functools
import math

import jax
import jax.numpy as jnp
from jax import lax
from jax.experimental import pallas as pl
from jax.experimental.pallas import tpu as pltpu

F32 = jnp.float32
BF16 = jnp.bfloat16

D_MODEL = 1024
GRID_W = 64
HEAD_DIM = 64
N_Q_HEADS = 8
N_KV_HEADS = 2
Q_PER_KV = N_Q_HEADS // N_KV_HEADS
D_ATTN = N_Q_HEADS * HEAD_DIM
D_KV = N_KV_HEADS * HEAD_DIM
ROPE_THETA = 10000.0
D_SSM = D_MODEL // 2
SSM_GROUP = 16
N_SSM_GROUPS = D_SSM // SSM_GROUP
SSM_STATE = 64
N_EXPERT_GROUPS = 4
EXPERTS_PER_GROUP = 4
N_EXPERTS = N_EXPERT_GROUPS * EXPERTS_PER_GROUP
D_FF_EXPERT = D_MODEL // 2
DEPTH = 1
ALPHA = (2.0 * DEPTH) ** 0.25
EPS = 1e-6
D_IN_PROJ = D_ATTN + 2 * D_KV + D_SSM + 2 * D_MODEL
OFF_K = D_ATTN
OFF_V = OFF_K + D_KV
OFF_U = OFF_V + D_KV
OFF_GA = OFF_U + D_SSM
OFF_GS = OFF_GA + D_MODEL

LANES = 128
MXU_DIM = 256
SSM_CHUNK = 16
CHUNK_W = SSM_CHUNK * SSM_GROUP
ROUTE_E0 = N_EXPERT_GROUPS
VMEM_LIMIT = 56 * 1024 * 1024
EXPERT_CAP = 96


def _cparams(sem):
    return pltpu.CompilerParams(dimension_semantics=sem, vmem_limit_bytes=VMEM_LIMIT)


def _dot(a, b):
    return jnp.dot(a, b, preferred_element_type=F32)


def _split_bf16(x):
    hi = x.astype(BF16)
    lo = (x - hi.astype(F32)).astype(BF16)
    return hi, lo


def _layer_norm(h, g, b):
    mu = jnp.mean(h, axis=-1, keepdims=True)
    var = jnp.mean(jnp.square(h - mu), axis=-1, keepdims=True)
    return (h - mu) * lax.rsqrt(var + EPS) * g + b


QK_SCALE = HEAD_DIM ** -0.5 * math.log2(math.e)
MAX_UNSHIFTED_EXPONENT = 40.0
V_ROWS = HEAD_DIM + 16
LANE_BLOCKS = LANES // SSM_GROUP


def _block_transpose(blocks):
    lane = lax.broadcasted_iota(jnp.int32, blocks[0].shape, 1)
    v = list(blocks)
    for s in range(3):
        d = SSM_GROUP << s
        keep = ((lane // d) % 2) == 0
        nxt = list(v)
        for a in range(LANE_BLOCKS):
            if not a & (1 << s):
                a2 = a | (1 << s)
                nxt[a] = jnp.where(keep, v[a], pltpu.roll(v[a2], d, 1))
                nxt[a2] = jnp.where(keep, pltpu.roll(v[a], LANES - d, 1), v[a2])
        v = nxt
    return v


def _inproj_body(x_ref, w_ref, gq_ref, gk_ref, cos_ref, sin_ref, seg_ref,
                 qt_ref, kh_ref, vt_ref, ug_ref, ga_ref, gs_ref, u_sc, *, tk):
    xb = x_ref[...].astype(BF16)
    tm = xb.shape[0]
    cos = cos_ref[...]
    sin = sin_ref[...]
    seg = seg_ref[...]
    lane = lax.broadcasted_iota(jnp.int32, (1, LANES), 1)
    first_half = (lane % HEAD_DIM) < (HEAD_DIM // 2)

    def mm(a, b):
        return _dot(xb, w_ref[:, a:b])

    def head_sumsq(y):
        w = y.shape[1]
        return _dot((y * y).astype(BF16), seg[0:w, 0:w])

    def norm_rope(y, ss, g):
        yn = y * lax.rsqrt(ss * (1.0 / HEAD_DIM) + EPS) * g
        sw = jnp.where(first_half, pltpu.roll(yn, LANES - HEAD_DIM // 2, 1), pltpu.roll(yn, HEAD_DIM // 2, 1))
        return yn * cos + sw * sin

    gq = gq_ref[...]
    heads_per_slab = LANES // HEAD_DIM
    wide = seg.shape[0]
    for j in range(D_ATTN // wide):
        y = mm(j * wide, (j + 1) * wide)
        ss = head_sumsq(y)
        for jj in range(wide // LANES):
            q = norm_rope(y[:, jj * LANES:(jj + 1) * LANES], ss[:, jj * LANES:(jj + 1) * LANES], gq)
            qt = (q * QK_SCALE).T
            for h in range(heads_per_slab):
                head = (j * (wide // LANES) + jj) * heads_per_slab + h
                qt_ref[head] = qt[h * HEAD_DIM:(h + 1) * HEAD_DIM].astype(BF16)
    yk = mm(OFF_K, OFF_V)
    k = norm_rope(yk, head_sumsq(yk), gk_ref[...])
    vt = mm(OFF_V, OFF_U).astype(BF16).astype(F32).T
    ones_tile = jnp.where(lax.broadcasted_iota(jnp.int32, (V_ROWS - HEAD_DIM, tk), 0) == 0, 1.0, 0.0).astype(BF16)
    for g in range(N_KV_HEADS):
        kh_ref[g] = k[:, g * HEAD_DIM:(g + 1) * HEAD_DIM].astype(BF16)
        for jb in range(tm // tk):
            vt_ref[g, jb, 0:HEAD_DIM] = vt[g * HEAD_DIM:(g + 1) * HEAD_DIM, jb * tk:(jb + 1) * tk].astype(BF16)
            vt_ref[g, jb, HEAD_DIM:V_ROWS] = ones_tile
    u = mm(OFF_U, OFF_GA)
    n_tiles = D_SSM // LANES
    for gh in range(n_tiles):
        u_sc[gh] = u[:, gh * LANES:(gh + 1) * LANES]
    n_chunks = tm // SSM_CHUNK
    for gh in range(n_tiles):
        for ih in range(SSM_CHUNK // LANE_BLOCKS):
            rows = [u_sc[gh, pl.ds(LANE_BLOCKS * ih + a, n_chunks, stride=SSM_CHUNK), :] for a in range(LANE_BLOCKS)]
            for gl, blk in enumerate(_block_transpose(rows)):
                ug_ref[LANE_BLOCKS * gh + gl, :, ih * LANES:(ih + 1) * LANES] = blk
    ga_ref[...] = jax.nn.sigmoid(mm(OFF_GA, OFF_GS)).astype(BF16)
    gs_ref[...] = jax.nn.sigmoid(mm(OFF_GS, D_IN_PROJ)).astype(BF16)


def _rope_tables(t_max):
    t = jnp.arange(t_max, dtype=jnp.int32)
    row_pos = (t // GRID_W).astype(F32)
    col_pos = (t % GRID_W).astype(F32)
    n_freq = HEAD_DIM // 4
    inv_freq = ROPE_THETA ** (-jnp.arange(n_freq, dtype=F32) / n_freq)
    ang = jnp.concatenate([row_pos[:, None] * inv_freq, col_pos[:, None] * inv_freq], axis=-1)
    c, s = jnp.cos(ang), jnp.sin(ang)
    reps = LANES // HEAD_DIM
    cos_t = jnp.tile(jnp.concatenate([c, c], axis=-1), (1, reps))
    sin_t = jnp.tile(jnp.concatenate([-s, s], axis=-1), (1, reps))
    return cos_t, sin_t


def _in_projection(x, t, w_in_bf16, q_norm_g, k_norm_g, tm, tk):
    n = x.shape[0]
    tiles_per_seq = t // tm
    cos_t, sin_t = _rope_tables(t)
    reps = LANES // HEAD_DIM
    gq = jnp.tile(q_norm_g.astype(F32), reps)[None, :]
    gk = jnp.tile(k_norm_g.astype(F32), reps)[None, :]
    head_of_lane = jnp.arange(MXU_DIM) // HEAD_DIM
    seg = (head_of_lane[:, None] == head_of_lane[None, :]).astype(BF16)
    pos_map = lambda i: (i % tiles_per_seq, 0)
    row = lambda i: (i, 0)
    const = lambda i: (0, 0)
    return pl.pallas_call(
        functools.partial(_inproj_body, tk=tk),
        grid=(n // tm,),
        in_specs=[
            pl.BlockSpec((tm, D_MODEL), row),
            pl.BlockSpec((D_MODEL, D_IN_PROJ), const),
            pl.BlockSpec((1, LANES), const),
            pl.BlockSpec((1, LANES), const),
            pl.BlockSpec((tm, LANES), pos_map),
            pl.BlockSpec((tm, LANES), pos_map),
            pl.BlockSpec((MXU_DIM, MXU_DIM), const),
        ],
        out_specs=[
            pl.BlockSpec((N_Q_HEADS, HEAD_DIM, tm), lambda i: (0, 0, i)),
            pl.BlockSpec((N_KV_HEADS, tm, HEAD_DIM), lambda i: (0, i, 0)),
            pl.BlockSpec((N_KV_HEADS, tm // tk, V_ROWS, tk), lambda i: (0, i, 0, 0)),
            pl.BlockSpec((N_SSM_GROUPS, tm // SSM_CHUNK, CHUNK_W), lambda i: (0, i, 0)),
            pl.BlockSpec((tm, D_MODEL), row),
            pl.BlockSpec((tm, D_MODEL), row),
        ],
        out_shape=[
            jax.ShapeDtypeStruct((N_Q_HEADS, HEAD_DIM, n), BF16),
            jax.ShapeDtypeStruct((N_KV_HEADS, n, HEAD_DIM), BF16),
            jax.ShapeDtypeStruct((N_KV_HEADS, n // tk, V_ROWS, tk), BF16),
            jax.ShapeDtypeStruct((N_SSM_GROUPS, n // SSM_CHUNK, CHUNK_W), F32),
            jax.ShapeDtypeStruct((n, D_MODEL), BF16),
            jax.ShapeDtypeStruct((n, D_MODEL), BF16),
        ],
        scratch_shapes=[pltpu.VMEM((D_SSM // LANES, tm, LANES), F32)],
        compiler_params=_cparams(("parallel",)),
        name="in_projection",
    )(x, w_in_bf16, gq, gk, cos_t, sin_t, seg)


def _attn_body(qt_ref, k_ref, vt_ref, o_ref, m_sc, acc_sc, s_sc, *, tq, tk, n_kv, shift):
    q4t = jnp.concatenate([qt_ref[h] for h in range(Q_PER_KV)], axis=1)
    m_sc[...] = jnp.full(m_sc.shape, -jnp.inf, F32)
    acc_sc[...] = jnp.zeros(acc_sc.shape, F32)

    def scores(j):
        kk = k_ref[0, pl.ds(pl.multiple_of(j * tk, tk), tk), :]
        return _dot(kk, q4t)

    def softmax_pv(s, j):
        if not shift:
            acc_sc[...] += _dot(vt_ref[0, j], jnp.exp2(s).astype(BF16))
            return
        m_old = m_sc[...]
        m_new = jnp.maximum(m_old, jnp.max(s, axis=0, keepdims=True))
        p = jnp.exp2(s - m_new).astype(BF16)
        acc_sc[...] = jnp.exp2(m_old - m_new) * acc_sc[...] + _dot(vt_ref[0, j], p)
        m_sc[...] = m_new

    s_sc[0] = scores(0)

    def body(jj, carry):
        j = 2 * jj
        s_sc[1] = scores(j + 1)
        softmax_pv(s_sc[0], j)
        s_sc[0] = scores(jnp.minimum(j + 2, n_kv - 1))
        softmax_pv(s_sc[1], j + 1)
        return carry

    lax.fori_loop(0, n_kv // 2, body, 0)
    out = acc_sc[0:HEAD_DIM, :] / acc_sc[HEAD_DIM:HEAD_DIM + 1, :]
    for h in range(Q_PER_KV):
        o_ref[h] = out[:, h * tq:(h + 1) * tq].astype(o_ref.dtype)


def _attention(qt, k_heads, vt_blocks, q_norm_g, k_norm_g, b, t, tq, tk):
    n_qt = t // tq
    n_kv = t // tk
    bound = QK_SCALE * HEAD_DIM * jnp.max(jnp.abs(q_norm_g.astype(F32))) * jnp.max(jnp.abs(k_norm_g.astype(F32)))
    call = functools.partial(_attention_call, b=b, t=t, tq=tq, tk=tk, n_qt=n_qt, n_kv=n_kv)
    return lax.cond(bound < MAX_UNSHIFTED_EXPONENT,
                    functools.partial(call, shift=False), functools.partial(call, shift=True),
                    qt, k_heads, vt_blocks)


def _attention_call(qt, k_heads, vt_blocks, *, b, t, tq, tk, n_qt, n_kv, shift):
    body = functools.partial(_attn_body, tq=tq, tk=tk, n_kv=n_kv, shift=shift)
    return pl.pallas_call(
        body,
        grid=(b, N_KV_HEADS, n_qt),
        in_specs=[
            pl.BlockSpec((Q_PER_KV, HEAD_DIM, tq), lambda bi, g, qi: (g, 0, bi * n_qt + qi)),
            pl.BlockSpec((1, t, HEAD_DIM), lambda bi, g, qi: (g, bi, 0)),
            pl.BlockSpec((1, n_kv, V_ROWS, tk), lambda bi, g, qi: (g, bi, 0, 0)),
        ],
        out_specs=pl.BlockSpec((Q_PER_KV, HEAD_DIM, tq), lambda bi, g, qi: (g, 0, bi * n_qt + qi)),
        out_shape=jax.ShapeDtypeStruct((N_Q_HEADS, HEAD_DIM, b * t), BF16),
        scratch_shapes=[
            pltpu.VMEM((1, Q_PER_KV * tq), F32),
            pltpu.VMEM((V_ROWS, Q_PER_KV * tq), F32),
            pltpu.VMEM((2, tk, Q_PER_KV * tq), F32),
        ],
        compiler_params=_cparams(("parallel", "parallel", "parallel")),
        name="attention",
    )(qt, k_heads, vt_blocks)


def _s5_matrices(a_re, a_im, log_dt, b_re, b_im, c_re, c_im, d_skip):
    hp = lax.Precision.HIGHEST
    L, G, P, H = SSM_CHUNK, N_SSM_GROUPS, SSM_STATE, SSM_GROUP
    f = lambda v: v.astype(F32)
    lr, li = f(a_re), f(a_im)
    dt = jnp.exp(f(log_dt))[..., None]
    mag = jnp.exp(lr * dt)
    ar = mag * jnp.cos(li * dt)
    ai = mag * jnp.sin(li * dt)
    den = lr * lr + li * li
    kr = ((ar - 1.0) * lr + ai * li) / den
    ki = (ai * lr - (ar - 1.0) * li) / den
    br, bi = f(b_re), f(b_im)
    bbr = kr[..., None] * br - ki[..., None] * bi
    bbi = kr[..., None] * bi + ki[..., None] * br
    kk = jnp.arange(L + 1, dtype=F32)[:, None, None, None]
    pmag = jnp.exp(kk * (lr * dt)[None])
    ph = kk * (li * dt)[None]
    pr = pmag * jnp.cos(ph)
    pi = pmag * jnp.sin(ph)
    cr, ci = f(c_re), f(c_im)
    car = cr[None] * pr[:, :, :, None, :] - ci[None] * pi[:, :, :, None, :]
    cai = cr[None] * pi[:, :, :, None, :] + ci[None] * pr[:, :, :, None, :]
    kern = (jnp.einsum('kdgop,dgpi->kdgoi', car, bbr, precision=hp)
            - jnp.einsum('kdgop,dgpi->kdgoi', cai, bbi, precision=hp))
    jj = jnp.arange(L)[:, None]
    ii = jnp.arange(L)[None, :]

    def toeplitz(kd, lag, valid):
        blocks = jnp.where(valid[:, :, None, None, None], kd[jnp.clip(lag, 0, L)], 0.0)
        return blocks.transpose(2, 0, 4, 1, 3).reshape(G, CHUNK_W, CHUNK_W)

    msum = toeplitz(kern[:, 0], ii - jj, ii >= jj) + toeplitz(kern[:, 1], jj - ii, jj >= ii)

    def in_op(d, powers):
        prj, pij = pr[powers, d], pi[powers, d]
        re = prj[..., None] * bbr[d][None] - pij[..., None] * bbi[d][None]
        im = prj[..., None] * bbi[d][None] + pij[..., None] * bbr[d][None]
        to = lambda m: m.transpose(1, 0, 3, 2).reshape(G, CHUNK_W, P)
        return to(re), to(im)

    def out_op(d, powers):
        to = lambda m: m.transpose(1, 3, 0, 2).reshape(G, P, CHUNK_W)
        return to(car[powers, d]), to(-cai[powers, d])

    steps = jnp.arange(L)
    bm = jnp.stack(in_op(0, L - 1 - steps) + in_op(1, steps), axis=1)
    cm = jnp.stack(out_op(0, steps + 1) + out_op(1, L - steps), axis=1)
    al = jnp.stack([pr[L, 0], pi[L, 0], pr[L, 1], pi[L, 1]], axis=1)[:, :, None, :]
    dvec = jnp.tile(f(d_skip).reshape(G, 1, H), (1, 1, L))
    z_in = jnp.zeros((G // 2, 4, CHUNK_W, P), F32)
    z_out = jnp.zeros((G // 2, 4, P, CHUNK_W), F32)
    bm_pair = jnp.concatenate([jnp.concatenate([bm[0::2], z_in], axis=3),
                               jnp.concatenate([z_in, bm[1::2]], axis=3)], axis=2)
    cm_pair = jnp.concatenate([jnp.concatenate([cm[0::2], z_out], axis=3),
                               jnp.concatenate([z_out, cm[1::2]], axis=3)], axis=2)
    al_pair = jnp.concatenate([al[0::2], al[1::2]], axis=3)
    bm_all = jnp.concatenate([bm_pair[:, d] for d in range(4)], axis=2)
    cm_all = jnp.concatenate([cm_pair[:, d] for d in range(4)], axis=1)
    return msum.astype(BF16), bm_all.astype(BF16), cm_all.astype(BF16), al_pair, dvec


def _s5_body(u_ref, m_ref, bm_ref, cm_ref, al_ref, d_ref, y_ref, st_sc, *, nb, nc):
    u0, u1 = u_ref[0], u_ref[1]
    ub = jnp.concatenate([u0.astype(BF16), u1.astype(BF16)], axis=1)
    local = _dot(ub, bm_ref[0])
    for d in range(4):
        st_sc[d] = local[:, d * LANES:(d + 1) * LANES]
    al = [al_ref[0, d] for d in range(4)]

    def advance(rows, d, sr, si):
        lr = st_sc[d, rows, :]
        li = st_sc[d + 1, rows, :]
        st_sc[d, rows, :] = sr
        st_sc[d + 1, rows, :] = si
        ar, ai = al[d], al[d + 1]
        return ar * sr - ai * si + lr, ar * si + ai * sr + li

    def step(k, st):
        fr, fi, br, bi = st
        fr, fi = advance(pl.ds(k, nb, stride=nc), 0, fr, fi)
        br, bi = advance(pl.ds(nc - 1 - k, nb, stride=nc), 2, br, bi)
        return fr, fi, br, bi

    lax.fori_loop(0, nc, step, tuple(jnp.zeros((nb, LANES), F32) for _ in range(4)))

    entering = jnp.concatenate([st_sc[d].astype(BF16) for d in range(4)], axis=1)
    y = _dot(entering, cm_ref[0])
    y_ref[0] = y[:, :CHUNK_W] + _dot(ub[:, :CHUNK_W], m_ref[0]) + u0 * d_ref[0]
    y_ref[1] = y[:, CHUNK_W:] + _dot(ub[:, CHUNK_W:], m_ref[1]) + u1 * d_ref[1]


def _s5_scan(ug, mats, nb, nc):
    msum, bm, cm, al, dvec = mats
    rows = nb * nc
    pair3 = lambda g: (g, 0, 0)
    pair4 = lambda g: (g, 0, 0, 0)
    return pl.pallas_call(
        functools.partial(_s5_body, nb=nb, nc=nc),
        grid=(N_SSM_GROUPS // 2,),
        in_specs=[
            pl.BlockSpec((2, rows, CHUNK_W), pair3),
            pl.BlockSpec((2, CHUNK_W, CHUNK_W), pair3),
            pl.BlockSpec((1, 2 * CHUNK_W, 4 * LANES), pair3),
            pl.BlockSpec((1, 4 * LANES, 2 * CHUNK_W), pair3),
            pl.BlockSpec((1, 4, 1, LANES), pair4),
            pl.BlockSpec((2, 1, CHUNK_W), pair3),
        ],
        out_specs=pl.BlockSpec((2, rows, CHUNK_W), pair3),
        out_shape=jax.ShapeDtypeStruct((N_SSM_GROUPS, rows, CHUNK_W), F32),
        scratch_shapes=[pltpu.VMEM((4, rows, LANES), F32)],
        compiler_params=_cparams(("parallel",)),
        name="s5_scan",
    )(ug, msum, bm, cm, al, dvec)


def _route(logits):
    lane = lax.broadcasted_iota(jnp.int32, logits.shape, 1)
    neg = -jnp.inf
    big = jnp.int32(LANES)
    is_g = lane < N_EXPERT_GROUPS
    gl = jnp.where(is_g, logits, neg)
    gmax = jnp.max(gl, axis=-1, keepdims=True)
    gidx = jnp.min(jnp.where(gl == gmax, lane, big), axis=-1, keepdims=True)
    g_w = 1.0 / jnp.sum(jnp.where(is_g, jnp.exp(logits - gmax), 0.0), axis=-1, keepdims=True)
    e_lane = lane - ROUTE_E0
    in_group = (e_lane >= 0) & (e_lane < N_EXPERTS) & ((e_lane // EXPERTS_PER_GROUP) == gidx)
    el = jnp.where(in_group, logits, neg)
    v1 = jnp.max(el, axis=-1, keepdims=True)
    i1 = jnp.min(jnp.where(el == v1, lane, big), axis=-1, keepdims=True)
    el2 = jnp.where(lane == i1, neg, el)
    v2 = jnp.max(el2, axis=-1, keepdims=True)
    i2 = jnp.min(jnp.where(el2 == v2, lane, big), axis=-1, keepdims=True)
    e2 = jnp.exp(v2 - v1)
    inv = g_w / (1.0 + e2)
    combine = jnp.where(lane == i1, inv, 0.0) + jnp.where(lane == i2, e2 * inv, 0.0)
    return combine, (lane == i1) | (lane == i2)


AUX_LANE = LANES - 1


def _slot_tables(cap):
    slot = jnp.arange(N_EXPERTS * cap)
    lane = jnp.arange(LANES)[:, None]
    expand = (lane == ROUTE_E0 + slot // cap).astype(F32)
    expand_aug = jnp.where(lane == AUX_LANE, -(slot % cap).astype(F32), expand)
    return expand.astype(BF16), expand_aug.astype(BF16)


def _merge_body(x_ref, ot_ref, yg_ref, ga_ref, gs_ref, wua_ref, wglu_ref, bglu_ref, wus_ref, wout_ref,
                g1_ref, b1_ref, wr_ref, br_ref, tri_ref, eaugt_ref,
                x1_ref, route_ref, rank_ref, xs_ref, y_sc, *, cap):
    tm = x_ref.shape[0]
    n_chunks = tm // SSM_CHUNK
    for gh in range(D_SSM // LANES):
        for ih in range(SSM_CHUNK // LANE_BLOCKS):
            blocks = [yg_ref[LANE_BLOCKS * gh + gl, :, ih * LANES:(ih + 1) * LANES] for gl in range(LANE_BLOCKS)]
            for a, blk in enumerate(_block_transpose(blocks)):
                y_sc[gh, pl.ds(LANE_BLOCKS * ih + a, n_chunks, stride=SSM_CHUNK), :] = blk
    y = jnp.concatenate([y_sc[gh] for gh in range(D_SSM // LANES)], axis=1)
    o = ot_ref[...].reshape(D_ATTN, tm).astype(F32).T.astype(BF16)
    z = jax.nn.gelu(y)
    z = z * jax.nn.sigmoid(_dot(z.astype(BF16), wglu_ref[...]) + bglu_ref[...])
    ssm_up = _dot(z.astype(BF16), wus_ref[...])
    att_up = _dot(o, wua_ref[...])
    merged = ga_ref[...].astype(F32) * att_up + gs_ref[...].astype(F32) * ssm_up
    mix = _dot(merged.astype(BF16), wout_ref[...])
    x1 = _layer_norm(ALPHA * x_ref[...] + mix, g1_ref[...], b1_ref[...])
    x1_ref[...] = x1
    xh = x1.astype(BF16)
    logits = _dot(xh, wr_ref[...]) + br_ref[...]
    combine, sel = _route(logits)
    route_ref[...] = combine
    rank = _dot(tri_ref[...], jnp.where(sel, 1.0, 0.0).astype(BF16))
    lane = lax.broadcasted_iota(jnp.int32, rank.shape, 1)
    rank = jnp.where(sel, jnp.minimum(rank, float(cap)), float(cap))
    rank = jnp.where(lane == AUX_LANE, 1.0, rank).astype(BF16)
    rank_ref[...] = rank
    miss = lax.dot_general(eaugt_ref[...], rank, (((1,), (1,)), ((), ())), preferred_element_type=F32)
    gather = jnp.where(miss == 0.0, 1.0, 0.0).astype(BF16)
    xs_ref[0] = _dot(gather, xh).astype(BF16)


def _merge_weights(w, tm, cap):
    n_r = N_EXPERT_GROUPS + N_EXPERTS
    wr = jnp.zeros((D_MODEL, LANES), F32).at[:, :n_r].set(
        jnp.concatenate([w['w_router_group'], w['w_router_expert']], axis=1).astype(F32))
    br = jnp.zeros((1, LANES), F32).at[0, :n_r].set(
        jnp.concatenate([w['b_router_group'], w['b_router_expert']]).astype(F32))
    tri =(jnp.arange(tm)[:, None] > jnp.arange(tm)[None, :]).astype(BF16)
    _, expand_aug = _slot_tables(cap)
    return [w['w_up_attn'].astype(BF16), w['w_glu'].astype(BF16), w['b_glu'].astype(F32)[None, :],
            w['w_up_ssm'].astype(BF16), w['w_out'].astype(BF16), w['ln1_g'].astype(F32)[None, :],
            w['ln1_b'].astype(F32)[None, :], wr.astype(BF16), br, tri, expand_aug.T]


def _merge(x, ot, yg, ga, gs, weights, tm, cap):
    n = x.shape[0]
    n_slot = N_EXPERTS * cap
    row = lambda i: (i, 0)
    const = lambda i: (0, 0)
    full = lambda a: pl.BlockSpec(a.shape, const)
    return pl.pallas_call(
        functools.partial(_merge_body, cap=cap),
        grid=(n // tm,),
        in_specs=[pl.BlockSpec((tm, D_MODEL), row),
                  pl.BlockSpec((N_Q_HEADS, HEAD_DIM, tm), lambda i: (0, 0, i)),
                  pl.BlockSpec((N_SSM_GROUPS, tm // SSM_CHUNK, CHUNK_W), lambda i: (0, i, 0)),
                  pl.BlockSpec((tm, D_MODEL), row), pl.BlockSpec((tm, D_MODEL), row)] + [full(a) for a in weights],
        out_specs=[pl.BlockSpec((tm, D_MODEL), row), pl.BlockSpec((tm, LANES), row), pl.BlockSpec((tm, LANES), row),
                   pl.BlockSpec((1, n_slot, D_MODEL), lambda i: (i, 0, 0))],
        out_shape=[jax.ShapeDtypeStruct((n, D_MODEL), F32), jax.ShapeDtypeStruct((n, LANES), F32),
                   jax.ShapeDtypeStruct((n, LANES), BF16), jax.ShapeDtypeStruct((n // tm, n_slot, D_MODEL), BF16)],
        scratch_shapes=[pltpu.VMEM((D_SSM // LANES, tm, LANES), F32)],
        compiler_params=_cparams(("parallel",)),
        name="merge_ln1_router",
    )(x, ot, yg, ga, gs, *weights)


def _expert_mlp(xb, wg, wu, wd):
    h = jax.nn.silu(_dot(xb, wg)) * _dot(xb, wu)
    return _dot(h.astype(BF16), wd)


def _experts_body(xs_ref, wg_ref, wu_ref, wd_ref, ys_ref):
    shp = xs_ref.shape
    xb = xs_ref[...].reshape(shp[0] * shp[2], D_MODEL)
    ys_ref[...] = _expert_mlp(xb, wg_ref[0], wu_ref[0], wd_ref[0]).astype(BF16).reshape(shp)


def _experts(xs, w_gate, w_up, w_down, cap, tiles_per_step):
    n_tiles = xs.shape[0]
    xs4 = xs.reshape(n_tiles, N_EXPERTS, cap, D_MODEL)
    slots = pl.BlockSpec((tiles_per_step, 1, cap, D_MODEL), lambda e, t: (t, e, 0, 0))
    exp3 = lambda e, t: (e, 0, 0)
    ys = pl.pallas_call(
        _experts_body,
        grid=(N_EXPERTS, n_tiles // tiles_per_step),
        in_specs=[slots, pl.BlockSpec((1, D_MODEL, D_FF_EXPERT), exp3), pl.BlockSpec((1, D_MODEL, D_FF_EXPERT), exp3),
                  pl.BlockSpec((1, D_FF_EXPERT, D_MODEL), exp3)],
        out_specs=slots,
        out_shape=jax.ShapeDtypeStruct(xs4.shape, BF16),
        compiler_params=_cparams(("parallel", "parallel")),
        name="experts",
    )(xs4, w_gate, w_up, w_down)
    return ys.reshape(xs.shape)


def _combine_body(x1_ref, route_ref, rank_ref, ys_ref, eaug_ref, e_ref, g2_ref, b2_ref, wg_hbm, wu_hbm, wd_hbm,
                  o_ref, acc_sc, wg_sc, wu_sc, wd_sc, *, cap):
    route = route_ref[...]
    rank = rank_ref[...]
    hit = _dot(rank, eaug_ref[...]) == 0.0
    spread = _dot(route.astype(BF16), e_ref[...])
    acc_sc[...] = _dot(jnp.where(hit, spread, 0.0).astype(BF16), ys_ref[0])
    lane = lax.broadcasted_iota(jnp.int32, route.shape, 1)
    spilled = jnp.where((rank.astype(F32) == float(cap)) & (lane != AUX_LANE), route, 0.0)

    @pl.when(jnp.max(jnp.abs(spilled)) > 0.0)
    def _():
        xb = x1_ref[...].astype(BF16)

        def one_expert(e, carry):
            pltpu.sync_copy(wg_hbm.at[e], wg_sc)
            pltpu.sync_copy(wu_hbm.at[e], wu_sc)
            pltpu.sync_copy(wd_hbm.at[e], wd_sc)
            w_e = jnp.sum(jnp.where(lane == e + ROUTE_E0, spilled, 0.0), axis=-1, keepdims=True)
            acc_sc[...] += w_e * _expert_mlp(xb, wg_sc[...], wu_sc[...], wd_sc[...])
            return carry

        lax.fori_loop(0, N_EXPERTS, one_expert, 0)

    o_ref[...] = _layer_norm(ALPHA * x1_ref[...] + acc_sc[...], g2_ref[...], b2_ref[...])


def _combine(x1, route, rank, ys, w_gate, w_up, w_down, ln2_g, ln2_b, tm, cap):
    n = x1.shape[0]
    n_slot = N_EXPERTS * cap
    expand, expand_aug = _slot_tables(cap)
    row = lambda i: (i, 0)
    const = lambda i: (0, 0)
    hbm = pl.BlockSpec(memory_space=pl.ANY)
    return pl.pallas_call(
        functools.partial(_combine_body, cap=cap),
        grid=(n // tm,),
        in_specs=[pl.BlockSpec((tm, D_MODEL), row), pl.BlockSpec((tm, LANES), row), pl.BlockSpec((tm, LANES), row),
                  pl.BlockSpec((1, n_slot, D_MODEL), lambda i: (i, 0, 0)),
                  pl.BlockSpec((LANES, n_slot), const), pl.BlockSpec((LANES, n_slot), const),
                  pl.BlockSpec((1, D_MODEL), const), pl.BlockSpec((1, D_MODEL), const), hbm, hbm, hbm],
        out_specs=pl.BlockSpec((tm, D_MODEL), row),
        out_shape=jax.ShapeDtypeStruct((n, D_MODEL), F32),
        scratch_shapes=[pltpu.VMEM((tm, D_MODEL), F32), pltpu.VMEM((D_MODEL, D_FF_EXPERT), BF16),
                        pltpu.VMEM((D_MODEL, D_FF_EXPERT), BF16), pltpu.VMEM((D_FF_EXPERT, D_MODEL), BF16)],
        compiler_params=_cparams(("parallel",)),
        name="combine_ln2",
    )(x1, route, rank, ys, expand_aug, expand, ln2_g.astype(F32)[None, :], ln2_b.astype(F32)[None, :],
      w_gate, w_up, w_down)


def _encode(x, w, prep, cap):
    b, t, _ = x.shape
    n = b * t
    tm, tq, tk = math.gcd(512, t), math.gcd(256, t), math.gcd(512, t // 2)
    x2 = x.reshape(n, D_MODEL).astype(F32)
    qt, kh, vt, ug, ga, gs = _in_projection(x2, t, prep['w_in'], w['q_norm_g'], w['k_norm_g'], tm, tk)
    ot = _attention(qt, kh, vt, w['q_norm_g'], w['k_norm_g'], b, t, tq, tk)
    yg = _s5_scan(ug, prep['s5'], b, t // SSM_CHUNK)
    x1, route, rank, xs = _merge(x2, ot, yg, ga, gs, _merge_weights(w, tm, cap), tm, cap)
    wg, wu, wd = prep['experts']
    ys = _experts(xs, wg, wu, wd, cap, math.gcd(8, n // tm))
    out = _combine(x1, route, rank, ys, wg, wu, wd, w['ln2_g'], w['ln2_b'], tm, cap)
    return out.reshape(b, t, D_MODEL)


def kernel(x_prompt, x_sample, w_in, q_norm_g, k_norm_g, ssm_a_re, ssm_a_im, ssm_log_dt, ssm_b_re, ssm_b_im, ssm_c_re, ssm_c_im, ssm_d, w_glu, b_glu, w_up_attn, w_up_ssm, w_out, ln1_g, ln1_b, w_router_group, b_router_group, w_router_expert, b_router_expert, w_exp_gate, w_exp_up, w_exp_down, ln2_g, ln2_b):
    w = dict(w_in=w_in, q_norm_g=q_norm_g, k_norm_g=k_norm_g, ssm_a_re=ssm_a_re, ssm_a_im=ssm_a_im,
             ssm_log_dt=ssm_log_dt, ssm_b_re=ssm_b_re, ssm_b_im=ssm_b_im, ssm_c_re=ssm_c_re, ssm_c_im=ssm_c_im,
             ssm_d=ssm_d, w_glu=w_glu, b_glu=b_glu, w_up_attn=w_up_attn, w_up_ssm=w_up_ssm, w_out=w_out,
             ln1_g=ln1_g, ln1_b=ln1_b, w_router_group=w_router_group, b_router_group=b_router_group,
             w_router_expert=w_router_expert, b_router_expert=b_router_expert, w_exp_gate=w_exp_gate,
             w_exp_up=w_exp_up, w_exp_down=w_exp_down, ln2_g=ln2_g, ln2_b=ln2_b)
    w = {name: val[0] for name, val in w.items()}
    prep = dict(
        w_in=w['w_in'].astype(BF16),
        s5=_s5_matrices(w['ssm_a_re'], w['ssm_a_im'], w['ssm_log_dt'], w['ssm_b_re'], w['ssm_b_im'],
                        w['ssm_c_re'], w['ssm_c_im'], w['ssm_d']),
        experts=tuple(w[name].astype(BF16) for name in ('w_exp_gate', 'w_exp_up', 'w_exp_down')))
    return _encode(x_prompt, w, prep, EXPERT_CAP), _encode(x_sample, w, prep, EXPERT_CAP)
```

```python
import functools
import math

import jax
import jax.numpy as jnp
from jax import lax
from jax.experimental import pallas as pl
from jax.experimental.pallas import tpu as pltpu

F32 = jnp.float32
BF16 = jnp.bfloat16

D_MODEL = 1024
GRID_W = 64
HEAD_DIM = 64
N_Q_HEADS = 8
N_KV_HEADS = 2
Q_PER_KV = N_Q_HEADS // N_KV_HEADS
D_ATTN = N_Q_HEADS * HEAD_DIM
D_KV = N_KV_HEADS * HEAD_DIM
ROPE_THETA = 10000.0
D_SSM = D_MODEL // 2
SSM_GROUP = 16
N_SSM_GROUPS = D_SSM // SSM_GROUP
SSM_STATE = 64
N_EXPERT_GROUPS = 4
EXPERTS_PER_GROUP = 4
N_EXPERTS = N_EXPERT_GROUPS * EXPERTS_PER_GROUP
D_FF_EXPERT = D_MODEL // 2
DEPTH = 1
ALPHA = (2.0 * DEPTH) ** 0.25
EPS = 1e-6
D_IN_PROJ = D_ATTN + 2 * D_KV + D_SSM + 2 * D_MODEL
OFF_K = D_ATTN
OFF_V = OFF_K + D_KV
OFF_U = OFF_V + D_KV
OFF_GA = OFF_U + D_SSM
OFF_GS = OFF_GA + D_MODEL

LANES = 128
MXU_DIM = 256
SSM_CHUNK = 16
CHUNK_W = SSM_CHUNK * SSM_GROUP
ROUTE_E0 = N_EXPERT_GROUPS
VMEM_LIMIT = 56 * 1024 * 1024
EXPERT_CAP = 96


def _cparams(sem):
    return pltpu.CompilerParams(dimension_semantics=sem, vmem_limit_bytes=VMEM_LIMIT)


def _dot(a, b):
    return jnp.dot(a, b, preferred_element_type=F32)


def _split_bf16(x):
    hi = x.astype(BF16)
    lo = (x - hi.astype(F32)).astype(BF16)
    return hi, lo


def _layer_norm(h, g, b):
    mu = jnp.mean(h, axis=-1, keepdims=True)
    var = jnp.mean(jnp.square(h - mu), axis=-1, keepdims=True)
    return (h - mu) * lax.rsqrt(var + EPS) * g + b


QK_SCALE = HEAD_DIM ** -0.5 * math.log2(math.e)
MAX_UNSHIFTED_EXPONENT = 40.0
LANE_BLOCKS = LANES // SSM_GROUP


def _block_transpose(blocks):
    lane = lax.broadcasted_iota(jnp.int32, blocks[0].shape, 1)
    v = list(blocks)
    for s in range(3):
        d = SSM_GROUP << s
        keep = ((lane // d) % 2) == 0
        nxt = list(v)
        for a in range(LANE_BLOCKS):
            if not a & (1 << s):
                a2 = a | (1 << s)
                nxt[a] = jnp.where(keep, v[a], pltpu.roll(v[a2], d, 1))
                nxt[a2] = jnp.where(keep, pltpu.roll(v[a], LANES - d, 1), v[a2])
        v = nxt
    return v


def _inproj_body(x_ref, w_ref, gq_ref, gk_ref, cos_ref, sin_ref, seg_ref,
                 qt_ref, kh_ref, vt_ref, ug_ref, ga_ref, gs_ref, u_sc, *, tk):
    xb = x_ref[...].astype(BF16)
    tm = xb.shape[0]
    cos = cos_ref[...]
    sin = sin_ref[...]
    seg = seg_ref[...]
    lane = lax.broadcasted_iota(jnp.int32, (1, LANES), 1)
    first_half = (lane % HEAD_DIM) < (HEAD_DIM // 2)

    def mm(a, b):
        return _dot(xb, w_ref[:, a:b])

    def head_sumsq(y):
        w = y.shape[1]
        return _dot((y * y).astype(BF16), seg[0:w, 0:w])

    def norm_rope(y, ss, g):
        yn = y * lax.rsqrt(ss * (1.0 / HEAD_DIM) + EPS) * g
        sw = jnp.where(first_half, pltpu.roll(yn, LANES - HEAD_DIM // 2, 1), pltpu.roll(yn, HEAD_DIM // 2, 1))
        return yn * cos + sw * sin

    gq = gq_ref[...]
    heads_per_slab = LANES // HEAD_DIM
    wide = seg.shape[0]
    for j in range(D_ATTN // wide):
        y = mm(j * wide, (j + 1) * wide)
        ss = head_sumsq(y)
        for jj in range(wide // LANES):
            q = norm_rope(y[:, jj * LANES:(jj + 1) * LANES], ss[:, jj * LANES:(jj + 1) * LANES], gq)
            qt = (q * QK_SCALE).T
            for h in range(heads_per_slab):
                head = (j * (wide // LANES) + jj) * heads_per_slab + h
                qt_ref[head] = qt[h * HEAD_DIM:(h + 1) * HEAD_DIM].astype(BF16)
    yk = mm(OFF_K, OFF_V)
    k = norm_rope(yk, head_sumsq(yk), gk_ref[...])
    vt = mm(OFF_V, OFF_U).astype(BF16).astype(F32).T
    for g in range(N_KV_HEADS):
        kh_ref[g] = k[:, g * HEAD_DIM:(g + 1) * HEAD_DIM].astype(BF16)
        for jb in range(tm // tk):
            vt_ref[g, jb] = vt[g * HEAD_DIM:(g + 1) * HEAD_DIM, jb * tk:(jb + 1) * tk].astype(BF16)
    u = mm(OFF_U, OFF_GA)
    n_tiles = D_SSM // LANES
    for gh in range(n_tiles):
        u_sc[gh] = u[:, gh * LANES:(gh + 1) * LANES]
    n_chunks = tm // SSM_CHUNK
    for gh in range(n_tiles):
        for ih in range(SSM_CHUNK // LANE_BLOCKS):
            rows = [u_sc[gh, pl.ds(LANE_BLOCKS * ih + a, n_chunks, stride=SSM_CHUNK), :] for a in range(LANE_BLOCKS)]
            for gl, blk in enumerate(_block_transpose(rows)):
                ug_ref[LANE_BLOCKS * gh + gl, :, ih * LANES:(ih + 1) * LANES] = blk
    ga_ref[...] = jax.nn.sigmoid(mm(OFF_GA, OFF_GS)).astype(BF16)
    gs_ref[...] = jax.nn.sigmoid(mm(OFF_GS, D_IN_PROJ)).astype(BF16)


def _rope_tables(t_max):
    t = jnp.arange(t_max, dtype=jnp.int32)
    row_pos = (t // GRID_W).astype(F32)
    col_pos = (t % GRID_W).astype(F32)
    n_freq = HEAD_DIM // 4
    inv_freq = ROPE_THETA ** (-jnp.arange(n_freq, dtype=F32) / n_freq)
    ang = jnp.concatenate([row_pos[:, None] * inv_freq, col_pos[:, None] * inv_freq], axis=-1)
    c, s = jnp.cos(ang), jnp.sin(ang)
    reps = LANES // HEAD_DIM
    cos_t = jnp.tile(jnp.concatenate([c, c], axis=-1), (1, reps))
    sin_t = jnp.tile(jnp.concatenate([-s, s], axis=-1), (1, reps))
    return cos_t, sin_t


def _in_projection(x, t, w_in_bf16, q_norm_g, k_norm_g, tm, tk):
    n = x.shape[0]
    tiles_per_seq = t // tm
    cos_t, sin_t = _rope_tables(t)
    reps = LANES // HEAD_DIM
    gq = jnp.tile(q_norm_g.astype(F32), reps)[None, :]
    gk = jnp.tile(k_norm_g.astype(F32), reps)[None, :]
    head_of_lane = jnp.arange(MXU_DIM) // HEAD_DIM
    seg = (head_of_lane[:, None] == head_of_lane[None, :]).astype(BF16)
    pos_map = lambda i: (i % tiles_per_seq, 0)
    row = lambda i: (i, 0)
    const = lambda i: (0, 0)
    return pl.pallas_call(
        functools.partial(_inproj_body, tk=tk),
        grid=(n // tm,),
        in_specs=[
            pl.BlockSpec((tm, D_MODEL), row),
            pl.BlockSpec((D_MODEL, D_IN_PROJ), const),
            pl.BlockSpec((1, LANES), const),
            pl.BlockSpec((1, LANES), const),
            pl.BlockSpec((tm, LANES), pos_map),
            pl.BlockSpec((tm, LANES), pos_map),
            pl.BlockSpec((MXU_DIM, MXU_DIM), const),
        ],
        out_specs=[
            pl.BlockSpec((N_Q_HEADS, HEAD_DIM, tm), lambda i: (0, 0, i)),
            pl.BlockSpec((N_KV_HEADS, tm, HEAD_DIM), lambda i: (0, i, 0)),
            pl.BlockSpec((N_KV_HEADS, tm // tk, HEAD_DIM, tk), lambda i: (0, i, 0, 0)),
            pl.BlockSpec((N_SSM_GROUPS, tm // SSM_CHUNK, CHUNK_W), lambda i: (0, i, 0)),
            pl.BlockSpec((tm, D_MODEL), row),
            pl.BlockSpec((tm, D_MODEL), row),
        ],
        out_shape=[
            jax.ShapeDtypeStruct((N_Q_HEADS, HEAD_DIM, n), BF16),
            jax.ShapeDtypeStruct((N_KV_HEADS, n, HEAD_DIM), BF16),
            jax.ShapeDtypeStruct((N_KV_HEADS, n // tk, HEAD_DIM, tk), BF16),
            jax.ShapeDtypeStruct((N_SSM_GROUPS, n // SSM_CHUNK, CHUNK_W), F32),
            jax.ShapeDtypeStruct((n, D_MODEL), BF16),
            jax.ShapeDtypeStruct((n, D_MODEL), BF16),
        ],
        scratch_shapes=[pltpu.VMEM((D_SSM // LANES, tm, LANES), F32)],
        compiler_params=_cparams(("parallel",)),
        name="in_projection",
    )(x, w_in_bf16, gq, gk, cos_t, sin_t, seg)


def _attn_body(qt_ref, k_ref, vt_ref, o_ref, m_sc, l_sc, acc_sc, s_sc, *, tq, tk, n_qt, n_kv, shift):
    n_blocks = n_qt * n_kv

    def scores(f):
        qo = pl.multiple_of((f // n_kv) * tq, tq)
        ko = pl.multiple_of((f % n_kv) * tk, tk)
        q4t = jnp.concatenate([qt_ref[h, :, pl.ds(qo, tq)] for h in range(Q_PER_KV)], axis=1)
        return _dot(k_ref[0, pl.ds(ko, tk), :], q4t)

    def row_groups(p):
        return jnp.sum(p.reshape(tk // 8, 8, p.shape[1]), axis=0)

    def softmax_pv(s, j):
        if shift:
            m_old = m_sc[...]
            m_new = jnp.maximum(m_old, jnp.max(s, axis=0, keepdims=True))
            alpha = jnp.exp2(m_old - m_new)
            p = jnp.exp2(s - m_new)
            l_sc[...] = alpha * l_sc[...] + row_groups(p)
            acc_sc[...] = alpha * acc_sc[...] + _dot(vt_ref[0, j], p.astype(BF16))
            m_sc[...] = m_new
        else:
            p = jnp.exp2(s)
            l_sc[...] += row_groups(p)
            acc_sc[...] += _dot(vt_ref[0, j], p.astype(BF16))

    s_sc[0] = scores(0)

    def query_tile(qi, carry):
        m_sc[...] = jnp.full(m_sc.shape, -jnp.inf, F32)
        l_sc[...] = jnp.zeros(l_sc.shape, F32)
        acc_sc[...] = jnp.zeros(acc_sc.shape, F32)

        def body(jj, c):
            f = qi * n_kv + 2 * jj
            s_sc[1] = scores(f + 1)
            softmax_pv(s_sc[0], 2 * jj)
            s_sc[0] = scores(jnp.minimum(f + 2, n_blocks - 1))
            softmax_pv(s_sc[1], 2 * jj + 1)
            return c

        lax.fori_loop(0, n_kv // 2, body, 0)
        out = acc_sc[...] / jnp.sum(l_sc[...], axis=0, keepdims=True)
        qo = pl.multiple_of(qi * tq, tq)
        for h in range(Q_PER_KV):
            o_ref[h, :, pl.ds(qo, tq)] = out[:, h * tq:(h + 1) * tq].astype(o_ref.dtype)
        return carry

    lax.fori_loop(0, n_qt, query_tile, 0)


def _attention(qt, k_heads, vt_blocks, q_norm_g, k_norm_g, b, t, tq, tk):
    n_qt = t // tq
    n_kv = t // tk
    bound = QK_SCALE * HEAD_DIM * jnp.max(jnp.abs(q_norm_g.astype(F32))) * jnp.max(jnp.abs(k_norm_g.astype(F32)))
    call = functools.partial(_attention_call, b=b, t=t, tq=tq, tk=tk, n_qt=n_qt, n_kv=n_kv)
    return lax.cond(bound < MAX_UNSHIFTED_EXPONENT,
                    functools.partial(call, shift=False), functools.partial(call, shift=True),
                    qt, k_heads, vt_blocks)


def _attention_call(qt, k_heads, vt_blocks, *, b, t, tq, tk, n_qt, n_kv, shift):
    body = functools.partial(_attn_body, tq=tq, tk=tk, n_qt=n_qt, n_kv=n_kv, shift=shift)
    return pl.pallas_call(
        body,
        grid=(b, N_KV_HEADS),
        in_specs=[
            pl.BlockSpec((Q_PER_KV, HEAD_DIM, t), lambda bi, g: (g, 0, bi)),
            pl.BlockSpec((1, t, HEAD_DIM), lambda bi, g: (g, bi, 0)),
            pl.BlockSpec((1, n_kv, HEAD_DIM, tk), lambda bi, g: (g, bi, 0, 0)),
        ],
        out_specs=pl.BlockSpec((Q_PER_KV, HEAD_DIM, t), lambda bi, g: (g, 0, bi)),
        out_shape=jax.ShapeDtypeStruct((N_Q_HEADS, HEAD_DIM, b * t), BF16),
        scratch_shapes=[
            pltpu.VMEM((1, Q_PER_KV * tq), F32),
            pltpu.VMEM((8, Q_PER_KV * tq), F32),
            pltpu.VMEM((HEAD_DIM, Q_PER_KV * tq), F32),
            pltpu.VMEM((2, tk, Q_PER_KV * tq), F32),
        ],
        compiler_params=_cparams(("parallel", "parallel")),
        name="attention",
    )(qt, k_heads, vt_blocks)


def _s5_matrices(a_re, a_im, log_dt, b_re, b_im, c_re, c_im, d_skip):
    hp = lax.Precision.HIGHEST
    L, G, P, H = SSM_CHUNK, N_SSM_GROUPS, SSM_STATE, SSM_GROUP
    f = lambda v: v.astype(F32)
    lr, li = f(a_re), f(a_im)
    dt = jnp.exp(f(log_dt))[..., None]
    mag = jnp.exp(lr * dt)
    ar = mag * jnp.cos(li * dt)
    ai = mag * jnp.sin(li * dt)
    den = lr * lr + li * li
    kr = ((ar - 1.0) * lr + ai * li) / den
    ki = (ai * lr - (ar - 1.0) * li) / den
    br, bi = f(b_re), f(b_im)
    bbr = kr[..., None] * br - ki[..., None] * bi
    bbi = kr[..., None] * bi + ki[..., None] * br
    kk = jnp.arange(L + 1, dtype=F32)[:, None, None, None]
    pmag = jnp.exp(kk * (lr * dt)[None])
    ph = kk * (li * dt)[None]
    pr = pmag * jnp.cos(ph)
    pi = pmag * jnp.sin(ph)
    cr, ci = f(c_re), f(c_im)
    car = cr[None] * pr[:, :, :, None, :] - ci[None] * pi[:, :, :, None, :]
    cai = cr[None] * pi[:, :, :, None, :] + ci[None] * pr[:, :, :, None, :]
    kern = (jnp.einsum('kdgop,dgpi->kdgoi', car, bbr, precision=hp)
            - jnp.einsum('kdgop,dgpi->kdgoi', cai, bbi, precision=hp))
    jj = jnp.arange(L)[:, None]
    ii = jnp.arange(L)[None, :]

    def toeplitz(kd, lag, valid):
        blocks = jnp.where(valid[:, :, None, None, None], kd[jnp.clip(lag, 0, L)], 0.0)
        return blocks.transpose(2, 0, 4, 1, 3).reshape(G, CHUNK_W, CHUNK_W)

    msum = toeplitz(kern[:, 0], ii - jj, ii >= jj) + toeplitz(kern[:, 1], jj - ii, jj >= ii)

    def in_op(d, powers):
        prj, pij = pr[powers, d], pi[powers, d]
        re = prj[..., None] * bbr[d][None] - pij[..., None] * bbi[d][None]
        im = prj[..., None] * bbi[d][None] + pij[..., None] * bbr[d][None]
        to = lambda m: m.transpose(1, 0, 3, 2).reshape(G, CHUNK_W, P)
        return to(re), to(im)

    def out_op(d, powers):
        to = lambda m: m.transpose(1, 3, 0, 2).reshape(G, P, CHUNK_W)
        return to(car[powers, d]), to(-cai[powers, d])

    steps = jnp.arange(L)
    bm = jnp.stack(in_op(0, L - 1 - steps) + in_op(1, steps), axis=1)
    cm = jnp.stack(out_op(0, steps + 1) + out_op(1, L - steps), axis=1)
    al = jnp.stack([pr[L, 0], pi[L, 0], pr[L, 1], pi[L, 1]], axis=1)[:, :, None, :]
    dvec = jnp.tile(f(d_skip).reshape(G, 1, H), (1, 1, L))
    z_in = jnp.zeros((G // 2, 4, CHUNK_W, P), F32)
    z_out = jnp.zeros((G // 2, 4, P, CHUNK_W), F32)
    bm_pair = jnp.concatenate([jnp.concatenate([bm[0::2], z_in], axis=3),
                               jnp.concatenate([z_in, bm[1::2]], axis=3)], axis=2)
    cm_pair = jnp.concatenate([jnp.concatenate([cm[0::2], z_out], axis=3),
                               jnp.concatenate([z_out, cm[1::2]], axis=3)], axis=2)
    al_pair = jnp.concatenate([al[0::2], al[1::2]], axis=3)
    bm_all = jnp.concatenate([bm_pair[:, d] for d in range(4)], axis=2)
    cm_all = jnp.concatenate([cm_pair[:, d] for d in range(4)], axis=1)
    return msum.astype(BF16), bm_all.astype(BF16), cm_all.astype(BF16), al_pair, dvec


def _s5_body(u_ref, m_ref, bm_ref, cm_ref, al_ref, d_ref, y_ref, st_sc, *, nb, nc):
    u0, u1 = u_ref[0], u_ref[1]
    ub = jnp.concatenate([u0.astype(BF16), u1.astype(BF16)], axis=1)
    local = _dot(ub, bm_ref[0])
    for d in range(4):
        st_sc[d] = local[:, d * LANES:(d + 1) * LANES]
    al = [al_ref[0, d] for d in range(4)]

    def advance(rows, d, sr, si):
        lr = st_sc[d, rows, :]
        li = st_sc[d + 1, rows, :]
        st_sc[d, rows, :] = sr
        st_sc[d + 1, rows, :] = si
        ar, ai = al[d], al[d + 1]
        return ar * sr - ai * si + lr, ar * si + ai * sr + li

    def step(k, st):
        fr, fi, br, bi = st
        fr, fi = advance(pl.ds(k, nb, stride=nc), 0, fr, fi)
        br, bi = advance(pl.ds(nc - 1 - k, nb, stride=nc), 2, br, bi)
        return fr, fi, br, bi

    lax.fori_loop(0, nc, step, tuple(jnp.zeros((nb, LANES), F32) for _ in range(4)))

    entering = jnp.concatenate([st_sc[d].astype(BF16) for d in range(4)], axis=1)
    y = _dot(entering, cm_ref[0])
    y_ref[0] = y[:, :CHUNK_W] + _dot(ub[:, :CHUNK_W], m_ref[0]) + u0 * d_ref[0]
    y_ref[1] = y[:, CHUNK_W:] + _dot(ub[:, CHUNK_W:], m_ref[1]) + u1 * d_ref[1]


def _s5_scan(ug, mats, nb, nc):
    msum, bm, cm, al, dvec = mats
    rows = nb * nc
    pair3 = lambda g: (g, 0, 0)
    pair4 = lambda g: (g, 0, 0, 0)
    return pl.pallas_call(
        functools.partial(_s5_body, nb=nb, nc=nc),
        grid=(N_SSM_GROUPS // 2,),
        in_specs=[
            pl.BlockSpec((2, rows, CHUNK_W), pair3),
            pl.BlockSpec((2, CHUNK_W, CHUNK_W), pair3),
            pl.BlockSpec((1, 2 * CHUNK_W, 4 * LANES), pair3),
            pl.BlockSpec((1, 4 * LANES, 2 * CHUNK_W), pair3),
            pl.BlockSpec((1, 4, 1, LANES), pair4),
            pl.BlockSpec((2, 1, CHUNK_W), pair3),
        ],
        out_specs=pl.BlockSpec((2, rows, CHUNK_W), pair3),
        out_shape=jax.ShapeDtypeStruct((N_SSM_GROUPS, rows, CHUNK_W), F32),
        scratch_shapes=[pltpu.VMEM((4, rows, LANES), F32)],
        compiler_params=_cparams(("parallel",)),
        name="s5_scan",
    )(ug, msum, bm, cm, al, dvec)


def _route(logits):
    lane = lax.broadcasted_iota(jnp.int32, logits.shape, 1)
    neg = -jnp.inf
    big = jnp.int32(LANES)
    is_g = lane < N_EXPERT_GROUPS
    gl = jnp.where(is_g, logits, neg)
    gmax = jnp.max(gl, axis=-1, keepdims=True)
    gidx = jnp.min(jnp.where(gl == gmax, lane, big), axis=-1, keepdims=True)
    g_w = 1.0 / jnp.sum(jnp.where(is_g, jnp.exp(logits - gmax), 0.0), axis=-1, keepdims=True)
    e_lane = lane - ROUTE_E0
    in_group = (e_lane >= 0) & (e_lane < N_EXPERTS) & ((e_lane // EXPERTS_PER_GROUP) == gidx)
    el = jnp.where(in_group, logits, neg)
    v1 = jnp.max(el, axis=-1, keepdims=True)
    i1 = jnp.min(jnp.where(el == v1, lane, big), axis=-1, keepdims=True)
    el2 = jnp.where(lane == i1, neg, el)
    v2 = jnp.max(el2, axis=-1, keepdims=True)
    i2 = jnp.min(jnp.where(el2 == v2, lane, big), axis=-1, keepdims=True)
    e2 = jnp.exp(v2 - v1)
    inv = g_w / (1.0 + e2)
    combine = jnp.where(lane == i1, inv, 0.0) + jnp.where(lane == i2, e2 * inv, 0.0)
    return combine, (lane == i1) | (lane == i2)


AUX_LANE = LANES - 1


def _slot_tables(cap):
    slot = jnp.arange(N_EXPERTS * cap)
    lane = jnp.arange(LANES)[:, None]
    expand = (lane == ROUTE_E0 + slot // cap).astype(F32)
    expand_aug = jnp.where(lane == AUX_LANE, -(slot % cap).astype(F32), expand)
    return expand.astype(BF16), expand_aug.astype(BF16)


def _merge_body(x_ref, ot_ref, yg_ref, ga_ref, gs_ref, wua_ref, wglu_ref, bglu_ref, wus_ref, wout_ref,
                g1_ref, b1_ref, wr_ref, br_ref, tri_ref, eaugt_ref,
                x1_ref, route_ref, rank_ref, xs_ref, y_sc, *, cap):
    tm = x_ref.shape[0]
    n_chunks = tm // SSM_CHUNK
    for gh in range(D_SSM // LANES):
        for ih in range(SSM_CHUNK // LANE_BLOCKS):
            blocks = [yg_ref[LANE_BLOCKS * gh + gl, :, ih * LANES:(ih + 1) * LANES] for gl in range(LANE_BLOCKS)]
            for a, blk in enumerate(_block_transpose(blocks)):
                y_sc[gh, pl.ds(LANE_BLOCKS * ih + a, n_chunks, stride=SSM_CHUNK), :] = blk
    y = jnp.concatenate([y_sc[gh] for gh in range(D_SSM // LANES)], axis=1)
    o = ot_ref[...].reshape(D_ATTN, tm).astype(F32).T.astype(BF16)
    z = jax.nn.gelu(y)
    z = z * jax.nn.sigmoid(_dot(z.astype(BF16), wglu_ref[...]) + bglu_ref[...])
    ssm_up = _dot(z.astype(BF16), wus_ref[...])
    att_up = _dot(o, wua_ref[...])
    merged = ga_ref[...].astype(F32) * att_up + gs_ref[...].astype(F32) * ssm_up
    mix = _dot(merged.astype(BF16), wout_ref[...])
    x1 = _layer_norm(ALPHA * x_ref[...] + mix, g1_ref[...], b1_ref[...])
    x1_ref[...] = x1
    xh = x1.astype(BF16)
    logits = _dot(xh, wr_ref[...]) + br_ref[...]
    combine, sel = _route(logits)
    route_ref[...] = combine
    rank = _dot(tri_ref[...], jnp.where(sel, 1.0, 0.0).astype(BF16))
    lane = lax.broadcasted_iota(jnp.int32, rank.shape, 1)
    rank = jnp.where(sel, jnp.minimum(rank, float(cap)), float(cap))
    rank = jnp.where(lane == AUX_LANE, 1.0, rank).astype(BF16)
    rank_ref[...] = rank
    miss = lax.dot_general(eaugt_ref[...], rank, (((1,), (1,)), ((), ())), preferred_element_type=F32)
    gather = jnp.where(miss == 0.0, 1.0, 0.0).astype(BF16)
    xs_ref[0] = _dot(gather, xh).astype(BF16)


def _merge_weights(w, tm, cap):
    n_r = N_EXPERT_GROUPS + N_EXPERTS
    wr = jnp.zeros((D_MODEL, LANES), F32).at[:, :n_r].set(
        jnp.concatenate([w['w_router_group'], w['w_router_expert']], axis=1).astype(F32))
    br = jnp.zeros((1, LANES), F32).at[0, :n_r].set(
        jnp.concatenate([w['b_router_group'], w['b_router_expert']]).astype(F32))
    tri =(jnp.arange(tm)[:, None] > jnp.arange(tm)[None, :]).astype(BF16)
    _, expand_aug = _slot_tables(cap)
    return [w['w_up_attn'].astype(BF16), w['w_glu'].astype(BF16), w['b_glu'].astype(F32)[None, :],
            w['w_up_ssm'].astype(BF16), w['w_out'].astype(BF16), w['ln1_g'].astype(F32)[None, :],
            w['ln1_b'].astype(F32)[None, :], wr.astype(BF16), br, tri, expand_aug.T]


def _merge(x, ot, yg, ga, gs, weights, tm, cap):
    n = x.shape[0]
    n_slot = N_EXPERTS * cap
    row = lambda i: (i, 0)
    const = lambda i: (0, 0)
    full = lambda a: pl.BlockSpec(a.shape, const)
    return pl.pallas_call(
        functools.partial(_merge_body, cap=cap),
        grid=(n // tm,),
        in_specs=[pl.BlockSpec((tm, D_MODEL), row),
                  pl.BlockSpec((N_Q_HEADS, HEAD_DIM, tm), lambda i: (0, 0, i)),
                  pl.BlockSpec((N_SSM_GROUPS, tm // SSM_CHUNK, CHUNK_W), lambda i: (0, i, 0)),
                  pl.BlockSpec((tm, D_MODEL), row), pl.BlockSpec((tm, D_MODEL), row)] + [full(a) for a in weights],
        out_specs=[pl.BlockSpec((tm, D_MODEL), row), pl.BlockSpec((tm, LANES), row), pl.BlockSpec((tm, LANES), row),
                   pl.BlockSpec((1, n_slot, D_MODEL), lambda i: (i, 0, 0))],
        out_shape=[jax.ShapeDtypeStruct((n, D_MODEL), F32), jax.ShapeDtypeStruct((n, LANES), F32),
                   jax.ShapeDtypeStruct((n, LANES), BF16), jax.ShapeDtypeStruct((n // tm, n_slot, D_MODEL), BF16)],
        scratch_shapes=[pltpu.VMEM((D_SSM // LANES, tm, LANES), F32)],
        compiler_params=_cparams(("parallel",)),
        name="merge_ln1_router",
    )(x, ot, yg, ga, gs, *weights)


def _expert_mlp(xb, wg, wu, wd):
    h = jax.nn.silu(_dot(xb, wg)) * _dot(xb, wu)
    return _dot(h.astype(BF16), wd)


def _experts_body(xs_ref, wg_ref, wu_ref, wd_ref, ys_ref):
    shp = xs_ref.shape
    xb = xs_ref[...].reshape(shp[0] * shp[2], D_MODEL)
    ys_ref[...] = _expert_mlp(xb, wg_ref[0], wu_ref[0], wd_ref[0]).astype(BF16).reshape(shp)


def _experts(xs, w_gate, w_up, w_down, cap, tiles_per_step):
    n_tiles = xs.shape[0]
    xs4 = xs.reshape(n_tiles, N_EXPERTS, cap, D_MODEL)
    slots = pl.BlockSpec((tiles_per_step, 1, cap, D_MODEL), lambda e, t: (t, e, 0, 0))
    exp3 = lambda e, t: (e, 0, 0)
    ys = pl.pallas_call(
        _experts_body,
        grid=(N_EXPERTS, n_tiles // tiles_per_step),
        in_specs=[slots, pl.BlockSpec((1, D_MODEL, D_FF_EXPERT), exp3), pl.BlockSpec((1, D_MODEL, D_FF_EXPERT), exp3),
                  pl.BlockSpec((1, D_FF_EXPERT, D_MODEL), exp3)],
        out_specs=slots,
        out_shape=jax.ShapeDtypeStruct(xs4.shape, BF16),
        compiler_params=_cparams(("parallel", "parallel")),
        name="experts",
    )(xs4, w_gate, w_up, w_down)
    return ys.reshape(xs.shape)


def _combine_body(x1_ref, route_ref, rank_ref, ys_ref, eaug_ref, e_ref, g2_ref, b2_ref, wg_hbm, wu_hbm, wd_hbm,
                  o_ref, acc_sc, wg_sc, wu_sc, wd_sc, *, cap):
    route = route_ref[...]
    rank = rank_ref[...]
    hit = _dot(rank, eaug_ref[...]) == 0.0
    spread = _dot(route.astype(BF16), e_ref[...])
    acc_sc[...] = _dot(jnp.where(hit, spread, 0.0).astype(BF16), ys_ref[0])
    lane = lax.broadcasted_iota(jnp.int32, route.shape, 1)
    spilled = jnp.where((rank.astype(F32) == float(cap)) & (lane != AUX_LANE), route, 0.0)

    @pl.when(jnp.max(jnp.abs(spilled)) > 0.0)
    def _():
        xb = x1_ref[...].astype(BF16)

        def one_expert(e, carry):
            pltpu.sync_copy(wg_hbm.at[e], wg_sc)
            pltpu.sync_copy(wu_hbm.at[e], wu_sc)
            pltpu.sync_copy(wd_hbm.at[e], wd_sc)
            w_e = jnp.sum(jnp.where(lane == e + ROUTE_E0, spilled, 0.0), axis=-1, keepdims=True)
            acc_sc[...] += w_e * _expert_mlp(xb, wg_sc[...], wu_sc[...], wd_sc[...])
            return carry

        lax.fori_loop(0, N_EXPERTS, one_expert, 0)

    o_ref[...] = _layer_norm(ALPHA * x1_ref[...] + acc_sc[...], g2_ref[...], b2_ref[...])


def _combine(x1, route, rank, ys, w_gate, w_up, w_down, ln2_g, ln2_b, tm, cap):
    n = x1.shape[0]
    n_slot = N_EXPERTS * cap
    expand, expand_aug = _slot_tables(cap)
    row = lambda i: (i, 0)
    const = lambda i: (0, 0)
    hbm = pl.BlockSpec(memory_space=pl.ANY)
    return pl.pallas_call(
        functools.partial(_combine_body, cap=cap),
        grid=(n // tm,),
        in_specs=[pl.BlockSpec((tm, D_MODEL), row), pl.BlockSpec((tm, LANES), row), pl.BlockSpec((tm, LANES), row),
                  pl.BlockSpec((1, n_slot, D_MODEL), lambda i: (i, 0, 0)),
                  pl.BlockSpec((LANES, n_slot), const), pl.BlockSpec((LANES, n_slot), const),
                  pl.BlockSpec((1, D_MODEL), const), pl.BlockSpec((1, D_MODEL), const), hbm, hbm, hbm],
        out_specs=pl.BlockSpec((tm, D_MODEL), row),
        out_shape=jax.ShapeDtypeStruct((n, D_MODEL), F32),
        scratch_shapes=[pltpu.VMEM((tm, D_MODEL), F32), pltpu.VMEM((D_MODEL, D_FF_EXPERT), BF16),
                        pltpu.VMEM((D_MODEL, D_FF_EXPERT), BF16), pltpu.VMEM((D_FF_EXPERT, D_MODEL), BF16)],
        compiler_params=_cparams(("parallel",)),
        name="combine_ln2",
    )(x1, route, rank, ys, expand_aug, expand, ln2_g.astype(F32)[None, :], ln2_b.astype(F32)[None, :],
      w_gate, w_up, w_down)


def _encode(x, w, prep, cap):
    b, t, _ = x.shape
    n = b * t
    tm, tq, tk = math.gcd(512, t), math.gcd(256, t), math.gcd(512, t // 2)
    x2 = x.reshape(n, D_MODEL).astype(F32)
    qt, kh, vt, ug, ga, gs = _in_projection(x2, t, prep['w_in'], w['q_norm_g'], w['k_norm_g'], tm, tk)
    ot = _attention(qt, kh, vt, w['q_norm_g'], w['k_norm_g'], b, t, tq, tk)
    yg = _s5_scan(ug, prep['s5'], b, t // SSM_CHUNK)
    x1, route, rank, xs = _merge(x2, ot, yg, ga, gs, _merge_weights(w, tm, cap), tm, cap)
    wg, wu, wd = prep['experts']
    ys = _experts(xs, wg, wu, wd, cap, math.gcd(8, n // tm))
    out = _combine(x1, route, rank, ys, wg, wu, wd, w['ln2_g'], w['ln2_b'], tm, cap)
    return out.reshape(b, t, D_MODEL)


def kernel(x_prompt, x_sample, w_in, q_norm_g, k_norm_g, ssm_a_re, ssm_a_im, ssm_log_dt, ssm_b_re, ssm_b_im, ssm_c_re, ssm_c_im, ssm_d, w_glu, b_glu, w_up_attn, w_up_ssm, w_out, ln1_g, ln1_b, w_router_group, b_router_group, w_router_expert, b_router_expert, w_exp_gate, w_exp_up, w_exp_down, ln2_g, ln2_b):
    w = dict(w_in=w_in, q_norm_g=q_norm_g, k_norm_g=k_norm_g, ssm_a_re=ssm_a_re, ssm_a_im=ssm_a_im,
             ssm_log_dt=ssm_log_dt, ssm_b_re=ssm_b_re, ssm_b_im=ssm_b_im, ssm_c_re=ssm_c_re, ssm_c_im=ssm_c_im,
             ssm_d=ssm_d, w_glu=w_glu, b_glu=b_glu, w_up_attn=w_up_attn, w_up_ssm=w_up_ssm, w_out=w_out,
             ln1_g=ln1_g, ln1_b=ln1_b, w_router_group=w_router_group, b_router_group=b_router_group,
             w_router_expert=w_router_expert, b_router_expert=b_router_expert, w_exp_gate=w_exp_gate,
             w_exp_up=w_exp_up, w_exp_down=w_exp_down, ln2_g=ln2_g, ln2_b=ln2_b)
    w = {name: val[0] for name, val in w.items()}
    prep = dict(
        w_in=w['w_in'].astype(BF16),
        s5=_s5_matrices(w['ssm_a_re'], w['ssm_a_im'], w['ssm_log_dt'], w['ssm_b_re'], w['ssm_b_im'],
                        w['ssm_c_re'], w['ssm_c_im'], w['ssm_d']),
        experts=tuple(w[name].astype(BF16) for name in ('w_exp_gate', 'w_exp_up', 'w_exp_down')))
    return _encode(x_prompt, w, prep, EXPERT_CAP), _encode(x_sample, w, prep, EXPERT_CAP)
```

```python
import functools
import math

import jax
import jax.numpy as jnp
from jax import lax
from jax.experimental import pallas as pl
from jax.experimental.pallas import tpu as pltpu

F32 = jnp.float32
BF16 = jnp.bfloat16

D_MODEL = 1024
GRID_W = 64
HEAD_DIM = 64
N_Q_HEADS = 8
N_KV_HEADS = 2
Q_PER_KV = N_Q_HEADS // N_KV_HEADS
D_ATTN = N_Q_HEADS * HEAD_DIM
D_KV = N_KV_HEADS * HEAD_DIM
ROPE_THETA = 10000.0
D_SSM = D_MODEL // 2
SSM_GROUP = 16
N_SSM_GROUPS = D_SSM // SSM_GROUP
SSM_STATE = 64
N_EXPERT_GROUPS = 4
EXPERTS_PER_GROUP = 4
N_EXPERTS = N_EXPERT_GROUPS * EXPERTS_PER_GROUP
D_FF_EXPERT = D_MODEL // 2
DEPTH = 1
ALPHA = (2.0 * DEPTH) ** 0.25
EPS = 1e-6
D_IN_PROJ = D_ATTN + 2 * D_KV + D_SSM + 2 * D_MODEL
OFF_K = D_ATTN
OFF_V = OFF_K + D_KV
OFF_U = OFF_V + D_KV
OFF_GA = OFF_U + D_SSM
OFF_GS = OFF_GA + D_MODEL

LANES = 128
MXU_DIM = 256
SSM_CHUNK = 16
CHUNK_W = SSM_CHUNK * SSM_GROUP
ROUTE_E0 = N_EXPERT_GROUPS
VMEM_LIMIT = 56 * 1024 * 1024
EXPERT_CAP = 96


def _cparams(sem):
    return pltpu.CompilerParams(dimension_semantics=sem, vmem_limit_bytes=VMEM_LIMIT)


def _dot(a, b):
    return jnp.dot(a, b, preferred_element_type=F32)


def _split_bf16(x):
    hi = x.astype(BF16)
    lo = (x - hi.astype(F32)).astype(BF16)
    return hi, lo


def _layer_norm(h, g, b):
    mu = jnp.mean(h, axis=-1, keepdims=True)
    var = jnp.mean(jnp.square(h - mu), axis=-1, keepdims=True)
    return (h - mu) * lax.rsqrt(var + EPS) * g + b


QK_SCALE = HEAD_DIM ** -0.5 * math.log2(math.e)
KV_BLOCKS_PER_TRIP = 8
MAX_UNSHIFTED_EXPONENT = 40.0
LANE_BLOCKS = LANES // SSM_GROUP


def _block_transpose(blocks):
    lane = lax.broadcasted_iota(jnp.int32, blocks[0].shape, 1)
    v = list(blocks)
    for s in range(3):
        d = SSM_GROUP << s
        keep = ((lane // d) % 2) == 0
        nxt = list(v)
        for a in range(LANE_BLOCKS):
            if not a & (1 << s):
                a2 = a | (1 << s)
                nxt[a] = jnp.where(keep, v[a], pltpu.roll(v[a2], d, 1))
                nxt[a2] = jnp.where(keep, pltpu.roll(v[a], LANES - d, 1), v[a2])
        v = nxt
    return v


def _inproj_body(x_ref, w_ref, gq_ref, gk_ref, cos_ref, sin_ref, seg_ref,
                 qt_ref, kh_ref, vt_ref, ug_ref, ga_ref, gs_ref, u_sc, *, tk):
    xb = x_ref[...].astype(BF16)
    tm = xb.shape[0]
    cos = cos_ref[...]
    sin = sin_ref[...]
    seg = seg_ref[...]
    lane = lax.broadcasted_iota(jnp.int32, (1, LANES), 1)
    first_half = (lane % HEAD_DIM) < (HEAD_DIM // 2)

    def mm(a, b):
        return _dot(xb, w_ref[:, a:b])

    def head_sumsq(y):
        w = y.shape[1]
        return _dot((y * y).astype(BF16), seg[0:w, 0:w])

    def norm_rope(y, ss, g):
        yn = y * lax.rsqrt(ss * (1.0 / HEAD_DIM) + EPS) * g
        sw = jnp.where(first_half, pltpu.roll(yn, LANES - HEAD_DIM // 2, 1), pltpu.roll(yn, HEAD_DIM // 2, 1))
        return yn * cos + sw * sin

    gq = gq_ref[...]
    heads_per_slab = LANES // HEAD_DIM
    wide = seg.shape[0]
    for j in range(D_ATTN // wide):
        y = mm(j * wide, (j + 1) * wide)
        ss = head_sumsq(y)
        for jj in range(wide // LANES):
            q = norm_rope(y[:, jj * LANES:(jj + 1) * LANES], ss[:, jj * LANES:(jj + 1) * LANES], gq)
            qt = (q * QK_SCALE).T
            for h in range(heads_per_slab):
                head = (j * (wide // LANES) + jj) * heads_per_slab + h
                qt_ref[head] = qt[h * HEAD_DIM:(h + 1) * HEAD_DIM].astype(BF16)
    yk = mm(OFF_K, OFF_V)
    k = norm_rope(yk, head_sumsq(yk), gk_ref[...])
    vt = mm(OFF_V, OFF_U).astype(BF16).astype(F32).T
    for g in range(N_KV_HEADS):
        kh_ref[g] = k[:, g * HEAD_DIM:(g + 1) * HEAD_DIM].astype(BF16)
        for jb in range(tm // tk):
            vt_ref[g, jb] = vt[g * HEAD_DIM:(g + 1) * HEAD_DIM, jb * tk:(jb + 1) * tk].astype(BF16)
    u = mm(OFF_U, OFF_GA)
    n_tiles = D_SSM // LANES
    for gh in range(n_tiles):
        u_sc[gh] = u[:, gh * LANES:(gh + 1) * LANES]
    n_chunks = tm // SSM_CHUNK
    for gh in range(n_tiles):
        for ih in range(SSM_CHUNK // LANE_BLOCKS):
            rows = [u_sc[gh, pl.ds(LANE_BLOCKS * ih + a, n_chunks, stride=SSM_CHUNK), :] for a in range(LANE_BLOCKS)]
            for gl, blk in enumerate(_block_transpose(rows)):
                ug_ref[LANE_BLOCKS * gh + gl, :, ih * LANES:(ih + 1) * LANES] = blk
    ga_ref[...] = jax.nn.sigmoid(mm(OFF_GA, OFF_GS)).astype(BF16)
    gs_ref[...] = jax.nn.sigmoid(mm(OFF_GS, D_IN_PROJ)).astype(BF16)


def _rope_tables(t_max):
    t = jnp.arange(t_max, dtype=jnp.int32)
    row_pos = (t // GRID_W).astype(F32)
    col_pos = (t % GRID_W).astype(F32)
    n_freq = HEAD_DIM // 4
    inv_freq = ROPE_THETA ** (-jnp.arange(n_freq, dtype=F32) / n_freq)
    ang = jnp.concatenate([row_pos[:, None] * inv_freq, col_pos[:, None] * inv_freq], axis=-1)
    c, s = jnp.cos(ang), jnp.sin(ang)
    reps = LANES // HEAD_DIM
    cos_t = jnp.tile(jnp.concatenate([c, c], axis=-1), (1, reps))
    sin_t = jnp.tile(jnp.concatenate([-s, s], axis=-1), (1, reps))
    return cos_t, sin_t


def _in_projection(x, t, w_in_bf16, q_norm_g, k_norm_g, tm, tk):
    n = x.shape[0]
    tiles_per_seq = t // tm
    cos_t, sin_t = _rope_tables(t)
    reps = LANES // HEAD_DIM
    gq = jnp.tile(q_norm_g.astype(F32), reps)[None, :]
    gk = jnp.tile(k_norm_g.astype(F32), reps)[None, :]
    head_of_lane = jnp.arange(MXU_DIM) // HEAD_DIM
    seg = (head_of_lane[:, None] == head_of_lane[None, :]).astype(BF16)
    pos_map = lambda i: (i % tiles_per_seq, 0)
    row = lambda i: (i, 0)
    const = lambda i: (0, 0)
    return pl.pallas_call(
        functools.partial(_inproj_body, tk=tk),
        grid=(n // tm,),
        in_specs=[
            pl.BlockSpec((tm, D_MODEL), row),
            pl.BlockSpec((D_MODEL, D_IN_PROJ), const),
            pl.BlockSpec((1, LANES), const),
            pl.BlockSpec((1, LANES), const),
            pl.BlockSpec((tm, LANES), pos_map),
            pl.BlockSpec((tm, LANES), pos_map),
            pl.BlockSpec((MXU_DIM, MXU_DIM), const),
        ],
        out_specs=[
            pl.BlockSpec((N_Q_HEADS, HEAD_DIM, tm), lambda i: (0, 0, i)),
            pl.BlockSpec((N_KV_HEADS, tm, HEAD_DIM), lambda i: (0, i, 0)),
            pl.BlockSpec((N_KV_HEADS, tm // tk, HEAD_DIM, tk), lambda i: (0, i, 0, 0)),
            pl.BlockSpec((N_SSM_GROUPS, tm // SSM_CHUNK, CHUNK_W), lambda i: (0, i, 0)),
            pl.BlockSpec((tm, D_MODEL), row),
            pl.BlockSpec((tm, D_MODEL), row),
        ],
        out_shape=[
            jax.ShapeDtypeStruct((N_Q_HEADS, HEAD_DIM, n), BF16),
            jax.ShapeDtypeStruct((N_KV_HEADS, n, HEAD_DIM), BF16),
            jax.ShapeDtypeStruct((N_KV_HEADS, n // tk, HEAD_DIM, tk), BF16),
            jax.ShapeDtypeStruct((N_SSM_GROUPS, n // SSM_CHUNK, CHUNK_W), F32),
            jax.ShapeDtypeStruct((n, D_MODEL), BF16),
            jax.ShapeDtypeStruct((n, D_MODEL), BF16),
        ],
        scratch_shapes=[pltpu.VMEM((D_SSM // LANES, tm, LANES), F32)],
        compiler_params=_cparams(("parallel",)),
        name="in_projection",
    )(x, w_in_bf16, gq, gk, cos_t, sin_t, seg)


def _attn_body(qt_ref, k_ref, vt_ref, o_ref, m_sc, l_sc, acc_sc, s_sc, *, tq, tk, n_qt, n_kv, shift):
    n_blocks = n_qt * n_kv

    def scores(f):
        qo = pl.multiple_of((f // n_kv) * tq, tq)
        ko = pl.multiple_of((f % n_kv) * tk, tk)
        q4t = jnp.concatenate([qt_ref[h, :, pl.ds(qo, tq)] for h in range(Q_PER_KV)], axis=1)
        return _dot(k_ref[0, pl.ds(ko, tk), :], q4t)

    def row_groups(p):
        return jnp.sum(p.reshape(tk // 8, 8, p.shape[1]), axis=0)

    def softmax_pv(s, j):
        if shift:
            m_old = m_sc[...]
            m_new = jnp.maximum(m_old, jnp.max(s, axis=0, keepdims=True))
            alpha = jnp.exp2(m_old - m_new)
            p = jnp.exp2(s - m_new)
            l_sc[...] = alpha * l_sc[...] + row_groups(p)
            acc_sc[...] = alpha * acc_sc[...] + _dot(vt_ref[0, j], p.astype(BF16))
            m_sc[...] = m_new
        else:
            p = jnp.exp2(s)
            l_sc[...] += row_groups(p)
            acc_sc[...] += _dot(vt_ref[0, j], p.astype(BF16))

    s_sc[0] = scores(0)

    def query_tile(qi, carry):
        m_sc[...] = jnp.full(m_sc.shape, -jnp.inf, F32)
        l_sc[...] = jnp.zeros(l_sc.shape, F32)
        acc_sc[...] = jnp.zeros(acc_sc.shape, F32)

        per_trip = math.gcd(KV_BLOCKS_PER_TRIP, n_kv)

        def body(jj, c):
            for u in range(0, per_trip, 2):
                j = per_trip * jj + u
                f = qi * n_kv + j
                s_sc[1] = scores(f + 1)
                softmax_pv(s_sc[0], j)
                s_sc[0] = scores(jnp.minimum(f + 2, n_blocks - 1))
                softmax_pv(s_sc[1], j + 1)
            return c

        lax.fori_loop(0, n_kv // per_trip, body, 0)
        out = acc_sc[...] / jnp.sum(l_sc[...], axis=0, keepdims=True)
        qo = pl.multiple_of(qi * tq, tq)
        for h in range(Q_PER_KV):
            o_ref[h, :, pl.ds(qo, tq)] = out[:, h * tq:(h + 1) * tq].astype(o_ref.dtype)
        return carry

    lax.fori_loop(0, n_qt, query_tile, 0)


def _attention(qt, k_heads, vt_blocks, q_norm_g, k_norm_g, b, t, tq, tk):
    n_qt = t // tq
    n_kv = t // tk
    bound = QK_SCALE * HEAD_DIM * jnp.max(jnp.abs(q_norm_g.astype(F32))) * jnp.max(jnp.abs(k_norm_g.astype(F32)))
    call = functools.partial(_attention_call, b=b, t=t, tq=tq, tk=tk, n_qt=n_qt, n_kv=n_kv)
    return lax.cond(bound < MAX_UNSHIFTED_EXPONENT,
                    functools.partial(call, shift=False), functools.partial(call, shift=True),
                    qt, k_heads, vt_blocks)


def _attention_call(qt, k_heads, vt_blocks, *, b, t, tq, tk, n_qt, n_kv, shift):
    body = functools.partial(_attn_body, tq=tq, tk=tk, n_qt=n_qt, n_kv=n_kv, shift=shift)
    return pl.pallas_call(
        body,
        grid=(b, N_KV_HEADS),
        in_specs=[
            pl.BlockSpec((Q_PER_KV, HEAD_DIM, t), lambda bi, g: (g, 0, bi)),
            pl.BlockSpec((1, t, HEAD_DIM), lambda bi, g: (g, bi, 0)),
            pl.BlockSpec((1, n_kv, HEAD_DIM, tk), lambda bi, g: (g, bi, 0, 0)),
        ],
        out_specs=pl.BlockSpec((Q_PER_KV, HEAD_DIM, t), lambda bi, g: (g, 0, bi)),
        out_shape=jax.ShapeDtypeStruct((N_Q_HEADS, HEAD_DIM, b * t), BF16),
        scratch_shapes=[
            pltpu.VMEM((1, Q_PER_KV * tq), F32),
            pltpu.VMEM((8, Q_PER_KV * tq), F32),
            pltpu.VMEM((HEAD_DIM, Q_PER_KV * tq), F32),
            pltpu.VMEM((2, tk, Q_PER_KV * tq), F32),
        ],
        compiler_params=_cparams(("parallel", "parallel")),
        name="attention",
    )(qt, k_heads, vt_blocks)


def _s5_matrices(a_re, a_im, log_dt, b_re, b_im, c_re, c_im, d_skip):
    hp = lax.Precision.HIGHEST
    L, G, P, H = SSM_CHUNK, N_SSM_GROUPS, SSM_STATE, SSM_GROUP
    f = lambda v: v.astype(F32)
    lr, li = f(a_re), f(a_im)
    dt = jnp.exp(f(log_dt))[..., None]
    mag = jnp.exp(lr * dt)
    ar = mag * jnp.cos(li * dt)
    ai = mag * jnp.sin(li * dt)
    den = lr * lr + li * li
    kr = ((ar - 1.0) * lr + ai * li) / den
    ki = (ai * lr - (ar - 1.0) * li) / den
    br, bi = f(b_re), f(b_im)
    bbr = kr[..., None] * br - ki[..., None] * bi
    bbi = kr[..., None] * bi + ki[..., None] * br
    kk = jnp.arange(L + 1, dtype=F32)[:, None, None, None]
    pmag = jnp.exp(kk * (lr * dt)[None])
    ph = kk * (li * dt)[None]
    pr = pmag * jnp.cos(ph)
    pi = pmag * jnp.sin(ph)
    cr, ci = f(c_re), f(c_im)
    car = cr[None] * pr[:, :, :, None, :] - ci[None] * pi[:, :, :, None, :]
    cai = cr[None] * pi[:, :, :, None, :] + ci[None] * pr[:, :, :, None, :]
    kern = (jnp.einsum('kdgop,dgpi->kdgoi', car, bbr, precision=hp)
            - jnp.einsum('kdgop,dgpi->kdgoi', cai, bbi, precision=hp))
    jj = jnp.arange(L)[:, None]
    ii = jnp.arange(L)[None, :]

    def toeplitz(kd, lag, valid):
        blocks = jnp.where(valid[:, :, None, None, None], kd[jnp.clip(lag, 0, L)], 0.0)
        return blocks.transpose(2, 0, 4, 1, 3).reshape(G, CHUNK_W, CHUNK_W)

    msum = toeplitz(kern[:, 0], ii - jj, ii >= jj) + toeplitz(kern[:, 1], jj - ii, jj >= ii)

    def in_op(d, powers):
        prj, pij = pr[powers, d], pi[powers, d]
        re = prj[..., None] * bbr[d][None] - pij[..., None] * bbi[d][None]
        im = prj[..., None] * bbi[d][None] + pij[..., None] * bbr[d][None]
        to = lambda m: m.transpose(1, 0, 3, 2).reshape(G, CHUNK_W, P)
        return to(re), to(im)

    def out_op(d, powers):
        to = lambda m: m.transpose(1, 3, 0, 2).reshape(G, P, CHUNK_W)
        return to(car[powers, d]), to(-cai[powers, d])

    steps = jnp.arange(L)
    bm = jnp.stack(in_op(0, L - 1 - steps) + in_op(1, steps), axis=1)
    cm = jnp.stack(out_op(0, steps + 1) + out_op(1, L - steps), axis=1)
    al = jnp.stack([pr[L, 0], pi[L, 0], pr[L, 1], pi[L, 1]], axis=1)[:, :, None, :]
    dvec = jnp.tile(f(d_skip).reshape(G, 1, H), (1, 1, L))
    z_in = jnp.zeros((G // 2, 4, CHUNK_W, P), F32)
    z_out = jnp.zeros((G // 2, 4, P, CHUNK_W), F32)
    bm_pair = jnp.concatenate([jnp.concatenate([bm[0::2], z_in], axis=3),
                               jnp.concatenate([z_in, bm[1::2]], axis=3)], axis=2)
    cm_pair = jnp.concatenate([jnp.concatenate([cm[0::2], z_out], axis=3),
                               jnp.concatenate([z_out, cm[1::2]], axis=3)], axis=2)
    al_pair = jnp.concatenate([al[0::2], al[1::2]], axis=3)
    bm_all = jnp.concatenate([bm_pair[:, d] for d in range(4)], axis=2)
    cm_all = jnp.concatenate([cm_pair[:, d] for d in range(4)], axis=1)
    return msum.astype(BF16), bm_all.astype(BF16), cm_all.astype(BF16), al_pair, dvec


def _s5_body(u_ref, m_ref, bm_ref, cm_ref, al_ref, d_ref, y_ref, st_sc, *, nb, nc):
    u0, u1 = u_ref[0], u_ref[1]
    ub = jnp.concatenate([u0.astype(BF16), u1.astype(BF16)], axis=1)
    local = _dot(ub, bm_ref[0])
    for d in range(4):
        st_sc[d] = local[:, d * LANES:(d + 1) * LANES]
    al = [al_ref[0, d] for d in range(4)]

    def advance(rows, d, sr, si):
        lr = st_sc[d, rows, :]
        li = st_sc[d + 1, rows, :]
        st_sc[d, rows, :] = sr
        st_sc[d + 1, rows, :] = si
        ar, ai = al[d], al[d + 1]
        return ar * sr - ai * si + lr, ar * si + ai * sr + li

    def step(k, st):
        fr, fi, br, bi = st
        fr, fi = advance(pl.ds(k, nb, stride=nc), 0, fr, fi)
        br, bi = advance(pl.ds(nc - 1 - k, nb, stride=nc), 2, br, bi)
        return fr, fi, br, bi

    lax.fori_loop(0, nc, step, tuple(jnp.zeros((nb, LANES), F32) for _ in range(4)), unroll=8)

    entering = jnp.concatenate([st_sc[d].astype(BF16) for d in range(4)], axis=1)
    y = _dot(entering, cm_ref[0])
    y_ref[0] = y[:, :CHUNK_W] + _dot(ub[:, :CHUNK_W], m_ref[0]) + u0 * d_ref[0]
    y_ref[1] = y[:, CHUNK_W:] + _dot(ub[:, CHUNK_W:], m_ref[1]) + u1 * d_ref[1]


def _s5_scan(ug, mats, nb, nc):
    msum, bm, cm, al, dvec = mats
    rows = nb * nc
    pair3 = lambda g: (g, 0, 0)
    pair4 = lambda g: (g, 0, 0, 0)
    return pl.pallas_call(
        functools.partial(_s5_body, nb=nb, nc=nc),
        grid=(N_SSM_GROUPS // 2,),
        in_specs=[
            pl.BlockSpec((2, rows, CHUNK_W), pair3),
            pl.BlockSpec((2, CHUNK_W, CHUNK_W), pair3),
            pl.BlockSpec((1, 2 * CHUNK_W, 4 * LANES), pair3),
            pl.BlockSpec((1, 4 * LANES, 2 * CHUNK_W), pair3),
            pl.BlockSpec((1, 4, 1, LANES), pair4),
            pl.BlockSpec((2, 1, CHUNK_W), pair3),
        ],
        out_specs=pl.BlockSpec((2, rows, CHUNK_W), pair3),
        out_shape=jax.ShapeDtypeStruct((N_SSM_GROUPS, rows, CHUNK_W), F32),
        scratch_shapes=[pltpu.VMEM((4, rows, LANES), F32)],
        compiler_params=_cparams(("parallel",)),
        name="s5_scan",
    )(ug, msum, bm, cm, al, dvec)


def _route(logits):
    lane = lax.broadcasted_iota(jnp.int32, logits.shape, 1)
    neg = -jnp.inf
    big = jnp.int32(LANES)
    is_g = lane < N_EXPERT_GROUPS
    gl = jnp.where(is_g, logits, neg)
    gmax = jnp.max(gl, axis=-1, keepdims=True)
    gidx = jnp.min(jnp.where(gl == gmax, lane, big), axis=-1, keepdims=True)
    g_w = 1.0 / jnp.sum(jnp.where(is_g, jnp.exp(logits - gmax), 0.0), axis=-1, keepdims=True)
    e_lane = lane - ROUTE_E0
    in_group = (e_lane >= 0) & (e_lane < N_EXPERTS) & ((e_lane // EXPERTS_PER_GROUP) == gidx)
    el = jnp.where(in_group, logits, neg)
    v1 = jnp.max(el, axis=-1, keepdims=True)
    i1 = jnp.min(jnp.where(el == v1, lane, big), axis=-1, keepdims=True)
    el2 = jnp.where(lane == i1, neg, el)
    v2 = jnp.max(el2, axis=-1, keepdims=True)
    i2 = jnp.min(jnp.where(el2 == v2, lane, big), axis=-1, keepdims=True)
    e2 = jnp.exp(v2 - v1)
    inv = g_w / (1.0 + e2)
    combine = jnp.where(lane == i1, inv, 0.0) + jnp.where(lane == i2, e2 * inv, 0.0)
    return combine, (lane == i1) | (lane == i2)


AUX_LANE = LANES - 1


def _slot_tables(cap):
    slot = jnp.arange(N_EXPERTS * cap)
    lane = jnp.arange(LANES)[:, None]
    expand = (lane == ROUTE_E0 + slot // cap).astype(F32)
    expand_aug = jnp.where(lane == AUX_LANE, -(slot % cap).astype(F32), expand)
    return expand.astype(BF16), expand_aug.astype(BF16)


def _merge_body(x_ref, ot_ref, yg_ref, ga_ref, gs_ref, wua_ref, wglu_ref, bglu_ref, wus_ref, wout_ref,
                g1_ref, b1_ref, wr_ref, br_ref, tri_ref, eaugt_ref,
                x1_ref, route_ref, rank_ref, xs_ref, y_sc, *, cap):
    tm = x_ref.shape[0]
    n_chunks = tm // SSM_CHUNK
    for gh in range(D_SSM // LANES):
        for ih in range(SSM_CHUNK // LANE_BLOCKS):
            blocks = [yg_ref[LANE_BLOCKS * gh + gl, :, ih * LANES:(ih + 1) * LANES] for gl in range(LANE_BLOCKS)]
            for a, blk in enumerate(_block_transpose(blocks)):
                y_sc[gh, pl.ds(LANE_BLOCKS * ih + a, n_chunks, stride=SSM_CHUNK), :] = blk
    y = jnp.concatenate([y_sc[gh] for gh in range(D_SSM // LANES)], axis=1)
    o = ot_ref[...].reshape(D_ATTN, tm).astype(F32).T.astype(BF16)
    z = jax.nn.gelu(y)
    z = z * jax.nn.sigmoid(_dot(z.astype(BF16), wglu_ref[...]) + bglu_ref[...])
    ssm_up = _dot(z.astype(BF16), wus_ref[...])
    att_up = _dot(o, wua_ref[...])
    merged = ga_ref[...].astype(F32) * att_up + gs_ref[...].astype(F32) * ssm_up
    mix = _dot(merged.astype(BF16), wout_ref[...])
    x1 = _layer_norm(ALPHA * x_ref[...] + mix, g1_ref[...], b1_ref[...])
    x1_ref[...] = x1
    xh = x1.astype(BF16)
    logits = _dot(xh, wr_ref[...]) + br_ref[...]
    combine, sel = _route(logits)
    route_ref[...] = combine
    rank = _dot(tri_ref[...], jnp.where(sel, 1.0, 0.0).astype(BF16))
    lane = lax.broadcasted_iota(jnp.int32, rank.shape, 1)
    rank = jnp.where(sel, jnp.minimum(rank, float(cap)), float(cap))
    rank = jnp.where(lane == AUX_LANE, 1.0, rank).astype(BF16)
    rank_ref[...] = rank
    miss = lax.dot_general(eaugt_ref[...], rank, (((1,), (1,)), ((), ())), preferred_element_type=F32)
    gather = jnp.where(miss == 0.0, 1.0, 0.0).astype(BF16)
    xs_ref[0] = _dot(gather, xh).astype(BF16)


def _merge_weights(w, tm, cap):
    n_r = N_EXPERT_GROUPS + N_EXPERTS
    wr = jnp.zeros((D_MODEL, LANES), F32).at[:, :n_r].set(
        jnp.concatenate([w['w_router_group'], w['w_router_expert']], axis=1).astype(F32))
    br = jnp.zeros((1, LANES), F32).at[0, :n_r].set(
        jnp.concatenate([w['b_router_group'], w['b_router_expert']]).astype(F32))
    tri =(jnp.arange(tm)[:, None] > jnp.arange(tm)[None, :]).astype(BF16)
    _, expand_aug = _slot_tables(cap)
    return [w['w_up_attn'].astype(BF16), w['w_glu'].astype(BF16), w['b_glu'].astype(F32)[None, :],
            w['w_up_ssm'].astype(BF16), w['w_out'].astype(BF16), w['ln1_g'].astype(F32)[None, :],
            w['ln1_b'].astype(F32)[None, :], wr.astype(BF16), br, tri, expand_aug.T]


def _merge(x, ot, yg, ga, gs, weights, tm, cap):
    n = x.shape[0]
    n_slot = N_EXPERTS * cap
    row = lambda i: (i, 0)
    const = lambda i: (0, 0)
    full = lambda a: pl.BlockSpec(a.shape, const)
    return pl.pallas_call(
        functools.partial(_merge_body, cap=cap),
        grid=(n // tm,),
        in_specs=[pl.BlockSpec((tm, D_MODEL), row),
                  pl.BlockSpec((N_Q_HEADS, HEAD_DIM, tm), lambda i: (0, 0, i)),
                  pl.BlockSpec((N_SSM_GROUPS, tm // SSM_CHUNK, CHUNK_W), lambda i: (0, i, 0)),
                  pl.BlockSpec((tm, D_MODEL), row), pl.BlockSpec((tm, D_MODEL), row)] + [full(a) for a in weights],
        out_specs=[pl.BlockSpec((tm, D_MODEL), row), pl.BlockSpec((tm, LANES), row), pl.BlockSpec((tm, LANES), row),
                   pl.BlockSpec((1, n_slot, D_MODEL), lambda i: (i, 0, 0))],
        out_shape=[jax.ShapeDtypeStruct((n, D_MODEL), F32), jax.ShapeDtypeStruct((n, LANES), F32),
                   jax.ShapeDtypeStruct((n, LANES), BF16), jax.ShapeDtypeStruct((n // tm, n_slot, D_MODEL), BF16)],
        scratch_shapes=[pltpu.VMEM((D_SSM // LANES, tm, LANES), F32)],
        compiler_params=_cparams(("parallel",)),
        name="merge_ln1_router",
    )(x, ot, yg, ga, gs, *weights)


def _expert_mlp(xb, wg, wu, wd):
    h = jax.nn.silu(_dot(xb, wg)) * _dot(xb, wu)
    return _dot(h.astype(BF16), wd)


def _experts_body(xs_ref, wg_ref, wu_ref, wd_ref, ys_ref):
    shp = xs_ref.shape
    xb = xs_ref[...].reshape(shp[0] * shp[2], D_MODEL)
    ys_ref[...] = _expert_mlp(xb, wg_ref[0], wu_ref[0], wd_ref[0]).astype(BF16).reshape(shp)


def _experts(xs, w_gate, w_up, w_down, cap, tiles_per_step):
    n_tiles = xs.shape[0]
    xs4 = xs.reshape(n_tiles, N_EXPERTS, cap, D_MODEL)
    slots = pl.BlockSpec((tiles_per_step, 1, cap, D_MODEL), lambda e, t: (t, e, 0, 0))
    exp3 = lambda e, t: (e, 0, 0)
    ys = pl.pallas_call(
        _experts_body,
        grid=(N_EXPERTS, n_tiles // tiles_per_step),
        in_specs=[slots, pl.BlockSpec((1, D_MODEL, D_FF_EXPERT), exp3), pl.BlockSpec((1, D_MODEL, D_FF_EXPERT), exp3),
                  pl.BlockSpec((1, D_FF_EXPERT, D_MODEL), exp3)],
        out_specs=slots,
        out_shape=jax.ShapeDtypeStruct(xs4.shape, BF16),
        compiler_params=_cparams(("parallel", "parallel")),
        name="experts",
    )(xs4, w_gate, w_up, w_down)
    return ys.reshape(xs.shape)


def _combine_body(x1_ref, route_ref, rank_ref, ys_ref, eaug_ref, e_ref, g2_ref, b2_ref, wg_hbm, wu_hbm, wd_hbm,
                  o_ref, acc_sc, wg_sc, wu_sc, wd_sc, *, cap):
    route = route_ref[...]
    rank = rank_ref[...]
    hit = _dot(rank, eaug_ref[...]) == 0.0
    spread = _dot(route.astype(BF16), e_ref[...])
    acc_sc[...] = _dot(jnp.where(hit, spread, 0.0).astype(BF16), ys_ref[0])
    lane = lax.broadcasted_iota(jnp.int32, route.shape, 1)
    spilled = jnp.where((rank.astype(F32) == float(cap)) & (lane != AUX_LANE), route, 0.0)

    @pl.when(jnp.max(jnp.abs(spilled)) > 0.0)
    def _():
        xb = x1_ref[...].astype(BF16)

        def one_expert(e, carry):
            pltpu.sync_copy(wg_hbm.at[e], wg_sc)
            pltpu.sync_copy(wu_hbm.at[e], wu_sc)
            pltpu.sync_copy(wd_hbm.at[e], wd_sc)
            w_e = jnp.sum(jnp.where(lane == e + ROUTE_E0, spilled, 0.0), axis=-1, keepdims=True)
            acc_sc[...] += w_e * _expert_mlp(xb, wg_sc[...], wu_sc[...], wd_sc[...])
            return carry

        lax.fori_loop(0, N_EXPERTS, one_expert, 0)

    o_ref[...] = _layer_norm(ALPHA * x1_ref[...] + acc_sc[...], g2_ref[...], b2_ref[...])


def _combine(x1, route, rank, ys, w_gate, w_up, w_down, ln2_g, ln2_b, tm, cap):
    n = x1.shape[0]
    n_slot = N_EXPERTS * cap
    expand, expand_aug = _slot_tables(cap)
    row = lambda i: (i, 0)
    const = lambda i: (0, 0)
    hbm = pl.BlockSpec(memory_space=pl.ANY)
    return pl.pallas_call(
        functools.partial(_combine_body, cap=cap),
        grid=(n // tm,),
        in_specs=[pl.BlockSpec((tm, D_MODEL), row), pl.BlockSpec((tm, LANES), row), pl.BlockSpec((tm, LANES), row),
                  pl.BlockSpec((1, n_slot, D_MODEL), lambda i: (i, 0, 0)),
                  pl.BlockSpec((LANES, n_slot), const), pl.BlockSpec((LANES, n_slot), const),
                  pl.BlockSpec((1, D_MODEL), const), pl.BlockSpec((1, D_MODEL), const), hbm, hbm, hbm],
        out_specs=pl.BlockSpec((tm, D_MODEL), row),
        out_shape=jax.ShapeDtypeStruct((n, D_MODEL), F32),
        scratch_shapes=[pltpu.VMEM((tm, D_MODEL), F32), pltpu.VMEM((D_MODEL, D_FF_EXPERT), BF16),
                        pltpu.VMEM((D_MODEL, D_FF_EXPERT), BF16), pltpu.VMEM((D_FF_EXPERT, D_MODEL), BF16)],
        compiler_params=_cparams(("parallel",)),
        name="combine_ln2",
    )(x1, route, rank, ys, expand_aug, expand, ln2_g.astype(F32)[None, :], ln2_b.astype(F32)[None, :],
      w_gate, w_up, w_down)


def _encode(x, w, prep, cap):
    b, t, _ = x.shape
    n = b * t
    tm, tq, tk = math.gcd(512, t), math.gcd(256, t), math.gcd(512, t // 2)
    x2 = x.reshape(n, D_MODEL).astype(F32)
    qt, kh, vt, ug, ga, gs = _in_projection(x2, t, prep['w_in'], w['q_norm_g'], w['k_norm_g'], tm, tk)
    ot = _attention(qt, kh, vt, w['q_norm_g'], w['k_norm_g'], b, t, tq, tk)
    yg = _s5_scan(ug, prep['s5'], b, t // SSM_CHUNK)
    x1, route, rank, xs = _merge(x2, ot, yg, ga, gs, _merge_weights(w, tm, cap), tm, cap)
    wg, wu, wd = prep['experts']
    ys = _experts(xs, wg, wu, wd, cap, math.gcd(8, n // tm))
    out = _combine(x1, route, rank, ys, wg, wu, wd, w['ln2_g'], w['ln2_b'], tm, cap)
    return out.reshape(b, t, D_MODEL)


def kernel(x_prompt, x_sample, w_in, q_norm_g, k_norm_g, ssm_a_re, ssm_a_im, ssm_log_dt, ssm_b_re, ssm_b_im, ssm_c_re, ssm_c_im, ssm_d, w_glu, b_glu, w_up_attn, w_up_ssm, w_out, ln1_g, ln1_b, w_router_group, b_router_group, w_router_expert, b_router_expert, w_exp_gate, w_exp_up, w_exp_down, ln2_g, ln2_b):
    w = dict(w_in=w_in, q_norm_g=q_norm_g, k_norm_g=k_norm_g, ssm_a_re=ssm_a_re, ssm_a_im=ssm_a_im,
             ssm_log_dt=ssm_log_dt, ssm_b_re=ssm_b_re, ssm_b_im=ssm_b_im, ssm_c_re=ssm_c_re, ssm_c_im=ssm_c_im,
             ssm_d=ssm_d, w_glu=w_glu, b_glu=b_glu, w_up_attn=w_up_attn, w_up_ssm=w_up_ssm, w_out=w_out,
             ln1_g=ln1_g, ln1_b=ln1_b, w_router_group=w_router_group, b_router_group=b_router_group,
             w_router_expert=w_router_expert, b_router_expert=b_router_expert, w_exp_gate=w_exp_gate,
             w_exp_up=w_exp_up, w_exp_down=w_exp_down, ln2_g=ln2_g, ln2_b=ln2_b)
    w = {name: val[0] for name, val in w.items()}
    prep = dict(
        w_in=w['w_in'].astype(BF16),
        s5=_s5_matrices(w['ssm_a_re'], w['ssm_a_im'], w['ssm_log_dt'], w['ssm_b_re'], w['ssm_b_im'],
                        w['ssm_c_re'], w['ssm_c_im'], w['ssm_d']),
        experts=tuple(w[name].astype(BF16) for name in ('w_exp_gate', 'w_exp_up', 'w_exp_down')))
    return _encode(x_prompt, w, prep, EXPERT_CAP), _encode(x_sample, w, prep, EXPERT_CAP)
```

```python
import functools
import math

import jax
import jax.numpy as jnp
from jax import lax
from jax.experimental import pallas as pl
from jax.experimental.pallas import tpu as pltpu

F32 = jnp.float32
BF16 = jnp.bfloat16

D_MODEL = 1024
GRID_W = 64
HEAD_DIM = 64
N_Q_HEADS = 8
N_KV_HEADS = 2
Q_PER_KV = N_Q_HEADS // N_KV_HEADS
D_ATTN = N_Q_HEADS * HEAD_DIM
D_KV = N_KV_HEADS * HEAD_DIM
ROPE_THETA = 10000.0
D_SSM = D_MODEL // 2
SSM_GROUP = 16
N_SSM_GROUPS = D_SSM // SSM_GROUP
SSM_STATE = 64
N_EXPERT_GROUPS = 4
EXPERTS_PER_GROUP = 4
N_EXPERTS = N_EXPERT_GROUPS * EXPERTS_PER_GROUP
D_FF_EXPERT = D_MODEL // 2
DEPTH = 1
ALPHA = (2.0 * DEPTH) ** 0.25
EPS = 1e-6
D_IN_PROJ = D_ATTN + 2 * D_KV + D_SSM + 2 * D_MODEL
OFF_K = D_ATTN
OFF_V = OFF_K + D_KV
OFF_U = OFF_V + D_KV
OFF_GA = OFF_U + D_SSM
OFF_GS = OFF_GA + D_MODEL

LANES = 128
MXU_DIM = 256
SSM_CHUNK = 16
CHUNK_W = SSM_CHUNK * SSM_GROUP
ROUTE_E0 = N_EXPERT_GROUPS
VMEM_LIMIT = 56 * 1024 * 1024
EXPERT_CAP = 96


def _cparams(sem):
    return pltpu.CompilerParams(dimension_semantics=sem, vmem_limit_bytes=VMEM_LIMIT)


def _dot(a, b):
    return jnp.dot(a, b, preferred_element_type=F32)


def _split_bf16(x):
    hi = x.astype(BF16)
    lo = (x - hi.astype(F32)).astype(BF16)
    return hi, lo


def _sigmoid(x):
    return 0.5 * jnp.tanh(0.5 * x) + 0.5


def _layer_norm(h, g, b):
    mu = jnp.mean(h, axis=-1, keepdims=True)
    var = jnp.mean(jnp.square(h - mu), axis=-1, keepdims=True)
    return (h - mu) * lax.rsqrt(var + EPS) * g + b


QK_SCALE = HEAD_DIM ** -0.5 * math.log2(math.e)
MERGE_ROW_BLOCKS = 2
KV_BLOCKS_PER_TRIP = 8
MAX_UNSHIFTED_EXPONENT = 40.0
LANE_BLOCKS = LANES // SSM_GROUP


def _block_transpose(blocks):
    lane = lax.broadcasted_iota(jnp.int32, blocks[0].shape, 1)
    v = list(blocks)
    for s in range(3):
        d = SSM_GROUP << s
        keep = ((lane // d) % 2) == 0
        nxt = list(v)
        for a in range(LANE_BLOCKS):
            if not a & (1 << s):
                a2 = a | (1 << s)
                nxt[a] = jnp.where(keep, v[a], pltpu.roll(v[a2], d, 1))
                nxt[a2] = jnp.where(keep, pltpu.roll(v[a], LANES - d, 1), v[a2])
        v = nxt
    return v


def _inproj_body(x_ref, w_ref, gq_ref, gk_ref, cos_ref, sin_ref, seg_ref,
                 qt_ref, kh_ref, vt_ref, ug_ref, ga_ref, gs_ref, u_sc, *, tk):
    xb = x_ref[...].astype(BF16)
    tm = xb.shape[0]
    cos = cos_ref[...]
    sin = sin_ref[...]
    seg = seg_ref[...]
    lane = lax.broadcasted_iota(jnp.int32, (1, LANES), 1)
    first_half = (lane % HEAD_DIM) < (HEAD_DIM // 2)

    def mm(a, b):
        return _dot(xb, w_ref[:, a:b])

    def head_sumsq(y):
        w = y.shape[1]
        return _dot((y * y).astype(BF16), seg[0:w, 0:w])

    def norm_rope(y, ss, g):
        yn = y * lax.rsqrt(ss * (1.0 / HEAD_DIM) + EPS) * g
        sw = jnp.where(first_half, pltpu.roll(yn, LANES - HEAD_DIM // 2, 1), pltpu.roll(yn, HEAD_DIM // 2, 1))
        return yn * cos + sw * sin

    gq = gq_ref[...]
    heads_per_slab = LANES // HEAD_DIM
    wide = seg.shape[0]

    def q_slab(j):
        y = mm(j * wide, (j + 1) * wide)
        ss = head_sumsq(y)
        for jj in range(wide // LANES):
            q = norm_rope(y[:, jj * LANES:(jj + 1) * LANES], ss[:, jj * LANES:(jj + 1) * LANES], gq)
            qt = (q * QK_SCALE).T
            for h in range(heads_per_slab):
                head = (j * (wide // LANES) + jj) * heads_per_slab + h
                qt_ref[head] = qt[h * HEAD_DIM:(h + 1) * HEAD_DIM].astype(BF16)

    def kv_heads():
        yk = mm(OFF_K, OFF_V)
        k = norm_rope(yk, head_sumsq(yk), gk_ref[...])
        vt = mm(OFF_V, OFF_U).astype(BF16).astype(F32).T
        for g in range(N_KV_HEADS):
            kh_ref[g] = k[:, g * HEAD_DIM:(g + 1) * HEAD_DIM].astype(BF16)
            for jb in range(tm // tk):
                vt_ref[g, jb] = vt[g * HEAD_DIM:(g + 1) * HEAD_DIM, jb * tk:(jb + 1) * tk].astype(BF16)

    def ssm_input_regroup():
        n_chunks = tm // SSM_CHUNK
        for gh in range(D_SSM // LANES):
            for ih in range(SSM_CHUNK // LANE_BLOCKS):
                rows = [u_sc[gh, pl.ds(LANE_BLOCKS * ih + a, n_chunks, stride=SSM_CHUNK), :]
                        for a in range(LANE_BLOCKS)]
                for gl, blk in enumerate(_block_transpose(rows)):
                    ug_ref[LANE_BLOCKS * gh + gl, :, ih * LANES:(ih + 1) * LANES] = blk

    u = mm(OFF_U, OFF_GA)
    for gh in range(D_SSM // LANES):
        u_sc[gh] = u[:, gh * LANES:(gh + 1) * LANES]
    q_slab(0)
    ga_ref[...] = _sigmoid(mm(OFF_GA, OFF_GS)).astype(BF16)
    ssm_input_regroup()
    q_slab(1)
    kv_heads()
    gs_ref[...] = _sigmoid(mm(OFF_GS, D_IN_PROJ)).astype(BF16)


def _rope_tables(t_max):
    t = jnp.arange(t_max, dtype=jnp.int32)
    row_pos = (t // GRID_W).astype(F32)
    col_pos = (t % GRID_W).astype(F32)
    n_freq = HEAD_DIM // 4
    inv_freq = ROPE_THETA ** (-jnp.arange(n_freq, dtype=F32) / n_freq)
    ang = jnp.concatenate([row_pos[:, None] * inv_freq, col_pos[:, None] * inv_freq], axis=-1)
    c, s = jnp.cos(ang), jnp.sin(ang)
    reps = LANES // HEAD_DIM
    cos_t = jnp.tile(jnp.concatenate([c, c], axis=-1), (1, reps))
    sin_t = jnp.tile(jnp.concatenate([-s, s], axis=-1), (1, reps))
    return cos_t, sin_t


def _in_projection(x, t, w_in_bf16, q_norm_g, k_norm_g, tm, tk):
    n = x.shape[0]
    tiles_per_seq = t // tm
    cos_t, sin_t = _rope_tables(t)
    reps = LANES // HEAD_DIM
    gq = jnp.tile(q_norm_g.astype(F32), reps)[None, :]
    gk = jnp.tile(k_norm_g.astype(F32), reps)[None, :]
    head_of_lane = jnp.arange(MXU_DIM) // HEAD_DIM
    seg = (head_of_lane[:, None] == head_of_lane[None, :]).astype(BF16)
    pos_map = lambda i: (i % tiles_per_seq, 0)
    row = lambda i: (i, 0)
    const = lambda i: (0, 0)
    return pl.pallas_call(
        functools.partial(_inproj_body, tk=tk),
        grid=(n // tm,),
        in_specs=[
            pl.BlockSpec((tm, D_MODEL), row),
            pl.BlockSpec((D_MODEL, D_IN_PROJ), const),
            pl.BlockSpec((1, LANES), const),
            pl.BlockSpec((1, LANES), const),
            pl.BlockSpec((tm, LANES), pos_map),
            pl.BlockSpec((tm, LANES), pos_map),
            pl.BlockSpec((MXU_DIM, MXU_DIM), const),
        ],
        out_specs=[
            pl.BlockSpec((N_Q_HEADS, HEAD_DIM, tm), lambda i: (0, 0, i)),
            pl.BlockSpec((N_KV_HEADS, tm, HEAD_DIM), lambda i: (0, i, 0)),
            pl.BlockSpec((N_KV_HEADS, tm // tk, HEAD_DIM, tk), lambda i: (0, i, 0, 0)),
            pl.BlockSpec((N_SSM_GROUPS, tm // SSM_CHUNK, CHUNK_W), lambda i: (0, i, 0)),
            pl.BlockSpec((tm, D_MODEL), row),
            pl.BlockSpec((tm, D_MODEL), row),
        ],
        out_shape=[
            jax.ShapeDtypeStruct((N_Q_HEADS, HEAD_DIM, n), BF16),
            jax.ShapeDtypeStruct((N_KV_HEADS, n, HEAD_DIM), BF16),
            jax.ShapeDtypeStruct((N_KV_HEADS, n // tk, HEAD_DIM, tk), BF16),
            jax.ShapeDtypeStruct((N_SSM_GROUPS, n // SSM_CHUNK, CHUNK_W), F32),
            jax.ShapeDtypeStruct((n, D_MODEL), BF16),
            jax.ShapeDtypeStruct((n, D_MODEL), BF16),
        ],
        scratch_shapes=[pltpu.VMEM((D_SSM // LANES, tm, LANES), F32)],
        compiler_params=_cparams(("parallel",)),
        name="in_projection",
    )(x, w_in_bf16, gq, gk, cos_t, sin_t, seg)


def _attn_body(qt_ref, k_ref, vt_ref, o_ref, m_sc, l_sc, acc_sc, s_sc, *, tq, tk, n_qt, n_kv, shift):
    n_blocks = n_qt * n_kv

    def scores(f):
        qo = pl.multiple_of((f // n_kv) * tq, tq)
        ko = pl.multiple_of((f % n_kv) * tk, tk)
        q4t = jnp.concatenate([qt_ref[h, :, pl.ds(qo, tq)] for h in range(Q_PER_KV)], axis=1)
        return _dot(k_ref[0, pl.ds(ko, tk), :], q4t)

    def row_groups(p):
        return jnp.sum(p.reshape(tk // 8, 8, p.shape[1]), axis=0)

    def softmax_pv(s, j):
        if shift:
            m_old = m_sc[...]
            m_new = jnp.maximum(m_old, jnp.max(s, axis=0, keepdims=True))
            alpha = jnp.exp2(m_old - m_new)
            p = jnp.exp2(s - m_new)
            l_sc[...] = alpha * l_sc[...] + row_groups(p)
            acc_sc[...] = alpha * acc_sc[...] + _dot(vt_ref[0, j], p.astype(BF16))
            m_sc[...] = m_new
        else:
            p = jnp.exp2(s)
            l_sc[...] += row_groups(p)
            acc_sc[...] += _dot(vt_ref[0, j], p.astype(BF16))

    s_sc[0] = scores(0)

    def query_tile(qi, carry):
        m_sc[...] = jnp.full(m_sc.shape, -jnp.inf, F32)
        l_sc[...] = jnp.zeros(l_sc.shape, F32)
        acc_sc[...] = jnp.zeros(acc_sc.shape, F32)

        per_trip = math.gcd(KV_BLOCKS_PER_TRIP, n_kv)

        def body(jj, c):
            for u in range(0, per_trip, 2):
                j = per_trip * jj + u
                f = qi * n_kv + j
                s_sc[1] = scores(f + 1)
                softmax_pv(s_sc[0], j)
                s_sc[0] = scores(jnp.minimum(f + 2, n_blocks - 1))
                softmax_pv(s_sc[1], j + 1)
            return c

        lax.fori_loop(0, n_kv // per_trip, body, 0)
        out = acc_sc[...] / jnp.sum(l_sc[...], axis=0, keepdims=True)
        qo = pl.multiple_of(qi * tq, tq)
        for h in range(Q_PER_KV):
            o_ref[h, :, pl.ds(qo, tq)] = out[:, h * tq:(h + 1) * tq].astype(o_ref.dtype)
        return carry

    lax.fori_loop(0, n_qt, query_tile, 0)


def _attention(qt, k_heads, vt_blocks, q_norm_g, k_norm_g, b, t, tq, tk):
    n_qt = t // tq
    n_kv = t // tk
    bound = QK_SCALE * HEAD_DIM * jnp.max(jnp.abs(q_norm_g.astype(F32))) * jnp.max(jnp.abs(k_norm_g.astype(F32)))
    call = functools.partial(_attention_call, b=b, t=t, tq=tq, tk=tk, n_qt=n_qt, n_kv=n_kv)
    return lax.cond(bound < MAX_UNSHIFTED_EXPONENT,
                    functools.partial(call, shift=False), functools.partial(call, shift=True),
                    qt, k_heads, vt_blocks)


def _attention_call(qt, k_heads, vt_blocks, *, b, t, tq, tk, n_qt, n_kv, shift):
    body = functools.partial(_attn_body, tq=tq, tk=tk, n_qt=n_qt, n_kv=n_kv, shift=shift)
    return pl.pallas_call(
        body,
        grid=(b, N_KV_HEADS),
        in_specs=[
            pl.BlockSpec((Q_PER_KV, HEAD_DIM, t), lambda bi, g: (g, 0, bi)),
            pl.BlockSpec((1, t, HEAD_DIM), lambda bi, g: (g, bi, 0)),
            pl.BlockSpec((1, n_kv, HEAD_DIM, tk), lambda bi, g: (g, bi, 0, 0)),
        ],
        out_specs=pl.BlockSpec((Q_PER_KV, HEAD_DIM, t), lambda bi, g: (g, 0, bi)),
        out_shape=jax.ShapeDtypeStruct((N_Q_HEADS, HEAD_DIM, b * t), BF16),
        scratch_shapes=[
            pltpu.VMEM((1, Q_PER_KV * tq), F32),
            pltpu.VMEM((8, Q_PER_KV * tq), F32),
            pltpu.VMEM((HEAD_DIM, Q_PER_KV * tq), F32),
            pltpu.VMEM((2, tk, Q_PER_KV * tq), F32),
        ],
        compiler_params=_cparams(("parallel", "parallel")),
        name="attention",
    )(qt, k_heads, vt_blocks)


def _s5_matrices(a_re, a_im, log_dt, b_re, b_im, c_re, c_im, d_skip):
    hp = lax.Precision.HIGHEST
    L, G, P, H = SSM_CHUNK, N_SSM_GROUPS, SSM_STATE, SSM_GROUP
    f = lambda v: v.astype(F32)
    lr, li = f(a_re), f(a_im)
    dt = jnp.exp(f(log_dt))[..., None]
    mag = jnp.exp(lr * dt)
    ar = mag * jnp.cos(li * dt)
    ai = mag * jnp.sin(li * dt)
    den = lr * lr + li * li
    kr = ((ar - 1.0) * lr + ai * li) / den
    ki = (ai * lr - (ar - 1.0) * li) / den
    br, bi = f(b_re), f(b_im)
    bbr = kr[..., None] * br - ki[..., None] * bi
    bbi = kr[..., None] * bi + ki[..., None] * br
    kk = jnp.arange(L + 1, dtype=F32)[:, None, None, None]
    pmag = jnp.exp(kk * (lr * dt)[None])
    ph = kk * (li * dt)[None]
    pr = pmag * jnp.cos(ph)
    pi = pmag * jnp.sin(ph)
    cr, ci = f(c_re), f(c_im)
    car = cr[None] * pr[:, :, :, None, :] - ci[None] * pi[:, :, :, None, :]
    cai = cr[None] * pi[:, :, :, None, :] + ci[None] * pr[:, :, :, None, :]
    kern = jnp.einsum('kdgoq,dgqi->kdgoi', jnp.concatenate([car, -cai], axis=-1),
                      jnp.concatenate([bbr, bbi], axis=2), precision=hp)
    jj = jnp.arange(L)[:, None]
    ii = jnp.arange(L)[None, :]

    def toeplitz(kd, lag, valid):
        blocks = jnp.where(valid[:, :, None, None, None], kd[jnp.clip(lag, 0, L)], 0.0)
        return blocks.transpose(2, 0, 4, 1, 3).reshape(G, CHUNK_W, CHUNK_W)

    msum = toeplitz(kern[:, 0], ii - jj, ii >= jj) + toeplitz(kern[:, 1], jj - ii, jj >= ii)

    def in_op(d, powers):
        prj, pij = pr[powers, d], pi[powers, d]
        re = prj[..., None] * bbr[d][None] - pij[..., None] * bbi[d][None]
        im = prj[..., None] * bbi[d][None] + pij[..., None] * bbr[d][None]
        to = lambda m: m.transpose(1, 0, 3, 2).reshape(G, CHUNK_W, P)
        return to(re), to(im)

    def out_op(d, powers):
        to = lambda m: m.transpose(1, 3, 0, 2).reshape(G, P, CHUNK_W)
        return to(car[powers, d]), to(-cai[powers, d])

    steps = jnp.arange(L)
    bm = jnp.stack(in_op(0, L - 1 - steps) + in_op(1, steps), axis=1)
    cm = jnp.stack(out_op(0, steps + 1) + out_op(1, L - steps), axis=1)
    al = jnp.stack([pr[L, 0], pi[L, 0], pr[L, 1], pi[L, 1]], axis=1)[:, :, None, :]
    dvec = jnp.tile(f(d_skip).reshape(G, 1, H), (1, 1, L))
    z_in = jnp.zeros((G // 2, 4, CHUNK_W, P), F32)
    z_out = jnp.zeros((G // 2, 4, P, CHUNK_W), F32)
    bm_pair = jnp.concatenate([jnp.concatenate([bm[0::2], z_in], axis=3),
                               jnp.concatenate([z_in, bm[1::2]], axis=3)], axis=2)
    cm_pair = jnp.concatenate([jnp.concatenate([cm[0::2], z_out], axis=3),
                               jnp.concatenate([z_out, cm[1::2]], axis=3)], axis=2)
    al_pair = jnp.concatenate([al[0::2], al[1::2]], axis=3)
    bm_all = jnp.concatenate([bm_pair[:, d] for d in range(4)], axis=2)
    cm_all = jnp.concatenate([cm_pair[:, d] for d in range(4)], axis=1)
    return msum.astype(BF16), bm_all.astype(BF16), cm_all.astype(BF16), al_pair, dvec


def _s5_body(u_ref, m_ref, bm_ref, cm_ref, al_ref, d_ref, y_ref, st_sc, *, nb, nc):
    u0, u1 = u_ref[0], u_ref[1]
    ub = jnp.concatenate([u0.astype(BF16), u1.astype(BF16)], axis=1)
    local = _dot(ub, bm_ref[0])
    for d in range(4):
        st_sc[d] = local[:, d * LANES:(d + 1) * LANES]
    al = [al_ref[0, d] for d in range(4)]

    def advance(rows, d, sr, si):
        lr = st_sc[d, rows, :]
        li = st_sc[d + 1, rows, :]
        st_sc[d, rows, :] = sr
        st_sc[d + 1, rows, :] = si
        ar, ai = al[d], al[d + 1]
        return ar * sr - ai * si + lr, ar * si + ai * sr + li

    def step(k, st):
        fr, fi, br, bi = st
        fr, fi = advance(pl.ds(k, nb, stride=nc), 0, fr, fi)
        br, bi = advance(pl.ds(nc - 1 - k, nb, stride=nc), 2, br, bi)
        return fr, fi, br, bi

    lax.fori_loop(0, nc, step, tuple(jnp.zeros((nb, LANES), F32) for _ in range(4)), unroll=8)

    entering = jnp.concatenate([st_sc[d].astype(BF16) for d in range(4)], axis=1)
    y = _dot(entering, cm_ref[0])
    y_ref[0] = y[:, :CHUNK_W] + _dot(ub[:, :CHUNK_W], m_ref[0]) + u0 * d_ref[0]
    y_ref[1] = y[:, CHUNK_W:] + _dot(ub[:, CHUNK_W:], m_ref[1]) + u1 * d_ref[1]


def _s5_scan(ug, mats, nb, nc):
    msum, bm, cm, al, dvec = mats
    rows = nb * nc
    pair3 = lambda g: (g, 0, 0)
    pair4 = lambda g: (g, 0, 0, 0)
    return pl.pallas_call(
        functools.partial(_s5_body, nb=nb, nc=nc),
        grid=(N_SSM_GROUPS // 2,),
        in_specs=[
            pl.BlockSpec((2, rows, CHUNK_W), pair3),
            pl.BlockSpec((2, CHUNK_W, CHUNK_W), pair3),
            pl.BlockSpec((1, 2 * CHUNK_W, 4 * LANES), pair3),
            pl.BlockSpec((1, 4 * LANES, 2 * CHUNK_W), pair3),
            pl.BlockSpec((1, 4, 1, LANES), pair4),
            pl.BlockSpec((2, 1, CHUNK_W), pair3),
        ],
        out_specs=pl.BlockSpec((2, rows, CHUNK_W), pair3),
        out_shape=jax.ShapeDtypeStruct((N_SSM_GROUPS, rows, CHUNK_W), F32),
        scratch_shapes=[pltpu.VMEM((4, rows, LANES), F32)],
        compiler_params=_cparams(("parallel",)),
        name="s5_scan",
    )(ug, msum, bm, cm, al, dvec)


def _route(logits):
    lane_i = lax.broadcasted_iota(jnp.int32, logits.shape, 1)
    lane = lane_i.astype(F32)
    neg = -jnp.inf
    big = float(LANES)
    is_g = lane_i < N_EXPERT_GROUPS
    gl = jnp.where(is_g, logits, neg)
    gmax = jnp.max(gl, axis=-1, keepdims=True)
    gidx = jnp.min(jnp.where(gl == gmax, lane, big), axis=-1, keepdims=True)
    g_w = 1.0 / jnp.sum(jnp.where(is_g, jnp.exp(logits - gmax), 0.0), axis=-1, keepdims=True)
    e_lane = lane_i - ROUTE_E0
    group_of_lane = (e_lane // EXPERTS_PER_GROUP).astype(F32)
    in_group = (e_lane >= 0) & (e_lane < N_EXPERTS) & (group_of_lane == gidx)
    el = jnp.where(in_group, logits, neg)
    v1 = jnp.max(el, axis=-1, keepdims=True)
    i1 = jnp.min(jnp.where(el == v1, lane, big), axis=-1, keepdims=True)
    el2 = jnp.where(lane == i1, neg, el)
    v2 = jnp.max(el2, axis=-1, keepdims=True)
    i2 = jnp.min(jnp.where(el2 == v2, lane, big), axis=-1, keepdims=True)
    e2 = jnp.exp(v2 - v1)
    inv = g_w / (1.0 + e2)
    combine = jnp.where(lane == i1, inv, 0.0) + jnp.where(lane == i2, e2 * inv, 0.0)
    return combine, (lane == i1) | (lane == i2)


AUX_LANE = LANES - 1


def _slot_tables(cap):
    slot = jnp.arange(N_EXPERTS * cap)
    lane = jnp.arange(LANES)[:, None]
    expand = (lane == ROUTE_E0 + slot // cap).astype(F32)
    expand_aug = jnp.where(lane == AUX_LANE, -(slot % cap).astype(F32), expand)
    return expand.astype(BF16), expand_aug.astype(BF16)


def _merge_body(x_ref, ot_ref, yg_ref, ga_ref, gs_ref, wua_ref, wglu_ref, bglu_ref, wus_ref, wout_ref,
                g1_ref, b1_ref, wr_ref, br_ref, tri_ref, eaugt_ref,
                x1_ref, route_ref, rank_ref, xs_ref, y_sc, *, cap):
    tm = x_ref.shape[0]
    n_chunks = tm // SSM_CHUNK
    for gh in range(D_SSM // LANES):
        for ih in range(SSM_CHUNK // LANE_BLOCKS):
            blocks = [yg_ref[LANE_BLOCKS * gh + gl, :, ih * LANES:(ih + 1) * LANES] for gl in range(LANE_BLOCKS)]
            for a, blk in enumerate(_block_transpose(blocks)):
                y_sc[gh, pl.ds(LANE_BLOCKS * ih + a, n_chunks, stride=SSM_CHUNK), :] = blk

    def token_rows(r0, nr):
        rows = slice(r0, r0 + nr)
        y = jnp.concatenate([y_sc[gh, rows, :] for gh in range(D_SSM // LANES)], axis=1)
        o = ot_ref[:, :, rows].reshape(D_ATTN, nr).astype(F32).T.astype(BF16)
        z = jax.nn.gelu(y)
        z = z * _sigmoid(_dot(z.astype(BF16), wglu_ref[...]) + bglu_ref[...])
        ssm_up = _dot(z.astype(BF16), wus_ref[...])
        att_up = _dot(o, wua_ref[...])
        merged = ga_ref[rows, :].astype(F32) * att_up + gs_ref[rows, :].astype(F32) * ssm_up
        mix = _dot(merged.astype(BF16), wout_ref[...])
        x1 = _layer_norm(ALPHA * x_ref[rows, :] + mix, g1_ref[...], b1_ref[...])
        x1_ref[rows, :] = x1
        xh = x1.astype(BF16)
        combine, sel = _route(_dot(xh, wr_ref[...]) + br_ref[...])
        route_ref[rows, :] = combine
        return xh, jnp.where(sel, 1.0, 0.0)

    nr = tm // MERGE_ROW_BLOCKS
    parts = [token_rows(i * nr, nr) for i in range(MERGE_ROW_BLOCKS)]
    xh = jnp.concatenate([p[0] for p in parts], axis=0)
    sel_f = jnp.concatenate([p[1] for p in parts], axis=0)
    sel = sel_f > 0.0
    rank = _dot(tri_ref[...], sel_f.astype(BF16))
    lane = lax.broadcasted_iota(jnp.int32, rank.shape, 1)
    rank = jnp.where(sel, jnp.minimum(rank, float(cap)), float(cap))
    rank = jnp.where(lane == AUX_LANE, 1.0, rank).astype(BF16)
    rank_ref[...] = rank
    miss = lax.dot_general(eaugt_ref[...], rank, (((1,), (1,)), ((), ())), preferred_element_type=F32)
    gather = jnp.where(miss == 0.0, 1.0, 0.0).astype(BF16)
    xs_ref[0] = _dot(gather, xh).astype(BF16)


def _merge_weights(w, tm, cap):
    n_r = N_EXPERT_GROUPS + N_EXPERTS
    wr = jnp.zeros((D_MODEL, LANES), F32).at[:, :n_r].set(
        jnp.concatenate([w['w_router_group'], w['w_router_expert']], axis=1).astype(F32))
    br = jnp.zeros((1, LANES), F32).at[0, :n_r].set(
        jnp.concatenate([w['b_router_group'], w['b_router_expert']]).astype(F32))
    tri =(jnp.arange(tm)[:, None] > jnp.arange(tm)[None, :]).astype(BF16)
    _, expand_aug = _slot_tables(cap)
    return [w['w_up_attn'].astype(BF16), w['w_glu'].astype(BF16), w['b_glu'].astype(F32)[None, :],
            w['w_up_ssm'].astype(BF16), w['w_out'].astype(BF16), w['ln1_g'].astype(F32)[None, :],
            w['ln1_b'].astype(F32)[None, :], wr.astype(BF16), br, tri, expand_aug.T]


def _merge(x, ot, yg, ga, gs, weights, tm, cap):
    n = x.shape[0]
    n_slot = N_EXPERTS * cap
    row = lambda i: (i, 0)
    const = lambda i: (0, 0)
    full = lambda a: pl.BlockSpec(a.shape, const)
    return pl.pallas_call(
        functools.partial(_merge_body, cap=cap),
        grid=(n // tm,),
        in_specs=[pl.BlockSpec((tm, D_MODEL), row),
                  pl.BlockSpec((N_Q_HEADS, HEAD_DIM, tm), lambda i: (0, 0, i)),
                  pl.BlockSpec((N_SSM_GROUPS, tm // SSM_CHUNK, CHUNK_W), lambda i: (0, i, 0)),
                  pl.BlockSpec((tm, D_MODEL), row), pl.BlockSpec((tm, D_MODEL), row)] + [full(a) for a in weights],
        out_specs=[pl.BlockSpec((tm, D_MODEL), row), pl.BlockSpec((tm, LANES), row), pl.BlockSpec((tm, LANES), row),
                   pl.BlockSpec((1, n_slot, D_MODEL), lambda i: (i, 0, 0))],
        out_shape=[jax.ShapeDtypeStruct((n, D_MODEL), F32), jax.ShapeDtypeStruct((n, LANES), F32),
                   jax.ShapeDtypeStruct((n, LANES), BF16), jax.ShapeDtypeStruct((n // tm, n_slot, D_MODEL), BF16)],
        scratch_shapes=[pltpu.VMEM((D_SSM // LANES, tm, LANES), F32)],
        compiler_params=_cparams(("parallel",)),
        name="merge_ln1_router",
    )(x, ot, yg, ga, gs, *weights)


def _expert_mlp(xb, wg, wu, wd):
    h = jax.nn.silu(_dot(xb, wg)) * _dot(xb, wu)
    return _dot(h.astype(BF16), wd)


def _experts_body(xs_ref, wg_ref, wu_ref, wd_ref, ys_ref, wg_sc, wu_sc, wd_sc):
    @pl.when(pl.program_id(1) == 0)
    def _():
        wg_sc[...] = wg_ref[0].astype(BF16)
        wu_sc[...] = wu_ref[0].astype(BF16)
        wd_sc[...] = wd_ref[0].astype(BF16)

    shp = xs_ref.shape
    xb = xs_ref[...].reshape(shp[0] * shp[2], D_MODEL)
    ys_ref[...] = _expert_mlp(xb, wg_sc[...], wu_sc[...], wd_sc[...]).astype(BF16).reshape(shp)


def _experts(xs, w_gate, w_up, w_down, cap, tiles_per_step):
    n_tiles = xs.shape[0]
    xs4 = xs.reshape(n_tiles, N_EXPERTS, cap, D_MODEL)
    slots = pl.BlockSpec((tiles_per_step, 1, cap, D_MODEL), lambda e, t: (t, e, 0, 0))
    exp3 = lambda e, t: (e, 0, 0)
    ys = pl.pallas_call(
        _experts_body,
        grid=(N_EXPERTS, n_tiles // tiles_per_step),
        in_specs=[slots, pl.BlockSpec((1, D_MODEL, D_FF_EXPERT), exp3), pl.BlockSpec((1, D_MODEL, D_FF_EXPERT), exp3),
                  pl.BlockSpec((1, D_FF_EXPERT, D_MODEL), exp3)],
        out_specs=slots,
        out_shape=jax.ShapeDtypeStruct(xs4.shape, BF16),
        scratch_shapes=[pltpu.VMEM((D_MODEL, D_FF_EXPERT), BF16), pltpu.VMEM((D_MODEL, D_FF_EXPERT), BF16),
                        pltpu.VMEM((D_FF_EXPERT, D_MODEL), BF16)],
        compiler_params=_cparams(("parallel", "arbitrary")),
        name="experts",
    )(xs4, w_gate, w_up, w_down)
    return ys.reshape(xs.shape)


def _combine_body(x1_ref, route_ref, rank_ref, ys_ref, eaug_ref, e_ref, g2_ref, b2_ref, wg_hbm, wu_hbm, wd_hbm,
                  o_ref, acc_sc, wg_sc, wu_sc, wd_sc, *, cap):
    route = route_ref[...]
    rank = rank_ref[...]
    hit = _dot(rank, eaug_ref[...]) == 0.0
    spread = _dot(route.astype(BF16), e_ref[...])
    acc_sc[...] = _dot(jnp.where(hit, spread, 0.0).astype(BF16), ys_ref[0])
    lane = lax.broadcasted_iota(jnp.int32, route.shape, 1)
    spilled = jnp.where((rank.astype(F32) == float(cap)) & (lane != AUX_LANE), route, 0.0)

    @pl.when(jnp.max(jnp.abs(spilled)) > 0.0)
    def _():
        xb = x1_ref[...].astype(BF16)

        def one_expert(e, carry):
            pltpu.sync_copy(wg_hbm.at[e], wg_sc)
            pltpu.sync_copy(wu_hbm.at[e], wu_sc)
            pltpu.sync_copy(wd_hbm.at[e], wd_sc)
            w_e = jnp.sum(jnp.where(lane == e + ROUTE_E0, spilled, 0.0), axis=-1, keepdims=True)
            acc_sc[...] += w_e * _expert_mlp(xb, wg_sc[...].astype(BF16), wu_sc[...].astype(BF16),
                                             wd_sc[...].astype(BF16))
            return carry

        lax.fori_loop(0, N_EXPERTS, one_expert, 0)

    o_ref[...] = _layer_norm(ALPHA * x1_ref[...] + acc_sc[...], g2_ref[...], b2_ref[...])


def _combine(x1, route, rank, ys, w_gate, w_up, w_down, ln2_g, ln2_b, tm, cap):
    n = x1.shape[0]
    n_slot = N_EXPERTS * cap
    expand, expand_aug = _slot_tables(cap)
    row = lambda i: (i, 0)
    const = lambda i: (0, 0)
    hbm = pl.BlockSpec(memory_space=pl.ANY)
    return pl.pallas_call(
        functools.partial(_combine_body, cap=cap),
        grid=(n // tm,),
        in_specs=[pl.BlockSpec((tm, D_MODEL), row), pl.BlockSpec((tm, LANES), row), pl.BlockSpec((tm, LANES), row),
                  pl.BlockSpec((1, n_slot, D_MODEL), lambda i: (i, 0, 0)),
                  pl.BlockSpec((LANES, n_slot), const), pl.BlockSpec((LANES, n_slot), const),
                  pl.BlockSpec((1, D_MODEL), const), pl.BlockSpec((1, D_MODEL), const), hbm, hbm, hbm],
        out_specs=pl.BlockSpec((tm, D_MODEL), row),
        out_shape=jax.ShapeDtypeStruct((n, D_MODEL), F32),
        scratch_shapes=[pltpu.VMEM((tm, D_MODEL), F32), pltpu.VMEM((D_MODEL, D_FF_EXPERT), w_gate.dtype),
                        pltpu.VMEM((D_MODEL, D_FF_EXPERT), w_up.dtype), pltpu.VMEM((D_FF_EXPERT, D_MODEL), w_down.dtype)],
        compiler_params=_cparams(("parallel",)),
        name="combine_ln2",
    )(x1, route, rank, ys, expand_aug, expand, ln2_g.astype(F32)[None, :], ln2_b.astype(F32)[None, :],
      w_gate, w_up, w_down)


def _encode(x, w, prep, cap):
    b, t, _ = x.shape
    n = b * t
    tm, tq, tk = math.gcd(512, t), math.gcd(256, t), math.gcd(512, t // 2)
    x2 = x.reshape(n, D_MODEL).astype(F32)
    qt, kh, vt, ug, ga, gs = _in_projection(x2, t, prep['w_in'], w['q_norm_g'], w['k_norm_g'], tm, tk)
    ot = _attention(qt, kh, vt, w['q_norm_g'], w['k_norm_g'], b, t, tq, tk)
    yg = _s5_scan(ug, prep['s5'], b, t // SSM_CHUNK)
    x1, route, rank, xs = _merge(x2, ot, yg, ga, gs, _merge_weights(w, tm, cap), tm, cap)
    wg, wu, wd = prep['experts']
    ys = _experts(xs, wg, wu, wd, cap, math.gcd(8, n // tm))
    out = _combine(x1, route, rank, ys, wg, wu, wd, w['ln2_g'], w['ln2_b'], tm, cap)
    return out.reshape(b, t, D_MODEL)


def kernel(x_prompt, x_sample, w_in, q_norm_g, k_norm_g, ssm_a_re, ssm_a_im, ssm_log_dt, ssm_b_re, ssm_b_im, ssm_c_re, ssm_c_im, ssm_d, w_glu, b_glu, w_up_attn, w_up_ssm, w_out, ln1_g, ln1_b, w_router_group, b_router_group, w_router_expert, b_router_expert, w_exp_gate, w_exp_up, w_exp_down, ln2_g, ln2_b):
    w = dict(w_in=w_in, q_norm_g=q_norm_g, k_norm_g=k_norm_g, ssm_a_re=ssm_a_re, ssm_a_im=ssm_a_im,
             ssm_log_dt=ssm_log_dt, ssm_b_re=ssm_b_re, ssm_b_im=ssm_b_im, ssm_c_re=ssm_c_re, ssm_c_im=ssm_c_im,
             ssm_d=ssm_d, w_glu=w_glu, b_glu=b_glu, w_up_attn=w_up_attn, w_up_ssm=w_up_ssm, w_out=w_out,
             ln1_g=ln1_g, ln1_b=ln1_b, w_router_group=w_router_group, b_router_group=b_router_group,
             w_router_expert=w_router_expert, b_router_expert=b_router_expert, w_exp_gate=w_exp_gate,
             w_exp_up=w_exp_up, w_exp_down=w_exp_down, ln2_g=ln2_g, ln2_b=ln2_b)
    w = {name: val[0] for name, val in w.items()}
    prep = dict(
        w_in=w['w_in'].astype(BF16),
        s5=_s5_matrices(w['ssm_a_re'], w['ssm_a_im'], w['ssm_log_dt'], w['ssm_b_re'], w['ssm_b_im'],
                        w['ssm_c_re'], w['ssm_c_im'], w['ssm_d']),
        experts=tuple(w[name] for name in ('w_exp_gate', 'w_exp_up', 'w_exp_down')))
    return _encode(x_prompt, w, prep, EXPERT_CAP), _encode(x_sample, w, prep, EXPERT_CAP)
```

```python
import functools
import math

import jax
import jax.numpy as jnp
from jax import lax
from jax.experimental import pallas as pl
from jax.experimental.pallas import tpu as pltpu

F32 = jnp.float32
BF16 = jnp.bfloat16

D_MODEL = 1024
GRID_W = 64
HEAD_DIM = 64
N_Q_HEADS = 8
N_KV_HEADS = 2
Q_PER_KV = N_Q_HEADS // N_KV_HEADS
D_ATTN = N_Q_HEADS * HEAD_DIM
D_KV = N_KV_HEADS * HEAD_DIM
ROPE_THETA = 10000.0
D_SSM = D_MODEL // 2
SSM_GROUP = 16
N_SSM_GROUPS = D_SSM // SSM_GROUP
SSM_STATE = 64
N_EXPERT_GROUPS = 4
EXPERTS_PER_GROUP = 4
N_EXPERTS = N_EXPERT_GROUPS * EXPERTS_PER_GROUP
D_FF_EXPERT = D_MODEL // 2
DEPTH = 1
ALPHA = (2.0 * DEPTH) ** 0.25
EPS = 1e-6
D_IN_PROJ = D_ATTN + 2 * D_KV + D_SSM + 2 * D_MODEL
OFF_K = D_ATTN
OFF_V = OFF_K + D_KV
OFF_U = OFF_V + D_KV
OFF_GA = OFF_U + D_SSM
OFF_GS = OFF_GA + D_MODEL

LANES = 128
MXU_DIM = 256
SSM_CHUNK = 16
CHUNK_W = SSM_CHUNK * SSM_GROUP
ROUTE_E0 = N_EXPERT_GROUPS
VMEM_LIMIT = 56 * 1024 * 1024
EXPERT_CAP = 96


def _cparams(sem):
    return pltpu.CompilerParams(dimension_semantics=sem, vmem_limit_bytes=VMEM_LIMIT)


def _dot(a, b):
    return jnp.dot(a, b, preferred_element_type=F32)


def _split_bf16(x):
    hi = x.astype(BF16)
    lo = (x - hi.astype(F32)).astype(BF16)
    return hi, lo


def _sigmoid(x):
    return 0.5 * jnp.tanh(0.5 * x) + 0.5


def _layer_norm(h, g, b):
    mu = jnp.mean(h, axis=-1, keepdims=True)
    var = jnp.mean(jnp.square(h - mu), axis=-1, keepdims=True)
    return (h - mu) * lax.rsqrt(var + EPS) * g + b


QK_SCALE = HEAD_DIM ** -0.5 * math.log2(math.e)
MERGE_ROW_BLOCKS = 2
KV_BLOCKS_PER_TRIP = 8
MAX_UNSHIFTED_EXPONENT = 40.0
LANE_BLOCKS = LANES // SSM_GROUP


def _block_transpose(blocks):
    lane = lax.broadcasted_iota(jnp.int32, blocks[0].shape, 1)
    v = list(blocks)
    for s in range(3):
        d = SSM_GROUP << s
        keep = ((lane // d) % 2) == 0
        nxt = list(v)
        for a in range(LANE_BLOCKS):
            if not a & (1 << s):
                a2 = a | (1 << s)
                nxt[a] = jnp.where(keep, v[a], pltpu.roll(v[a2], d, 1))
                nxt[a2] = jnp.where(keep, pltpu.roll(v[a], LANES - d, 1), v[a2])
        v = nxt
    return v


def _inproj_body(x_ref, w_ref, gq_ref, gk_ref, cos_ref, sin_ref, seg_ref,
                 qt_ref, kh_ref, vt_ref, ug_ref, ga_ref, gs_ref, u_sc, *, tk):
    xb = x_ref[...].astype(BF16)
    tm = xb.shape[0]
    cos = cos_ref[...]
    sin = sin_ref[...]
    seg = seg_ref[...]
    lane = lax.broadcasted_iota(jnp.int32, (1, LANES), 1)
    first_half = (lane % HEAD_DIM) < (HEAD_DIM // 2)

    def mm(a, b):
        return _dot(xb, w_ref[:, a:b])

    def head_sumsq(y):
        w = y.shape[1]
        return _dot((y * y).astype(BF16), seg[0:w, 0:w])

    def norm_rope(y, ss, g):
        yn = y * lax.rsqrt(ss * (1.0 / HEAD_DIM) + EPS) * g
        sw = jnp.where(first_half, pltpu.roll(yn, LANES - HEAD_DIM // 2, 1), pltpu.roll(yn, HEAD_DIM // 2, 1))
        return yn * cos + sw * sin

    gq = gq_ref[...]
    heads_per_slab = LANES // HEAD_DIM
    wide = seg.shape[0]

    def q_slab(j):
        y = mm(j * wide, (j + 1) * wide)
        ss = head_sumsq(y)
        for jj in range(wide // LANES):
            q = norm_rope(y[:, jj * LANES:(jj + 1) * LANES], ss[:, jj * LANES:(jj + 1) * LANES], gq)
            qt = (q * QK_SCALE).T
            for h in range(heads_per_slab):
                head = (j * (wide // LANES) + jj) * heads_per_slab + h
                qt_ref[head] = qt[h * HEAD_DIM:(h + 1) * HEAD_DIM].astype(BF16)

    def kv_heads():
        yk = mm(OFF_K, OFF_V)
        k = norm_rope(yk, head_sumsq(yk), gk_ref[...])
        vt = mm(OFF_V, OFF_U).astype(BF16).astype(F32).T
        for g in range(N_KV_HEADS):
            kh_ref[g] = k[:, g * HEAD_DIM:(g + 1) * HEAD_DIM].astype(BF16)
            w = vt_ref.shape[3]
            for jb in range(tm // w):
                vt_ref[g, jb] = vt[g * HEAD_DIM:(g + 1) * HEAD_DIM, jb * w:(jb + 1) * w].astype(BF16)

    def ssm_input_regroup():
        n_chunks = tm // SSM_CHUNK
        for gh in range(D_SSM // LANES):
            for ih in range(SSM_CHUNK // LANE_BLOCKS):
                rows = [u_sc[gh, pl.ds(LANE_BLOCKS * ih + a, n_chunks, stride=SSM_CHUNK), :]
                        for a in range(LANE_BLOCKS)]
                for gl, blk in enumerate(_block_transpose(rows)):
                    ug_ref[LANE_BLOCKS * gh + gl, :, ih * LANES:(ih + 1) * LANES] = blk

    u = mm(OFF_U, OFF_GA)
    for gh in range(D_SSM // LANES):
        u_sc[gh] = u[:, gh * LANES:(gh + 1) * LANES]
    q_slab(0)
    ga_ref[...] = _sigmoid(mm(OFF_GA, OFF_GS)).astype(BF16)
    ssm_input_regroup()
    q_slab(1)
    kv_heads()
    gs_ref[...] = _sigmoid(mm(OFF_GS, D_IN_PROJ)).astype(BF16)


def _rope_tables(t_max):
    t = jnp.arange(t_max, dtype=jnp.int32)
    row_pos = (t // GRID_W).astype(F32)
    col_pos = (t % GRID_W).astype(F32)
    n_freq = HEAD_DIM // 4
    inv_freq = ROPE_THETA ** (-jnp.arange(n_freq, dtype=F32) / n_freq)
    ang = jnp.concatenate([row_pos[:, None] * inv_freq, col_pos[:, None] * inv_freq], axis=-1)
    c, s = jnp.cos(ang), jnp.sin(ang)
    reps = LANES // HEAD_DIM
    cos_t = jnp.tile(jnp.concatenate([c, c], axis=-1), (1, reps))
    sin_t = jnp.tile(jnp.concatenate([-s, s], axis=-1), (1, reps))
    return cos_t, sin_t


def _in_projection(x, t, w_in_bf16, q_norm_g, k_norm_g, tm, tk):
    n = x.shape[0]
    tiles_per_seq = t // tm
    cos_t, sin_t = _rope_tables(t)
    reps = LANES // HEAD_DIM
    gq = jnp.tile(q_norm_g.astype(F32), reps)[None, :]
    gk = jnp.tile(k_norm_g.astype(F32), reps)[None, :]
    head_of_lane = jnp.arange(MXU_DIM) // HEAD_DIM
    seg = (head_of_lane[:, None] == head_of_lane[None, :]).astype(BF16)
    pos_map = lambda i: (i % tiles_per_seq, 0)
    row = lambda i: (i, 0)
    const = lambda i: (0, 0)
    return pl.pallas_call(
        functools.partial(_inproj_body, tk=tk),
        grid=(n // tm,),
        in_specs=[
            pl.BlockSpec((tm, D_MODEL), row),
            pl.BlockSpec((D_MODEL, D_IN_PROJ), const),
            pl.BlockSpec((1, LANES), const),
            pl.BlockSpec((1, LANES), const),
            pl.BlockSpec((tm, LANES), pos_map),
            pl.BlockSpec((tm, LANES), pos_map),
            pl.BlockSpec((MXU_DIM, MXU_DIM), const),
        ],
        out_specs=[
            pl.BlockSpec((N_Q_HEADS, HEAD_DIM, tm), lambda i: (0, 0, i)),
            pl.BlockSpec((N_KV_HEADS, tm, HEAD_DIM), lambda i: (0, i, 0)),
            (pl.BlockSpec((N_KV_HEADS, tm // tk, HEAD_DIM, tk), lambda i: (0, i, 0, 0)) if tk <= tm else
             pl.BlockSpec((N_KV_HEADS, 1, HEAD_DIM, tm), lambda i: (0, i // (tk // tm), 0, i % (tk // tm)))),
            pl.BlockSpec((N_SSM_GROUPS, tm // SSM_CHUNK, CHUNK_W), lambda i: (0, i, 0)),
            pl.BlockSpec((tm, D_MODEL), row),
            pl.BlockSpec((tm, D_MODEL), row),
        ],
        out_shape=[
            jax.ShapeDtypeStruct((N_Q_HEADS, HEAD_DIM, n), BF16),
            jax.ShapeDtypeStruct((N_KV_HEADS, n, HEAD_DIM), BF16),
            jax.ShapeDtypeStruct((N_KV_HEADS, n // tk, HEAD_DIM, tk), BF16),
            jax.ShapeDtypeStruct((N_SSM_GROUPS, n // SSM_CHUNK, CHUNK_W), F32),
            jax.ShapeDtypeStruct((n, D_MODEL), BF16),
            jax.ShapeDtypeStruct((n, D_MODEL), BF16),
        ],
        scratch_shapes=[pltpu.VMEM((D_SSM // LANES, tm, LANES), F32)],
        compiler_params=_cparams(("parallel",)),
        name="in_projection",
    )(x, w_in_bf16, gq, gk, cos_t, sin_t, seg)


def _attn_body(qt_ref, k_ref, vt_ref, o_ref, m_sc, l_sc, acc_sc, s_sc, *, tq, tk, n_qt, n_kv, shift):
    n_blocks = n_qt * n_kv

    def scores(f):
        qo = pl.multiple_of((f // n_kv) * tq, tq)
        ko = pl.multiple_of((f % n_kv) * tk, tk)
        q4t = jnp.concatenate([qt_ref[h, :, pl.ds(qo, tq)] for h in range(Q_PER_KV)], axis=1)
        return _dot(k_ref[0, pl.ds(ko, tk), :], q4t)

    def row_groups(p):
        return jnp.sum(p.reshape(tk // 8, 8, p.shape[1]), axis=0)

    def softmax_pv(s, j):
        if shift:
            m_old = m_sc[...]
            m_new = jnp.maximum(m_old, jnp.max(s, axis=0, keepdims=True))
            alpha = jnp.exp2(m_old - m_new)
            p = jnp.exp2(s - m_new)
            l_sc[...] = alpha * l_sc[...] + row_groups(p)
            acc_sc[...] = alpha * acc_sc[...] + _dot(vt_ref[0, j], p.astype(BF16))
            m_sc[...] = m_new
        else:
            p = jnp.exp2(s)
            l_sc[...] += row_groups(p)
            acc_sc[...] += _dot(vt_ref[0, j], p.astype(BF16))

    s_sc[0] = scores(0)

    def query_tile(qi, carry):
        m_sc[...] = jnp.full(m_sc.shape, -jnp.inf, F32)
        l_sc[...] = jnp.zeros(l_sc.shape, F32)
        acc_sc[...] = jnp.zeros(acc_sc.shape, F32)

        per_trip = math.gcd(KV_BLOCKS_PER_TRIP, n_kv)

        def body(jj, c):
            for u in range(0, per_trip, 2):
                j = per_trip * jj + u
                f = qi * n_kv + j
                s_sc[1] = scores(f + 1)
                softmax_pv(s_sc[0], j)
                s_sc[0] = scores(jnp.minimum(f + 2, n_blocks - 1))
                softmax_pv(s_sc[1], j + 1)
            return c

        lax.fori_loop(0, n_kv // per_trip, body, 0)
        out = acc_sc[...] / jnp.sum(l_sc[...], axis=0, keepdims=True)
        qo = pl.multiple_of(qi * tq, tq)
        for h in range(Q_PER_KV):
            o_ref[h, :, pl.ds(qo, tq)] = out[:, h * tq:(h + 1) * tq].astype(o_ref.dtype)
        return carry

    lax.fori_loop(0, n_qt, query_tile, 0)


def _attention(qt, k_heads, vt_blocks, q_norm_g, k_norm_g, b, t, tq, tk):
    n_qt = t // tq
    n_kv = t // tk
    bound = QK_SCALE * HEAD_DIM * jnp.max(jnp.abs(q_norm_g.astype(F32))) * jnp.max(jnp.abs(k_norm_g.astype(F32)))
    call = functools.partial(_attention_call, b=b, t=t, tq=tq, tk=tk, n_qt=n_qt, n_kv=n_kv)
    return lax.cond(bound < MAX_UNSHIFTED_EXPONENT,
                    functools.partial(call, shift=False), functools.partial(call, shift=True),
                    qt, k_heads, vt_blocks)


def _attention_call(qt, k_heads, vt_blocks, *, b, t, tq, tk, n_qt, n_kv, shift):
    body = functools.partial(_attn_body, tq=tq, tk=tk, n_qt=n_qt, n_kv=n_kv, shift=shift)
    return pl.pallas_call(
        body,
        grid=(b, N_KV_HEADS),
        in_specs=[
            pl.BlockSpec((Q_PER_KV, HEAD_DIM, t), lambda bi, g: (g, 0, bi)),
            pl.BlockSpec((1, t, HEAD_DIM), lambda bi, g: (g, bi, 0)),
            pl.BlockSpec((1, n_kv, HEAD_DIM, tk), lambda bi, g: (g, bi, 0, 0)),
        ],
        out_specs=pl.BlockSpec((Q_PER_KV, HEAD_DIM, t), lambda bi, g: (g, 0, bi)),
        out_shape=jax.ShapeDtypeStruct((N_Q_HEADS, HEAD_DIM, b * t), BF16),
        scratch_shapes=[
            pltpu.VMEM((1, Q_PER_KV * tq), F32),
            pltpu.VMEM((8, Q_PER_KV * tq), F32),
            pltpu.VMEM((HEAD_DIM, Q_PER_KV * tq), F32),
            pltpu.VMEM((2, tk, Q_PER_KV * tq), F32),
        ],
        compiler_params=_cparams(("parallel", "parallel")),
        name="attention",
    )(qt, k_heads, vt_blocks)


def _s5_matrices(a_re, a_im, log_dt, b_re, b_im, c_re, c_im, d_skip):
    hp = lax.Precision.HIGHEST
    L, G, P, H = SSM_CHUNK, N_SSM_GROUPS, SSM_STATE, SSM_GROUP
    f = lambda v: v.astype(F32)
    lr, li = f(a_re), f(a_im)
    dt = jnp.exp(f(log_dt))[..., None]
    mag = jnp.exp(lr * dt)
    ar = mag * jnp.cos(li * dt)
    ai = mag * jnp.sin(li * dt)
    den = lr * lr + li * li
    kr = ((ar - 1.0) * lr + ai * li) / den
    ki = (ai * lr - (ar - 1.0) * li) / den
    br, bi = f(b_re), f(b_im)
    bbr = kr[..., None] * br - ki[..., None] * bi
    bbi = kr[..., None] * bi + ki[..., None] * br
    kk = jnp.arange(L + 1, dtype=F32)[:, None, None, None]
    pmag = jnp.exp(kk * (lr * dt)[None])
    ph = kk * (li * dt)[None]
    pr = pmag * jnp.cos(ph)
    pi = pmag * jnp.sin(ph)
    cr, ci = f(c_re), f(c_im)
    car = cr[None] * pr[:, :, :, None, :] - ci[None] * pi[:, :, :, None, :]
    cai = cr[None] * pi[:, :, :, None, :] + ci[None] * pr[:, :, :, None, :]
    kern = jnp.einsum('kdgoq,dgqi->kdgoi', jnp.concatenate([car, -cai], axis=-1),
                      jnp.concatenate([bbr, bbi], axis=2), precision=hp)
    jj = jnp.arange(L)[:, None]
    ii = jnp.arange(L)[None, :]

    def toeplitz(kd, lag, valid):
        blocks = jnp.where(valid[:, :, None, None, None], kd[jnp.clip(lag, 0, L)], 0.0)
        return blocks.transpose(2, 0, 4, 1, 3).reshape(G, CHUNK_W, CHUNK_W)

    msum = toeplitz(kern[:, 0], ii - jj, ii >= jj) + toeplitz(kern[:, 1], jj - ii, jj >= ii)

    def in_op(d, powers):
        prj, pij = pr[powers, d], pi[powers, d]
        re = prj[..., None] * bbr[d][None] - pij[..., None] * bbi[d][None]
        im = prj[..., None] * bbi[d][None] + pij[..., None] * bbr[d][None]
        to = lambda m: m.transpose(1, 0, 3, 2).reshape(G, CHUNK_W, P)
        return to(re), to(im)

    def out_op(d, powers):
        to = lambda m: m.transpose(1, 3, 0, 2).reshape(G, P, CHUNK_W)
        return to(car[powers, d]), to(-cai[powers, d])

    steps = jnp.arange(L)
    bm = jnp.stack(in_op(0, L - 1 - steps) + in_op(1, steps), axis=1)
    cm = jnp.stack(out_op(0, steps + 1) + out_op(1, L - steps), axis=1)
    al = jnp.stack([pr[L, 0], pi[L, 0], pr[L, 1], pi[L, 1]], axis=1)[:, :, None, :]
    dvec = jnp.tile(f(d_skip).reshape(G, 1, H), (1, 1, L))
    z_in = jnp.zeros((G // 2, 4, CHUNK_W, P), F32)
    z_out = jnp.zeros((G // 2, 4, P, CHUNK_W), F32)
    bm_pair = jnp.concatenate([jnp.concatenate([bm[0::2], z_in], axis=3),
                               jnp.concatenate([z_in, bm[1::2]], axis=3)], axis=2)
    cm_pair = jnp.concatenate([jnp.concatenate([cm[0::2], z_out], axis=3),
                               jnp.concatenate([z_out, cm[1::2]], axis=3)], axis=2)
    al_pair = jnp.concatenate([al[0::2], al[1::2]], axis=3)
    bm_all = jnp.concatenate([bm_pair[:, d] for d in range(4)], axis=2)
    cm_all = jnp.concatenate([cm_pair[:, d] for d in range(4)], axis=1)
    return msum.astype(BF16), bm_all.astype(BF16), cm_all.astype(BF16), al_pair, dvec


def _s5_body(u_ref, m_ref, bm_ref, cm_ref, al_ref, d_ref, y_ref, st_sc, *, nb, nc):
    u0, u1 = u_ref[0], u_ref[1]
    ub = jnp.concatenate([u0.astype(BF16), u1.astype(BF16)], axis=1)
    local = _dot(ub, bm_ref[0])
    for d in range(4):
        st_sc[d] = local[:, d * LANES:(d + 1) * LANES]
    al = [al_ref[0, d] for d in range(4)]

    def advance(rows, d, sr, si):
        lr = st_sc[d, rows, :]
        li = st_sc[d + 1, rows, :]
        st_sc[d, rows, :] = sr
        st_sc[d + 1, rows, :] = si
        ar, ai = al[d], al[d + 1]
        return ar * sr - ai * si + lr, ar * si + ai * sr + li

    def step(k, st):
        fr, fi, br, bi = st
        fr, fi = advance(pl.ds(k, nb, stride=nc), 0, fr, fi)
        br, bi = advance(pl.ds(nc - 1 - k, nb, stride=nc), 2, br, bi)
        return fr, fi, br, bi

    lax.fori_loop(0, nc, step, tuple(jnp.zeros((nb, LANES), F32) for _ in range(4)), unroll=8)

    entering = jnp.concatenate([st_sc[d].astype(BF16) for d in range(4)], axis=1)
    y = _dot(entering, cm_ref[0])
    y_ref[0] = y[:, :CHUNK_W] + _dot(ub[:, :CHUNK_W], m_ref[0]) + u0 * d_ref[0]
    y_ref[1] = y[:, CHUNK_W:] + _dot(ub[:, CHUNK_W:], m_ref[1]) + u1 * d_ref[1]


def _s5_scan(ug, mats, nb, nc):
    msum, bm, cm, al, dvec = mats
    rows = nb * nc
    pair3 = lambda g: (g, 0, 0)
    pair4 = lambda g: (g, 0, 0, 0)
    return pl.pallas_call(
        functools.partial(_s5_body, nb=nb, nc=nc),
        grid=(N_SSM_GROUPS // 2,),
        in_specs=[
            pl.BlockSpec((2, rows, CHUNK_W), pair3),
            pl.BlockSpec((2, CHUNK_W, CHUNK_W), pair3),
            pl.BlockSpec((1, 2 * CHUNK_W, 4 * LANES), pair3),
            pl.BlockSpec((1, 4 * LANES, 2 * CHUNK_W), pair3),
            pl.BlockSpec((1, 4, 1, LANES), pair4),
            pl.BlockSpec((2, 1, CHUNK_W), pair3),
        ],
        out_specs=pl.BlockSpec((2, rows, CHUNK_W), pair3),
        out_shape=jax.ShapeDtypeStruct((N_SSM_GROUPS, rows, CHUNK_W), F32),
        scratch_shapes=[pltpu.VMEM((4, rows, LANES), F32)],
        compiler_params=_cparams(("parallel",)),
        name="s5_scan",
    )(ug, msum, bm, cm, al, dvec)


def _route(logits):
    lane_i = lax.broadcasted_iota(jnp.int32, logits.shape, 1)
    lane = lane_i.astype(F32)
    neg = -jnp.inf
    big = float(LANES)
    is_g = lane_i < N_EXPERT_GROUPS
    gl = jnp.where(is_g, logits, neg)
    gmax = jnp.max(gl, axis=-1, keepdims=True)
    gidx = jnp.min(jnp.where(gl == gmax, lane, big), axis=-1, keepdims=True)
    g_w = 1.0 / jnp.sum(jnp.where(is_g, jnp.exp(logits - gmax), 0.0), axis=-1, keepdims=True)
    e_lane = lane_i - ROUTE_E0
    group_of_lane = (e_lane // EXPERTS_PER_GROUP).astype(F32)
    in_group = (e_lane >= 0) & (e_lane < N_EXPERTS) & (group_of_lane == gidx)
    el = jnp.where(in_group, logits, neg)
    v1 = jnp.max(el, axis=-1, keepdims=True)
    i1 = jnp.min(jnp.where(el == v1, lane, big), axis=-1, keepdims=True)
    el2 = jnp.where(lane == i1, neg, el)
    v2 = jnp.max(el2, axis=-1, keepdims=True)
    i2 = jnp.min(jnp.where(el2 == v2, lane, big), axis=-1, keepdims=True)
    e2 = jnp.exp(v2 - v1)
    inv = g_w / (1.0 + e2)
    combine = jnp.where(lane == i1, inv, 0.0) + jnp.where(lane == i2, e2 * inv, 0.0)
    return combine, (lane == i1) | (lane == i2)


AUX_LANE = LANES - 1


def _slot_tables(cap):
    slot = jnp.arange(N_EXPERTS * cap)
    lane = jnp.arange(LANES)[:, None]
    expand = (lane == ROUTE_E0 + slot // cap).astype(F32)
    expand_aug = jnp.where(lane == AUX_LANE, -(slot % cap).astype(F32), expand)
    return expand.astype(BF16), expand_aug.astype(BF16)


def _merge_body(x_ref, ot_ref, yg_ref, ga_ref, gs_ref, wua_ref, wglu_ref, bglu_ref, wus_ref, wout_ref,
                g1_ref, b1_ref, wr_ref, br_ref, tri_ref, eaugt_ref,
                x1_ref, route_ref, rank_ref, xs_ref, y_sc, *, cap):
    tm = x_ref.shape[0]
    n_chunks = tm // SSM_CHUNK
    for gh in range(D_SSM // LANES):
        for ih in range(SSM_CHUNK // LANE_BLOCKS):
            blocks = [yg_ref[LANE_BLOCKS * gh + gl, :, ih * LANES:(ih + 1) * LANES] for gl in range(LANE_BLOCKS)]
            for a, blk in enumerate(_block_transpose(blocks)):
                y_sc[gh, pl.ds(LANE_BLOCKS * ih + a, n_chunks, stride=SSM_CHUNK), :] = blk

    def token_rows(r0, nr):
        rows = slice(r0, r0 + nr)
        y = jnp.concatenate([y_sc[gh, rows, :] for gh in range(D_SSM // LANES)], axis=1)
        o = ot_ref[:, :, rows].reshape(D_ATTN, nr).astype(F32).T.astype(BF16)
        z = jax.nn.gelu(y)
        z = z * _sigmoid(_dot(z.astype(BF16), wglu_ref[...]) + bglu_ref[...])
        ssm_up = _dot(z.astype(BF16), wus_ref[...])
        att_up = _dot(o, wua_ref[...])
        merged = ga_ref[rows, :].astype(F32) * att_up + gs_ref[rows, :].astype(F32) * ssm_up
        mix = _dot(merged.astype(BF16), wout_ref[...])
        x1 = _layer_norm(ALPHA * x_ref[rows, :] + mix, g1_ref[...], b1_ref[...])
        x1_ref[rows, :] = x1
        xh = x1.astype(BF16)
        combine, sel = _route(_dot(xh, wr_ref[...]) + br_ref[...])
        route_ref[rows, :] = combine
        return xh, jnp.where(sel, 1.0, 0.0)

    nr = tm // MERGE_ROW_BLOCKS
    parts = [token_rows(i * nr, nr) for i in range(MERGE_ROW_BLOCKS)]
    xh = jnp.concatenate([p[0] for p in parts], axis=0)
    sel_f = jnp.concatenate([p[1] for p in parts], axis=0)
    sel = sel_f > 0.0
    rank = _dot(tri_ref[...], sel_f.astype(BF16))
    lane = lax.broadcasted_iota(jnp.int32, rank.shape, 1)
    rank = jnp.where(sel, jnp.minimum(rank, float(cap)), float(cap))
    rank = jnp.where(lane == AUX_LANE, 1.0, rank).astype(BF16)
    rank_ref[...] = rank
    miss = lax.dot_general(eaugt_ref[...], rank, (((1,), (1,)), ((), ())), preferred_element_type=F32)
    gather = jnp.where(miss == 0.0, 1.0, 0.0).astype(BF16)
    xs_ref[0] = _dot(gather, xh).astype(BF16)


def _merge_weights(w, tm, cap):
    n_r = N_EXPERT_GROUPS + N_EXPERTS
    wr = jnp.zeros((D_MODEL, LANES), F32).at[:, :n_r].set(
        jnp.concatenate([w['w_router_group'], w['w_router_expert']], axis=1).astype(F32))
    br = jnp.zeros((1, LANES), F32).at[0, :n_r].set(
        jnp.concatenate([w['b_router_group'], w['b_router_expert']]).astype(F32))
    tri =(jnp.arange(tm)[:, None] > jnp.arange(tm)[None, :]).astype(BF16)
    _, expand_aug = _slot_tables(cap)
    return [w['w_up_attn'].astype(BF16), w['w_glu'].astype(BF16), w['b_glu'].astype(F32)[None, :],
            w['w_up_ssm'].astype(BF16), w['w_out'].astype(BF16), w['ln1_g'].astype(F32)[None, :],
            w['ln1_b'].astype(F32)[None, :], wr.astype(BF16), br, tri, expand_aug.T]


def _merge(x, ot, yg, ga, gs, weights, tm, cap):
    n = x.shape[0]
    n_slot = N_EXPERTS * cap
    row = lambda i: (i, 0)
    const = lambda i: (0, 0)
    full = lambda a: pl.BlockSpec(a.shape, const)
    return pl.pallas_call(
        functools.partial(_merge_body, cap=cap),
        grid=(n // tm,),
        in_specs=[pl.BlockSpec((tm, D_MODEL), row),
                  pl.BlockSpec((N_Q_HEADS, HEAD_DIM, tm), lambda i: (0, 0, i)),
                  pl.BlockSpec((N_SSM_GROUPS, tm // SSM_CHUNK, CHUNK_W), lambda i: (0, i, 0)),
                  pl.BlockSpec((tm, D_MODEL), row), pl.BlockSpec((tm, D_MODEL), row)] + [full(a) for a in weights],
        out_specs=[pl.BlockSpec((tm, D_MODEL), row), pl.BlockSpec((tm, LANES), row), pl.BlockSpec((tm, LANES), row),
                   pl.BlockSpec((1, n_slot, D_MODEL), lambda i: (i, 0, 0))],
        out_shape=[jax.ShapeDtypeStruct((n, D_MODEL), F32), jax.ShapeDtypeStruct((n, LANES), F32),
                   jax.ShapeDtypeStruct((n, LANES), BF16), jax.ShapeDtypeStruct((n // tm, n_slot, D_MODEL), BF16)],
        scratch_shapes=[pltpu.VMEM((D_SSM // LANES, tm, LANES), F32)],
        compiler_params=_cparams(("parallel",)),
        name="merge_ln1_router",
    )(x, ot, yg, ga, gs, *weights)


def _expert_mlp(xb, wg, wu, wd):
    h = jax.nn.silu(_dot(xb, wg)) * _dot(xb, wu)
    return _dot(h.astype(BF16), wd)


def _experts_body(xs_ref, wg_ref, wu_ref, wd_ref, ys_ref, wg_sc, wu_sc, wd_sc):
    @pl.when(pl.program_id(1) == 0)
    def _():
        wg_sc[...] = wg_ref[0].astype(BF16)
        wu_sc[...] = wu_ref[0].astype(BF16)
        wd_sc[...] = wd_ref[0].astype(BF16)

    shp = xs_ref.shape
    xb = xs_ref[...].reshape(shp[0] * shp[2], D_MODEL)
    ys_ref[...] = _expert_mlp(xb, wg_sc[...], wu_sc[...], wd_sc[...]).astype(BF16).reshape(shp)


def _experts(xs, w_gate, w_up, w_down, cap, tiles_per_step):
    n_tiles = xs.shape[0]
    xs4 = xs.reshape(n_tiles, N_EXPERTS, cap, D_MODEL)
    slots = pl.BlockSpec((tiles_per_step, 1, cap, D_MODEL), lambda e, t: (t, e, 0, 0))
    exp3 = lambda e, t: (e, 0, 0)
    ys = pl.pallas_call(
        _experts_body,
        grid=(N_EXPERTS, n_tiles // tiles_per_step),
        in_specs=[slots, pl.BlockSpec((1, D_MODEL, D_FF_EXPERT), exp3), pl.BlockSpec((1, D_MODEL, D_FF_EXPERT), exp3),
                  pl.BlockSpec((1, D_FF_EXPERT, D_MODEL), exp3)],
        out_specs=slots,
        out_shape=jax.ShapeDtypeStruct(xs4.shape, BF16),
        scratch_shapes=[pltpu.VMEM((D_MODEL, D_FF_EXPERT), BF16), pltpu.VMEM((D_MODEL, D_FF_EXPERT), BF16),
                        pltpu.VMEM((D_FF_EXPERT, D_MODEL), BF16)],
        compiler_params=_cparams(("parallel", "arbitrary")),
        name="experts",
    )(xs4, w_gate, w_up, w_down)
    return ys.reshape(xs.shape)


def _combine_body(x1_ref, route_ref, rank_ref, ys_ref, eaug_ref, e_ref, g2_ref, b2_ref, wg_hbm, wu_hbm, wd_hbm,
                  o_ref, acc_sc, wg_sc, wu_sc, wd_sc, *, cap):
    route = route_ref[...]
    rank = rank_ref[...]
    hit = _dot(rank, eaug_ref[...]) == 0.0
    spread = _dot(route.astype(BF16), e_ref[...])
    moe = _dot(jnp.where(hit, spread, 0.0).astype(BF16), ys_ref[0])
    o_ref[...] = _layer_norm(ALPHA * x1_ref[...] + moe, g2_ref[...], b2_ref[...])
    lane = lax.broadcasted_iota(jnp.int32, route.shape, 1)
    spilled = jnp.where((rank.astype(F32) == float(cap)) & (lane != AUX_LANE), route, 0.0)

    @pl.when(jnp.max(jnp.abs(spilled)) > 0.0)
    def _():
        acc_sc[...] = moe
        xb = x1_ref[...].astype(BF16)

        def one_expert(e, carry):
            pltpu.sync_copy(wg_hbm.at[e], wg_sc)
            pltpu.sync_copy(wu_hbm.at[e], wu_sc)
            pltpu.sync_copy(wd_hbm.at[e], wd_sc)
            w_e = jnp.sum(jnp.where(lane == e + ROUTE_E0, spilled, 0.0), axis=-1, keepdims=True)
            acc_sc[...] += w_e * _expert_mlp(xb, wg_sc[...].astype(BF16), wu_sc[...].astype(BF16),
                                             wd_sc[...].astype(BF16))
            return carry

        lax.fori_loop(0, N_EXPERTS, one_expert, 0)
        o_ref[...] = _layer_norm(ALPHA * x1_ref[...] + acc_sc[...], g2_ref[...], b2_ref[...])


def _combine(x1, route, rank, ys, w_gate, w_up, w_down, ln2_g, ln2_b, tm, cap):
    n = x1.shape[0]
    n_slot = N_EXPERTS * cap
    expand, expand_aug = _slot_tables(cap)
    row = lambda i: (i, 0)
    const = lambda i: (0, 0)
    hbm = pl.BlockSpec(memory_space=pl.ANY)
    return pl.pallas_call(
        functools.partial(_combine_body, cap=cap),
        grid=(n // tm,),
        in_specs=[pl.BlockSpec((tm, D_MODEL), row), pl.BlockSpec((tm, LANES), row), pl.BlockSpec((tm, LANES), row),
                  pl.BlockSpec((1, n_slot, D_MODEL), lambda i: (i, 0, 0)),
                  pl.BlockSpec((LANES, n_slot), const), pl.BlockSpec((LANES, n_slot), const),
                  pl.BlockSpec((1, D_MODEL), const), pl.BlockSpec((1, D_MODEL), const), hbm, hbm, hbm],
        out_specs=pl.BlockSpec((tm, D_MODEL), row),
        out_shape=jax.ShapeDtypeStruct((n, D_MODEL), F32),
        scratch_shapes=[pltpu.VMEM((tm, D_MODEL), F32), pltpu.VMEM((D_MODEL, D_FF_EXPERT), w_gate.dtype),
                        pltpu.VMEM((D_MODEL, D_FF_EXPERT), w_up.dtype), pltpu.VMEM((D_FF_EXPERT, D_MODEL), w_down.dtype)],
        compiler_params=_cparams(("parallel",)),
        name="combine_ln2",
    )(x1, route, rank, ys, expand_aug, expand, ln2_g.astype(F32)[None, :], ln2_b.astype(F32)[None, :],
      w_gate, w_up, w_down)


def _encode(x, w, prep, cap):
    b, t, _ = x.shape
    n = b * t
    tm, tq, tk = math.gcd(512, t), math.gcd(256, t), math.gcd(1024, t // 8)
    x2 = x.reshape(n, D_MODEL).astype(F32)
    qt, kh, vt, ug, ga, gs = _in_projection(x2, t, prep['w_in'], w['q_norm_g'], w['k_norm_g'], tm, tk)
    ot = _attention(qt, kh, vt, w['q_norm_g'], w['k_norm_g'], b, t, tq, tk)
    yg = _s5_scan(ug, prep['s5'], b, t // SSM_CHUNK)
    x1, route, rank, xs = _merge(x2, ot, yg, ga, gs, _merge_weights(w, tm, cap), tm, cap)
    wg, wu, wd = prep['experts']
    ys = _experts(xs, wg, wu, wd, cap, math.gcd(8, n // tm))
    out = _combine(x1, route, rank, ys, wg, wu, wd, w['ln2_g'], w['ln2_b'], tm, cap)
    return out.reshape(b, t, D_MODEL)


def kernel(x_prompt, x_sample, w_in, q_norm_g, k_norm_g, ssm_a_re, ssm_a_im, ssm_log_dt, ssm_b_re, ssm_b_im, ssm_c_re, ssm_c_im, ssm_d, w_glu, b_glu, w_up_attn, w_up_ssm, w_out, ln1_g, ln1_b, w_router_group, b_router_group, w_router_expert, b_router_expert, w_exp_gate, w_exp_up, w_exp_down, ln2_g, ln2_b):
    w = dict(w_in=w_in, q_norm_g=q_norm_g, k_norm_g=k_norm_g, ssm_a_re=ssm_a_re, ssm_a_im=ssm_a_im,
             ssm_log_dt=ssm_log_dt, ssm_b_re=ssm_b_re, ssm_b_im=ssm_b_im, ssm_c_re=ssm_c_re, ssm_c_im=ssm_c_im,
             ssm_d=ssm_d, w_glu=w_glu, b_glu=b_glu, w_up_attn=w_up_attn, w_up_ssm=w_up_ssm, w_out=w_out,
             ln1_g=ln1_g, ln1_b=ln1_b, w_router_group=w_router_group, b_router_group=b_router_group,
             w_router_expert=w_router_expert, b_router_expert=b_router_expert, w_exp_gate=w_exp_gate,
             w_exp_up=w_exp_up, w_exp_down=w_exp_down, ln2_g=ln2_g, ln2_b=ln2_b)
    w = {name: val[0] for name, val in w.items()}
    prep = dict(
        w_in=w['w_in'].astype(BF16),
        s5=_s5_matrices(w['ssm_a_re'], w['ssm_a_im'], w['ssm_log_dt'], w['ssm_b_re'], w['ssm_b_im'],
                        w['ssm_c_re'], w['ssm_c_im'], w['ssm_d']),
        experts=tuple(w[name] for name in ('w_exp_gate', 'w_exp_up', 'w_exp_down')))
    return _encode(x_prompt, w, prep, EXPERT_CAP), _encode(x_sample, w, prep, EXPERT_CAP)
```

```python
import functools
import math

import jax
import jax.numpy as jnp
from jax import lax
from jax.experimental import pallas as pl
from jax.experimental.pallas import tpu as pltpu

F32 = jnp.float32
BF16 = jnp.bfloat16

D_MODEL = 1024
GRID_W = 64
HEAD_DIM = 64
N_Q_HEADS = 8
N_KV_HEADS = 2
Q_PER_KV = N_Q_HEADS // N_KV_HEADS
D_ATTN = N_Q_HEADS * HEAD_DIM
D_KV = N_KV_HEADS * HEAD_DIM
ROPE_THETA = 10000.0
D_SSM = D_MODEL // 2
SSM_GROUP = 16
N_SSM_GROUPS = D_SSM // SSM_GROUP
SSM_STATE = 64
N_EXPERT_GROUPS = 4
EXPERTS_PER_GROUP = 4
N_EXPERTS = N_EXPERT_GROUPS * EXPERTS_PER_GROUP
D_FF_EXPERT = D_MODEL // 2
DEPTH = 1
ALPHA = (2.0 * DEPTH) ** 0.25
EPS = 1e-6
D_IN_PROJ = D_ATTN + 2 * D_KV + D_SSM + 2 * D_MODEL
OFF_K = D_ATTN
OFF_V = OFF_K + D_KV
OFF_U = OFF_V + D_KV
OFF_GA = OFF_U + D_SSM
OFF_GS = OFF_GA + D_MODEL

LANES = 128
MXU_DIM = 256
SSM_CHUNK = 16
CHUNK_W = SSM_CHUNK * SSM_GROUP
ROUTE_E0 = N_EXPERT_GROUPS
VMEM_LIMIT = 56 * 1024 * 1024
EXPERT_CAP = 96


def _cparams(sem):
    return pltpu.CompilerParams(dimension_semantics=sem, vmem_limit_bytes=VMEM_LIMIT)


def _dot(a, b):
    return jnp.dot(a, b, preferred_element_type=F32)


def _split_bf16(x):
    hi = x.astype(BF16)
    lo = (x - hi.astype(F32)).astype(BF16)
    return hi, lo


def _sigmoid(x):
    return 0.5 * jnp.tanh(0.5 * x) + 0.5


def _layer_norm(h, g, b):
    mu = jnp.mean(h, axis=-1, keepdims=True)
    var = jnp.mean(jnp.square(h - mu), axis=-1, keepdims=True)
    return (h - mu) * lax.rsqrt(var + EPS) * g + b


QK_SCALE = HEAD_DIM ** -0.5 * math.log2(math.e)
MERGE_ROW_BLOCKS = 2
KV_BLOCKS_PER_TRIP = 8
MAX_UNSHIFTED_EXPONENT = 40.0
LANE_BLOCKS = LANES // SSM_GROUP


def _block_transpose(blocks):
    lane = lax.broadcasted_iota(jnp.int32, blocks[0].shape, 1)
    v = list(blocks)
    for s in range(3):
        d = SSM_GROUP << s
        keep = ((lane // d) % 2) == 0
        nxt = list(v)
        for a in range(LANE_BLOCKS):
            if not a & (1 << s):
                a2 = a | (1 << s)
                nxt[a] = jnp.where(keep, v[a], pltpu.roll(v[a2], d, 1))
                nxt[a2] = jnp.where(keep, pltpu.roll(v[a], LANES - d, 1), v[a2])
        v = nxt
    return v


def _inproj_body(x_ref, w_ref, gq_ref, gk_ref, cos_ref, sin_ref, seg_ref,
                 qt_ref, kh_ref, vt_ref, ug_ref, ga_ref, gs_ref, u_sc, *, tk):
    xb = x_ref[...].astype(BF16)
    tm = xb.shape[0]
    cos = cos_ref[...]
    sin = sin_ref[...]
    seg = seg_ref[...]
    lane = lax.broadcasted_iota(jnp.int32, (1, LANES), 1)
    first_half = (lane % HEAD_DIM) < (HEAD_DIM // 2)

    def mm(a, b):
        return _dot(xb, w_ref[:, a:b])

    def head_sumsq(y):
        w = y.shape[1]
        return _dot((y * y).astype(BF16), seg[0:w, 0:w])

    def norm_rope(y, ss, g):
        yn = y * lax.rsqrt(ss * (1.0 / HEAD_DIM) + EPS) * g
        sw = jnp.where(first_half, pltpu.roll(yn, LANES - HEAD_DIM // 2, 1), pltpu.roll(yn, HEAD_DIM // 2, 1))
        return yn * cos + sw * sin

    gq = gq_ref[...]
    heads_per_slab = LANES // HEAD_DIM
    wide = seg.shape[0]

    def q_slab(j):
        y = mm(j * wide, (j + 1) * wide)
        ss = head_sumsq(y)
        for jj in range(wide // LANES):
            q = norm_rope(y[:, jj * LANES:(jj + 1) * LANES], ss[:, jj * LANES:(jj + 1) * LANES], gq)
            qt = (q * QK_SCALE).T
            for h in range(heads_per_slab):
                head = (j * (wide // LANES) + jj) * heads_per_slab + h
                qt_ref[head] = qt[h * HEAD_DIM:(h + 1) * HEAD_DIM].astype(BF16)

    def kv_heads():
        yk = mm(OFF_K, OFF_V)
        k = norm_rope(yk, head_sumsq(yk), gk_ref[...])
        vt = mm(OFF_V, OFF_U).astype(BF16).astype(F32).T
        for g in range(N_KV_HEADS):
            kh_ref[g] = k[:, g * HEAD_DIM:(g + 1) * HEAD_DIM].astype(BF16)
            w = vt_ref.shape[3]
            for jb in range(tm // w):
                vt_ref[g, jb] = vt[g * HEAD_DIM:(g + 1) * HEAD_DIM, jb * w:(jb + 1) * w].astype(BF16)

    def ssm_input_regroup():
        n_chunks = tm // SSM_CHUNK
        for gh in range(D_SSM // LANES):
            for ih in range(SSM_CHUNK // LANE_BLOCKS):
                rows = [u_sc[gh, pl.ds(LANE_BLOCKS * ih + a, n_chunks, stride=SSM_CHUNK), :]
                        for a in range(LANE_BLOCKS)]
                for gl, blk in enumerate(_block_transpose(rows)):
                    ug_ref[LANE_BLOCKS * gh + gl, :, ih * LANES:(ih + 1) * LANES] = blk

    u = mm(OFF_U, OFF_GA)
    for gh in range(D_SSM // LANES):
        u_sc[gh] = u[:, gh * LANES:(gh + 1) * LANES]
    q_slab(0)
    ga_ref[...] = _sigmoid(mm(OFF_GA, OFF_GS)).astype(BF16)
    ssm_input_regroup()
    q_slab(1)
    kv_heads()
    gs_ref[...] = _sigmoid(mm(OFF_GS, D_IN_PROJ)).astype(BF16)


def _rope_tables(t_max):
    t = jnp.arange(t_max, dtype=jnp.int32)
    row_pos = (t // GRID_W).astype(F32)
    col_pos = (t % GRID_W).astype(F32)
    n_freq = HEAD_DIM // 4
    inv_freq = ROPE_THETA ** (-jnp.arange(n_freq, dtype=F32) / n_freq)
    ang = jnp.concatenate([row_pos[:, None] * inv_freq, col_pos[:, None] * inv_freq], axis=-1)
    c, s = jnp.cos(ang), jnp.sin(ang)
    reps = LANES // HEAD_DIM
    cos_t = jnp.tile(jnp.concatenate([c, c], axis=-1), (1, reps))
    sin_t = jnp.tile(jnp.concatenate([-s, s], axis=-1), (1, reps))
    return cos_t, sin_t


def _in_projection(x, t, w_in_bf16, q_norm_g, k_norm_g, tm, tk):
    n = x.shape[0]
    tiles_per_seq = t // tm
    cos_t, sin_t = _rope_tables(t)
    reps = LANES // HEAD_DIM
    gq = jnp.tile(q_norm_g.astype(F32), reps)[None, :]
    gk = jnp.tile(k_norm_g.astype(F32), reps)[None, :]
    head_of_lane = jnp.arange(MXU_DIM) // HEAD_DIM
    seg = (head_of_lane[:, None] == head_of_lane[None, :]).astype(BF16)
    pos_map = lambda i: (i % tiles_per_seq, 0)
    row = lambda i: (i, 0)
    const = lambda i: (0, 0)
    return pl.pallas_call(
        functools.partial(_inproj_body, tk=tk),
        grid=(n // tm,),
        in_specs=[
            pl.BlockSpec((tm, D_MODEL), row),
            pl.BlockSpec((D_MODEL, D_IN_PROJ), const),
            pl.BlockSpec((1, LANES), const),
            pl.BlockSpec((1, LANES), const),
            pl.BlockSpec((tm, LANES), pos_map),
            pl.BlockSpec((tm, LANES), pos_map),
            pl.BlockSpec((MXU_DIM, MXU_DIM), const),
        ],
        out_specs=[
            pl.BlockSpec((N_Q_HEADS, HEAD_DIM, tm), lambda i: (0, 0, i)),
            pl.BlockSpec((N_KV_HEADS, tm, HEAD_DIM), lambda i: (0, i, 0)),
            (pl.BlockSpec((N_KV_HEADS, tm // tk, HEAD_DIM, tk), lambda i: (0, i, 0, 0)) if tk <= tm else
             pl.BlockSpec((N_KV_HEADS, 1, HEAD_DIM, tm), lambda i: (0, i // (tk // tm), 0, i % (tk // tm)))),
            pl.BlockSpec((N_SSM_GROUPS, tm // SSM_CHUNK, CHUNK_W), lambda i: (0, i, 0)),
            pl.BlockSpec((tm, D_MODEL), row),
            pl.BlockSpec((tm, D_MODEL), row),
        ],
        out_shape=[
            jax.ShapeDtypeStruct((N_Q_HEADS, HEAD_DIM, n), BF16),
            jax.ShapeDtypeStruct((N_KV_HEADS, n, HEAD_DIM), BF16),
            jax.ShapeDtypeStruct((N_KV_HEADS, n // tk, HEAD_DIM, tk), BF16),
            jax.ShapeDtypeStruct((N_SSM_GROUPS, n // SSM_CHUNK, CHUNK_W), F32),
            jax.ShapeDtypeStruct((n, D_MODEL), BF16),
            jax.ShapeDtypeStruct((n, D_MODEL), BF16),
        ],
        scratch_shapes=[pltpu.VMEM((D_SSM // LANES, tm, LANES), F32)],
        compiler_params=_cparams(("parallel",)),
        name="in_projection",
    )(x, w_in_bf16, gq, gk, cos_t, sin_t, seg)


def _attn_body(qt_ref, k_ref, vt_ref, o_ref, m_sc, l_sc, acc_sc, s_sc, *, tq, tk, n_qt, n_kv, shift):
    n_blocks = n_qt * n_kv

    def scores(f):
        qo = pl.multiple_of((f // n_kv) * tq, tq)
        ko = pl.multiple_of((f % n_kv) * tk, tk)
        q4t = jnp.concatenate([qt_ref[h, :, pl.ds(qo, tq)] for h in range(Q_PER_KV)], axis=1)
        return _dot(k_ref[0, pl.ds(ko, tk), :], q4t)

    def row_groups(p):
        return jnp.sum(p.reshape(tk // 8, 8, p.shape[1]), axis=0)

    def softmax_pv(s, j):
        if shift:
            m_old = m_sc[...]
            m_new = jnp.maximum(m_old, jnp.max(s, axis=0, keepdims=True))
            alpha = jnp.exp2(m_old - m_new)
            p = jnp.exp2(s - m_new)
            l_sc[...] = alpha * l_sc[...] + row_groups(p)
            acc_sc[...] = alpha * acc_sc[...] + _dot(vt_ref[0, j], p.astype(BF16))
            m_sc[...] = m_new
        else:
            p = jnp.exp2(s)
            l_sc[...] += row_groups(p)
            acc_sc[...] += _dot(vt_ref[0, j], p.astype(BF16))

    s_sc[0] = scores(0)

    def query_tile(qi, carry):
        m_sc[...] = jnp.full(m_sc.shape, -jnp.inf, F32)
        l_sc[...] = jnp.zeros(l_sc.shape, F32)
        acc_sc[...] = jnp.zeros(acc_sc.shape, F32)

        per_trip = math.gcd(KV_BLOCKS_PER_TRIP, n_kv)

        def body(jj, c):
            for u in range(0, per_trip, 2):
                j = per_trip * jj + u
                f = qi * n_kv + j
                s_sc[1] = scores(f + 1)
                softmax_pv(s_sc[0], j)
                s_sc[0] = scores(jnp.minimum(f + 2, n_blocks - 1))
                softmax_pv(s_sc[1], j + 1)
            return c

        lax.fori_loop(0, n_kv // per_trip, body, 0)
        out = acc_sc[...] / jnp.sum(l_sc[...], axis=0, keepdims=True)
        qo = pl.multiple_of(qi * tq, tq)
        for h in range(Q_PER_KV):
            o_ref[h, :, pl.ds(qo, tq)] = out[:, h * tq:(h + 1) * tq].astype(o_ref.dtype)
        return carry

    lax.fori_loop(0, n_qt, query_tile, 0)


def _attention(qt, k_heads, vt_blocks, q_norm_g, k_norm_g, b, t, tq, tk):
    n_qt = t // tq
    n_kv = t // tk
    bound = QK_SCALE * HEAD_DIM * jnp.max(jnp.abs(q_norm_g.astype(F32))) * jnp.max(jnp.abs(k_norm_g.astype(F32)))
    call = functools.partial(_attention_call, b=b, t=t, tq=tq, tk=tk, n_qt=n_qt, n_kv=n_kv)
    return lax.cond(bound < MAX_UNSHIFTED_EXPONENT,
                    functools.partial(call, shift=False), functools.partial(call, shift=True),
                    qt, k_heads, vt_blocks)


def _attention_call(qt, k_heads, vt_blocks, *, b, t, tq, tk, n_qt, n_kv, shift):
    body = functools.partial(_attn_body, tq=tq, tk=tk, n_qt=n_qt, n_kv=n_kv, shift=shift)
    return pl.pallas_call(
        body,
        grid=(b, N_KV_HEADS),
        in_specs=[
            pl.BlockSpec((Q_PER_KV, HEAD_DIM, t), lambda bi, g: (g, 0, bi)),
            pl.BlockSpec((1, t, HEAD_DIM), lambda bi, g: (g, bi, 0)),
            pl.BlockSpec((1, n_kv, HEAD_DIM, tk), lambda bi, g: (g, bi, 0, 0)),
        ],
        out_specs=pl.BlockSpec((Q_PER_KV, HEAD_DIM, t), lambda bi, g: (g, 0, bi)),
        out_shape=jax.ShapeDtypeStruct((N_Q_HEADS, HEAD_DIM, b * t), BF16),
        scratch_shapes=[
            pltpu.VMEM((1, Q_PER_KV * tq), F32),
            pltpu.VMEM((8, Q_PER_KV * tq), F32),
            pltpu.VMEM((HEAD_DIM, Q_PER_KV * tq), F32),
            pltpu.VMEM((2, tk, Q_PER_KV * tq), F32),
        ],
        compiler_params=_cparams(("parallel", "parallel")),
        name="attention",
    )(qt, k_heads, vt_blocks)


def _s5_matrices(a_re, a_im, log_dt, b_re, b_im, c_re, c_im, d_skip):
    L, G, P, H = SSM_CHUNK, N_SSM_GROUPS, SSM_STATE, SSM_GROUP
    f = lambda v: v.astype(F32)
    lr, li = f(a_re), f(a_im)
    dt = jnp.exp(f(log_dt))[..., None]
    mag = jnp.exp(lr * dt)
    ar = mag * jnp.cos(li * dt)
    ai = mag * jnp.sin(li * dt)
    den = lr * lr + li * li
    kr = ((ar - 1.0) * lr + ai * li) / den
    ki = (ai * lr - (ar - 1.0) * li) / den
    br, bi = f(b_re), f(b_im)
    bbr = kr[..., None] * br - ki[..., None] * bi
    bbi = kr[..., None] * bi + ki[..., None] * br
    kk = jnp.arange(L + 1, dtype=F32)[:, None, None, None]
    pmag = jnp.exp(kk * (lr * dt)[None])
    ph = kk * (li * dt)[None]
    pr = pmag * jnp.cos(ph)
    pi = pmag * jnp.sin(ph)
    cr, ci = f(c_re), f(c_im)
    car = cr[None] * pr[:, :, :, None, :] - ci[None] * pi[:, :, :, None, :]
    cai = cr[None] * pi[:, :, :, None, :] + ci[None] * pr[:, :, :, None, :]
    bbr_t = bbr.transpose(0, 1, 3, 2)[None, :, :, None, :, :]
    bbi_t = bbi.transpose(0, 1, 3, 2)[None, :, :, None, :, :]
    kern = jnp.sum(car[:L, :, :, :, None, :] * bbr_t - cai[:L, :, :, :, None, :] * bbi_t, axis=-1)

    def lower_blocks(kd):
        return jnp.stack([jnp.concatenate([jnp.zeros((j,) + kd.shape[1:], F32), kd[:L - j]], axis=0)
                          for j in range(L)], axis=0)

    blocks = lower_blocks(kern[:, 0]) + lower_blocks(kern[:, 1]).swapaxes(0, 1)
    msum = blocks.transpose(2, 0, 4, 1, 3).reshape(G, CHUNK_W, CHUNK_W)

    def in_op(d, powers):
        prj, pij = pr[powers, d], pi[powers, d]
        re = prj[..., None] * bbr[d][None] - pij[..., None] * bbi[d][None]
        im = prj[..., None] * bbi[d][None] + pij[..., None] * bbr[d][None]
        to = lambda m: m.transpose(1, 0, 3, 2).reshape(G, CHUNK_W, P)
        return to(re), to(im)

    def out_op(d, powers):
        to = lambda m: m.transpose(1, 3, 0, 2).reshape(G, P, CHUNK_W)
        return to(car[powers, d]), to(-cai[powers, d])

    steps = jnp.arange(L)
    bm = jnp.stack(in_op(0, L - 1 - steps) + in_op(1, steps), axis=1)
    cm = jnp.stack(out_op(0, steps + 1) + out_op(1, L - steps), axis=1)
    al = jnp.stack([pr[L, 0], pi[L, 0], pr[L, 1], pi[L, 1]], axis=1)[:, :, None, :]
    dvec = jnp.tile(f(d_skip).reshape(G, 1, H), (1, 1, L))
    z_in = jnp.zeros((G // 2, 4, CHUNK_W, P), F32)
    z_out = jnp.zeros((G // 2, 4, P, CHUNK_W), F32)
    bm_pair = jnp.concatenate([jnp.concatenate([bm[0::2], z_in], axis=3),
                               jnp.concatenate([z_in, bm[1::2]], axis=3)], axis=2)
    cm_pair = jnp.concatenate([jnp.concatenate([cm[0::2], z_out], axis=3),
                               jnp.concatenate([z_out, cm[1::2]], axis=3)], axis=2)
    al_pair = jnp.concatenate([al[0::2], al[1::2]], axis=3)
    bm_all = jnp.concatenate([bm_pair[:, d] for d in range(4)], axis=2)
    cm_all = jnp.concatenate([cm_pair[:, d] for d in range(4)], axis=1)
    return msum.astype(BF16), bm_all.astype(BF16), cm_all.astype(BF16), al_pair, dvec


def _s5_body(u_ref, m_ref, bm_ref, cm_ref, al_ref, d_ref, y_ref, st_sc, *, nb, nc):
    u0, u1 = u_ref[0], u_ref[1]
    ub = jnp.concatenate([u0.astype(BF16), u1.astype(BF16)], axis=1)
    local = _dot(ub, bm_ref[0])
    for d in range(4):
        st_sc[d] = local[:, d * LANES:(d + 1) * LANES]
    al = [al_ref[0, d] for d in range(4)]

    def advance(rows, d, sr, si):
        lr = st_sc[d, rows, :]
        li = st_sc[d + 1, rows, :]
        st_sc[d, rows, :] = sr
        st_sc[d + 1, rows, :] = si
        ar, ai = al[d], al[d + 1]
        return ar * sr - ai * si + lr, ar * si + ai * sr + li

    def step(k, st):
        fr, fi, br, bi = st
        fr, fi = advance(pl.ds(k, nb, stride=nc), 0, fr, fi)
        br, bi = advance(pl.ds(nc - 1 - k, nb, stride=nc), 2, br, bi)
        return fr, fi, br, bi

    lax.fori_loop(0, nc, step, tuple(jnp.zeros((nb, LANES), F32) for _ in range(4)), unroll=8)

    entering = jnp.concatenate([st_sc[d].astype(BF16) for d in range(4)], axis=1)
    y = _dot(entering, cm_ref[0])
    y_ref[0] = y[:, :CHUNK_W] + _dot(ub[:, :CHUNK_W], m_ref[0]) + u0 * d_ref[0]
    y_ref[1] = y[:, CHUNK_W:] + _dot(ub[:, CHUNK_W:], m_ref[1]) + u1 * d_ref[1]


def _s5_scan(ug, mats, nb, nc):
    msum, bm, cm, al, dvec = mats
    rows = nb * nc
    pair3 = lambda g: (g, 0, 0)
    pair4 = lambda g: (g, 0, 0, 0)
    return pl.pallas_call(
        functools.partial(_s5_body, nb=nb, nc=nc),
        grid=(N_SSM_GROUPS // 2,),
        in_specs=[
            pl.BlockSpec((2, rows, CHUNK_W), pair3),
            pl.BlockSpec((2, CHUNK_W, CHUNK_W), pair3),
            pl.BlockSpec((1, 2 * CHUNK_W, 4 * LANES), pair3),
            pl.BlockSpec((1, 4 * LANES, 2 * CHUNK_W), pair3),
            pl.BlockSpec((1, 4, 1, LANES), pair4),
            pl.BlockSpec((2, 1, CHUNK_W), pair3),
        ],
        out_specs=pl.BlockSpec((2, rows, CHUNK_W), pair3),
        out_shape=jax.ShapeDtypeStruct((N_SSM_GROUPS, rows, CHUNK_W), F32),
        scratch_shapes=[pltpu.VMEM((4, rows, LANES), F32)],
        compiler_params=_cparams(("parallel",)),
        name="s5_scan",
    )(ug, msum, bm, cm, al, dvec)


def _route(logits):
    lane_i = lax.broadcasted_iota(jnp.int32, logits.shape, 1)
    lane = lane_i.astype(F32)
    neg = -jnp.inf
    big = float(LANES)
    is_g = lane_i < N_EXPERT_GROUPS
    gl = jnp.where(is_g, logits, neg)
    gmax = jnp.max(gl, axis=-1, keepdims=True)
    gidx = jnp.min(jnp.where(gl == gmax, lane, big), axis=-1, keepdims=True)
    g_w = 1.0 / jnp.sum(jnp.where(is_g, jnp.exp(logits - gmax), 0.0), axis=-1, keepdims=True)
    e_lane = lane_i - ROUTE_E0
    group_of_lane = (e_lane // EXPERTS_PER_GROUP).astype(F32)
    in_group = (e_lane >= 0) & (e_lane < N_EXPERTS) & (group_of_lane == gidx)
    el = jnp.where(in_group, logits, neg)
    v1 = jnp.max(el, axis=-1, keepdims=True)
    i1 = jnp.min(jnp.where(el == v1, lane, big), axis=-1, keepdims=True)
    el2 = jnp.where(lane == i1, neg, el)
    v2 = jnp.max(el2, axis=-1, keepdims=True)
    i2 = jnp.min(jnp.where(el2 == v2, lane, big), axis=-1, keepdims=True)
    e2 = jnp.exp(v2 - v1)
    inv = g_w / (1.0 + e2)
    combine = jnp.where(lane == i1, inv, 0.0) + jnp.where(lane == i2, e2 * inv, 0.0)
    return combine, (lane == i1) | (lane == i2)


AUX_LANE = LANES - 1


def _slot_tables(cap):
    slot = jnp.arange(N_EXPERTS * cap)
    lane = jnp.arange(LANES)[:, None]
    expand = (lane == ROUTE_E0 + slot // cap).astype(F32)
    expand_aug = jnp.where(lane == AUX_LANE, -(slot % cap).astype(F32), expand)
    return expand.astype(BF16), expand_aug.astype(BF16)


def _merge_body(x_ref, ot_ref, yg_ref, ga_ref, gs_ref, wua_ref, wglu_ref, bglu_ref, wus_ref, wout_ref,
                g1_ref, b1_ref, wr_ref, br_ref, tri_ref, eaugt_ref,
                x1_ref, route_ref, rank_ref, xs_ref, y_sc, *, cap):
    tm = x_ref.shape[0]
    n_chunks = tm // SSM_CHUNK
    for gh in range(D_SSM // LANES):
        for ih in range(SSM_CHUNK // LANE_BLOCKS):
            blocks = [yg_ref[LANE_BLOCKS * gh + gl, :, ih * LANES:(ih + 1) * LANES] for gl in range(LANE_BLOCKS)]
            for a, blk in enumerate(_block_transpose(blocks)):
                y_sc[gh, pl.ds(LANE_BLOCKS * ih + a, n_chunks, stride=SSM_CHUNK), :] = blk

    def token_rows(r0, nr):
        rows = slice(r0, r0 + nr)
        y = jnp.concatenate([y_sc[gh, rows, :] for gh in range(D_SSM // LANES)], axis=1)
        o = ot_ref[:, :, rows].reshape(D_ATTN, nr).astype(F32).T.astype(BF16)
        z = jax.nn.gelu(y)
        z = z * _sigmoid(_dot(z.astype(BF16), wglu_ref[...]) + bglu_ref[...])
        ssm_up = _dot(z.astype(BF16), wus_ref[...])
        att_up = _dot(o, wua_ref[...])
        merged = ga_ref[rows, :].astype(F32) * att_up + gs_ref[rows, :].astype(F32) * ssm_up
        mix = _dot(merged.astype(BF16), wout_ref[...])
        x1 = _layer_norm(ALPHA * x_ref[rows, :] + mix, g1_ref[...], b1_ref[...])
        x1_ref[rows, :] = x1
        xh = x1.astype(BF16)
        combine, sel = _route(_dot(xh, wr_ref[...]) + br_ref[...])
        route_ref[rows, :] = combine
        return xh, jnp.where(sel, 1.0, 0.0)

    nr = tm // MERGE_ROW_BLOCKS
    parts = [token_rows(i * nr, nr) for i in range(MERGE_ROW_BLOCKS)]
    xh = jnp.concatenate([p[0] for p in parts], axis=0)
    sel_f = jnp.concatenate([p[1] for p in parts], axis=0)
    sel = sel_f > 0.0
    rank = _dot(tri_ref[...], sel_f.astype(BF16))
    lane = lax.broadcasted_iota(jnp.int32, rank.shape, 1)
    rank = jnp.where(sel, jnp.minimum(rank, float(cap)), float(cap))
    rank = jnp.where(lane == AUX_LANE, 1.0, rank).astype(BF16)
    rank_ref[...] = rank
    miss = lax.dot_general(eaugt_ref[...], rank, (((1,), (1,)), ((), ())), preferred_element_type=F32)
    gather = jnp.where(miss == 0.0, 1.0, 0.0).astype(BF16)
    xs_ref[0] = _dot(gather, xh).astype(BF16)


def _merge_weights(w, tm, cap):
    n_r = N_EXPERT_GROUPS + N_EXPERTS
    wr = jnp.zeros((D_MODEL, LANES), F32).at[:, :n_r].set(
        jnp.concatenate([w['w_router_group'], w['w_router_expert']], axis=1).astype(F32))
    br = jnp.zeros((1, LANES), F32).at[0, :n_r].set(
        jnp.concatenate([w['b_router_group'], w['b_router_expert']]).astype(F32))
    tri =(jnp.arange(tm)[:, None] > jnp.arange(tm)[None, :]).astype(BF16)
    _, expand_aug = _slot_tables(cap)
    return [w['w_up_attn'].astype(BF16), w['w_glu'].astype(BF16), w['b_glu'].astype(F32)[None, :],
            w['w_up_ssm'].astype(BF16), w['w_out'].astype(BF16), w['ln1_g'].astype(F32)[None, :],
            w['ln1_b'].astype(F32)[None, :], wr.astype(BF16), br, tri, expand_aug.T]


def _merge(x, ot, yg, ga, gs, weights, tm, cap):
    n = x.shape[0]
    n_slot = N_EXPERTS * cap
    row = lambda i: (i, 0)
    const = lambda i: (0, 0)
    full = lambda a: pl.BlockSpec(a.shape, const)
    return pl.pallas_call(
        functools.partial(_merge_body, cap=cap),
        grid=(n // tm,),
        in_specs=[pl.BlockSpec((tm, D_MODEL), row),
                  pl.BlockSpec((N_Q_HEADS, HEAD_DIM, tm), lambda i: (0, 0, i)),
                  pl.BlockSpec((N_SSM_GROUPS, tm // SSM_CHUNK, CHUNK_W), lambda i: (0, i, 0)),
                  pl.BlockSpec((tm, D_MODEL), row), pl.BlockSpec((tm, D_MODEL), row)] + [full(a) for a in weights],
        out_specs=[pl.BlockSpec((tm, D_MODEL), row), pl.BlockSpec((tm, LANES), row), pl.BlockSpec((tm, LANES), row),
                   pl.BlockSpec((1, n_slot, D_MODEL), lambda i: (i, 0, 0))],
        out_shape=[jax.ShapeDtypeStruct((n, D_MODEL), F32), jax.ShapeDtypeStruct((n, LANES), F32),
                   jax.ShapeDtypeStruct((n, LANES), BF16), jax.ShapeDtypeStruct((n // tm, n_slot, D_MODEL), BF16)],
        scratch_shapes=[pltpu.VMEM((D_SSM // LANES, tm, LANES), F32)],
        compiler_params=_cparams(("parallel",)),
        name="merge_ln1_router",
    )(x, ot, yg, ga, gs, *weights)


def _expert_mlp(xb, wg, wu, wd):
    h = jax.nn.silu(_dot(xb, wg)) * _dot(xb, wu)
    return _dot(h.astype(BF16), wd)


def _experts_body(xs_ref, wg_ref, wu_ref, wd_ref, ys_ref, wg_sc, wu_sc, wd_sc):
    @pl.when(pl.program_id(1) == 0)
    def _():
        wg_sc[...] = wg_ref[0].astype(BF16)
        wu_sc[...] = wu_ref[0].astype(BF16)
        wd_sc[...] = wd_ref[0].astype(BF16)

    shp = xs_ref.shape
    xb = xs_ref[...].reshape(shp[0] * shp[2], D_MODEL)
    ys_ref[...] = _expert_mlp(xb, wg_sc[...], wu_sc[...], wd_sc[...]).astype(BF16).reshape(shp)


def _experts(xs, w_gate, w_up, w_down, cap, tiles_per_step):
    n_tiles = xs.shape[0]
    xs4 = xs.reshape(n_tiles, N_EXPERTS, cap, D_MODEL)
    slots = pl.BlockSpec((tiles_per_step, 1, cap, D_MODEL), lambda e, t: (t, e, 0, 0))
    exp3 = lambda e, t: (e, 0, 0)
    ys = pl.pallas_call(
        _experts_body,
        grid=(N_EXPERTS, n_tiles // tiles_per_step),
        in_specs=[slots, pl.BlockSpec((1, D_MODEL, D_FF_EXPERT), exp3), pl.BlockSpec((1, D_MODEL, D_FF_EXPERT), exp3),
                  pl.BlockSpec((1, D_FF_EXPERT, D_MODEL), exp3)],
        out_specs=slots,
        out_shape=jax.ShapeDtypeStruct(xs4.shape, BF16),
        scratch_shapes=[pltpu.VMEM((D_MODEL, D_FF_EXPERT), BF16), pltpu.VMEM((D_MODEL, D_FF_EXPERT), BF16),
                        pltpu.VMEM((D_FF_EXPERT, D_MODEL), BF16)],
        compiler_params=_cparams(("parallel", "arbitrary")),
        name="experts",
    )(xs4, w_gate, w_up, w_down)
    return ys.reshape(xs.shape)


def _combine_body(x1_ref, route_ref, rank_ref, ys_ref, eaug_ref, e_ref, g2_ref, b2_ref, wg_hbm, wu_hbm, wd_hbm,
                  o_ref, acc_sc, wg_sc, wu_sc, wd_sc, *, cap):
    route = route_ref[...]
    rank = rank_ref[...]
    hit = _dot(rank, eaug_ref[...]) == 0.0
    spread = _dot(route.astype(BF16), e_ref[...])
    moe = _dot(jnp.where(hit, spread, 0.0).astype(BF16), ys_ref[0])
    o_ref[...] = _layer_norm(ALPHA * x1_ref[...] + moe, g2_ref[...], b2_ref[...])
    lane = lax.broadcasted_iota(jnp.int32, route.shape, 1)
    spilled = jnp.where((rank.astype(F32) == float(cap)) & (lane != AUX_LANE), route, 0.0)

    @pl.when(jnp.max(jnp.abs(spilled)) > 0.0)
    def _():
        acc_sc[...] = moe
        xb = x1_ref[...].astype(BF16)

        def one_expert(e, carry):
            pltpu.sync_copy(wg_hbm.at[e], wg_sc)
            pltpu.sync_copy(wu_hbm.at[e], wu_sc)
            pltpu.sync_copy(wd_hbm.at[e], wd_sc)
            w_e = jnp.sum(jnp.where(lane == e + ROUTE_E0, spilled, 0.0), axis=-1, keepdims=True)
            acc_sc[...] += w_e * _expert_mlp(xb, wg_sc[...].astype(BF16), wu_sc[...].astype(BF16),
                                             wd_sc[...].astype(BF16))
            return carry

        lax.fori_loop(0, N_EXPERTS, one_expert, 0)
        o_ref[...] = _layer_norm(ALPHA * x1_ref[...] + acc_sc[...], g2_ref[...], b2_ref[...])


def _combine(x1, route, rank, ys, w_gate, w_up, w_down, ln2_g, ln2_b, tm, cap):
    n = x1.shape[0]
    n_slot = N_EXPERTS * cap
    expand, expand_aug = _slot_tables(cap)
    row = lambda i: (i, 0)
    const = lambda i: (0, 0)
    hbm = pl.BlockSpec(memory_space=pl.ANY)
    return pl.pallas_call(
        functools.partial(_combine_body, cap=cap),
        grid=(n // tm,),
        in_specs=[pl.BlockSpec((tm, D_MODEL), row), pl.BlockSpec((tm, LANES), row), pl.BlockSpec((tm, LANES), row),
                  pl.BlockSpec((1, n_slot, D_MODEL), lambda i: (i, 0, 0)),
                  pl.BlockSpec((LANES, n_slot), const), pl.BlockSpec((LANES, n_slot), const),
                  pl.BlockSpec((1, D_MODEL), const), pl.BlockSpec((1, D_MODEL), const), hbm, hbm, hbm],
        out_specs=pl.BlockSpec((tm, D_MODEL), row),
        out_shape=jax.ShapeDtypeStruct((n, D_MODEL), F32),
        scratch_shapes=[pltpu.VMEM((tm, D_MODEL), F32), pltpu.VMEM((D_MODEL, D_FF_EXPERT), w_gate.dtype),
                        pltpu.VMEM((D_MODEL, D_FF_EXPERT), w_up.dtype), pltpu.VMEM((D_FF_EXPERT, D_MODEL), w_down.dtype)],
        compiler_params=_cparams(("parallel",)),
        name="combine_ln2",
    )(x1, route, rank, ys, expand_aug, expand, ln2_g.astype(F32)[None, :], ln2_b.astype(F32)[None, :],
      w_gate, w_up, w_down)


def _encode(x, w, prep, cap):
    b, t, _ = x.shape
    n = b * t
    tm, tq, tk = math.gcd(512, t), math.gcd(256, t), math.gcd(1024, t // 8)
    x2 = x.reshape(n, D_MODEL).astype(F32)
    qt, kh, vt, ug, ga, gs = _in_projection(x2, t, prep['w_in'], w['q_norm_g'], w['k_norm_g'], tm, tk)
    ot = _attention(qt, kh, vt, w['q_norm_g'], w['k_norm_g'], b, t, tq, tk)
    yg = _s5_scan(ug, prep['s5'], b, t // SSM_CHUNK)
    x1, route, rank, xs = _merge(x2, ot, yg, ga, gs, _merge_weights(w, tm, cap), tm, cap)
    wg, wu, wd = prep['experts']
    ys = _experts(xs, wg, wu, wd, cap, math.gcd(8, n // tm))
    out = _combine(x1, route, rank, ys, wg, wu, wd, w['ln2_g'], w['ln2_b'], tm, cap)
    return out.reshape(b, t, D_MODEL)


def kernel(x_prompt, x_sample, w_in, q_norm_g, k_norm_g, ssm_a_re, ssm_a_im, ssm_log_dt, ssm_b_re, ssm_b_im, ssm_c_re, ssm_c_im, ssm_d, w_glu, b_glu, w_up_attn, w_up_ssm, w_out, ln1_g, ln1_b, w_router_group, b_router_group, w_router_expert, b_router_expert, w_exp_gate, w_exp_up, w_exp_down, ln2_g, ln2_b):
    w = dict(w_in=w_in, q_norm_g=q_norm_g, k_norm_g=k_norm_g, ssm_a_re=ssm_a_re, ssm_a_im=ssm_a_im,
             ssm_log_dt=ssm_log_dt, ssm_b_re=ssm_b_re, ssm_b_im=ssm_b_im, ssm_c_re=ssm_c_re, ssm_c_im=ssm_c_im,
             ssm_d=ssm_d, w_glu=w_glu, b_glu=b_glu, w_up_attn=w_up_attn, w_up_ssm=w_up_ssm, w_out=w_out,
             ln1_g=ln1_g, ln1_b=ln1_b, w_router_group=w_router_group, b_router_group=b_router_group,
             w_router_expert=w_router_expert, b_router_expert=b_router_expert, w_exp_gate=w_exp_gate,
             w_exp_up=w_exp_up, w_exp_down=w_exp_down, ln2_g=ln2_g, ln2_b=ln2_b)
    w = {name: val[0] for name, val in w.items()}
    prep = dict(
        w_in=w['w_in'].astype(BF16),
        s5=_s5_matrices(w['ssm_a_re'], w['ssm_a_im'], w['ssm_log_dt'], w['ssm_b_re'], w['ssm_b_im'],
                        w['ssm_c_re'], w['ssm_c_im'], w['ssm_d']),
        experts=tuple(w[name] for name in ('w_exp_gate', 'w_exp_up', 'w_exp_down')))
    return _encode(x_prompt, w, prep, EXPERT_CAP), _encode(x_sample, w, prep, EXPERT_CAP)
```

```python
import functools
import math

import jax
import jax.numpy as jnp
from jax import lax
from jax.experimental import pallas as pl
from jax.experimental.pallas import tpu as pltpu

F32 = jnp.float32
BF16 = jnp.bfloat16

D_MODEL = 1024
GRID_W = 64
HEAD_DIM = 64
N_Q_HEADS = 8
N_KV_HEADS = 2
Q_PER_KV = N_Q_HEADS // N_KV_HEADS
D_ATTN = N_Q_HEADS * HEAD_DIM
D_KV = N_KV_HEADS * HEAD_DIM
ROPE_THETA = 10000.0
D_SSM = D_MODEL // 2
SSM_GROUP = 16
N_SSM_GROUPS = D_SSM // SSM_GROUP
SSM_STATE = 64
N_EXPERT_GROUPS = 4
EXPERTS_PER_GROUP = 4
N_EXPERTS = N_EXPERT_GROUPS * EXPERTS_PER_GROUP
D_FF_EXPERT = D_MODEL // 2
DEPTH = 1
ALPHA = (2.0 * DEPTH) ** 0.25
EPS = 1e-6
D_IN_PROJ = D_ATTN + 2 * D_KV + D_SSM + 2 * D_MODEL
OFF_K = D_ATTN
OFF_V = OFF_K + D_KV
OFF_U = OFF_V + D_KV
OFF_GA = OFF_U + D_SSM
OFF_GS = OFF_GA + D_MODEL

LANES = 128
MXU_DIM = 256
SSM_CHUNK = 16
CHUNK_W = SSM_CHUNK * SSM_GROUP
ROUTE_E0 = N_EXPERT_GROUPS
VMEM_LIMIT = 56 * 1024 * 1024
EXPERT_CAP = 96


def _cparams(sem):
    return pltpu.CompilerParams(dimension_semantics=sem, vmem_limit_bytes=VMEM_LIMIT)


def _dot(a, b):
    return jnp.dot(a, b, preferred_element_type=F32)


def _split_bf16(x):
    hi = x.astype(BF16)
    lo = (x - hi.astype(F32)).astype(BF16)
    return hi, lo


def _sigmoid(x):
    return 0.5 * jnp.tanh(0.5 * x) + 0.5


def _layer_norm(h, g, b):
    mu = jnp.mean(h, axis=-1, keepdims=True)
    var = jnp.mean(jnp.square(h - mu), axis=-1, keepdims=True)
    return (h - mu) * lax.rsqrt(var + EPS) * g + b


QK_SCALE = HEAD_DIM ** -0.5 * math.log2(math.e)
MERGE_ROW_BLOCKS = 2
KV_BLOCKS_PER_TRIP = 8
MAX_UNSHIFTED_EXPONENT = 40.0
LANE_BLOCKS = LANES // SSM_GROUP


def _block_transpose(blocks):
    lane = lax.broadcasted_iota(jnp.int32, blocks[0].shape, 1)
    v = list(blocks)
    for s in range(3):
        d = SSM_GROUP << s
        keep = ((lane // d) % 2) == 0
        nxt = list(v)
        for a in range(LANE_BLOCKS):
            if not a & (1 << s):
                a2 = a | (1 << s)
                nxt[a] = jnp.where(keep, v[a], pltpu.roll(v[a2], d, 1))
                nxt[a2] = jnp.where(keep, pltpu.roll(v[a], LANES - d, 1), v[a2])
        v = nxt
    return v


def _inproj_body(x_ref, w_ref, gq_ref, gk_ref, cos_ref, sin_ref, seg_ref,
                 qt_ref, kh_ref, vt_ref, ug_ref, ga_ref, gs_ref, u_sc, *, tk):
    xb = x_ref[...].astype(BF16)
    tm = xb.shape[0]
    cos = cos_ref[...]
    sin = sin_ref[...]
    seg = seg_ref[...]
    lane = lax.broadcasted_iota(jnp.int32, (1, LANES), 1)
    first_half = (lane % HEAD_DIM) < (HEAD_DIM // 2)

    def mm(a, b):
        return _dot(xb, w_ref[:, a:b])

    def head_sumsq(y):
        w = y.shape[1]
        return _dot((y * y).astype(BF16), seg[0:w, 0:w])

    def norm_rope(y, ss, g):
        yn = y * lax.rsqrt(ss * (1.0 / HEAD_DIM) + EPS) * g
        sw = jnp.where(first_half, pltpu.roll(yn, LANES - HEAD_DIM // 2, 1), pltpu.roll(yn, HEAD_DIM // 2, 1))
        return yn * cos + sw * sin

    gq = gq_ref[...]
    heads_per_slab = LANES // HEAD_DIM
    wide = seg.shape[0]

    def q_slab(j):
        y = mm(j * wide, (j + 1) * wide)
        ss = head_sumsq(y)
        for jj in range(wide // LANES):
            q = norm_rope(y[:, jj * LANES:(jj + 1) * LANES], ss[:, jj * LANES:(jj + 1) * LANES], gq)
            qt = (q * QK_SCALE).T
            for h in range(heads_per_slab):
                head = (j * (wide // LANES) + jj) * heads_per_slab + h
                qt_ref[head] = qt[h * HEAD_DIM:(h + 1) * HEAD_DIM].astype(BF16)

    def kv_heads():
        yk = mm(OFF_K, OFF_V)
        k = norm_rope(yk, head_sumsq(yk), gk_ref[...])
        vt = mm(OFF_V, OFF_U).astype(BF16).astype(F32).T
        for g in range(N_KV_HEADS):
            kh_ref[g] = k[:, g * HEAD_DIM:(g + 1) * HEAD_DIM].astype(BF16)
            w = vt_ref.shape[3]
            for jb in range(tm // w):
                vt_ref[g, jb] = vt[g * HEAD_DIM:(g + 1) * HEAD_DIM, jb * w:(jb + 1) * w].astype(BF16)

    def ssm_input_regroup():
        n_chunks = tm // SSM_CHUNK
        for gh in range(D_SSM // LANES):
            for ih in range(SSM_CHUNK // LANE_BLOCKS):
                rows = [u_sc[gh, pl.ds(LANE_BLOCKS * ih + a, n_chunks, stride=SSM_CHUNK), :]
                        for a in range(LANE_BLOCKS)]
                for gl, blk in enumerate(_block_transpose(rows)):
                    ug_ref[LANE_BLOCKS * gh + gl, :, ih * LANES:(ih + 1) * LANES] = blk

    u = mm(OFF_U, OFF_GA)
    for gh in range(D_SSM // LANES):
        u_sc[gh] = u[:, gh * LANES:(gh + 1) * LANES]
    q_slab(0)
    ga_ref[...] = _sigmoid(mm(OFF_GA, OFF_GS)).astype(BF16)
    ssm_input_regroup()
    q_slab(1)
    kv_heads()
    gs_ref[...] = _sigmoid(mm(OFF_GS, D_IN_PROJ)).astype(BF16)


def _rope_tables(t_max):
    t = jnp.arange(t_max, dtype=jnp.int32)
    row_pos = (t // GRID_W).astype(F32)
    col_pos = (t % GRID_W).astype(F32)
    n_freq = HEAD_DIM // 4
    inv_freq = ROPE_THETA ** (-jnp.arange(n_freq, dtype=F32) / n_freq)
    ang = jnp.concatenate([row_pos[:, None] * inv_freq, col_pos[:, None] * inv_freq], axis=-1)
    c, s = jnp.cos(ang), jnp.sin(ang)
    reps = LANES // HEAD_DIM
    cos_t = jnp.tile(jnp.concatenate([c, c], axis=-1), (1, reps))
    sin_t = jnp.tile(jnp.concatenate([-s, s], axis=-1), (1, reps))
    return cos_t, sin_t


def _in_projection(x, t, w_in_bf16, q_norm_g, k_norm_g, tm, tk):
    n = x.shape[0]
    tiles_per_seq = t // tm
    cos_t, sin_t = _rope_tables(t)
    reps = LANES // HEAD_DIM
    gq = jnp.tile(q_norm_g.astype(F32), reps)[None, :]
    gk = jnp.tile(k_norm_g.astype(F32), reps)[None, :]
    head_of_lane = jnp.arange(MXU_DIM) // HEAD_DIM
    seg = (head_of_lane[:, None] == head_of_lane[None, :]).astype(BF16)
    pos_map = lambda i: (i % tiles_per_seq, 0)
    row = lambda i: (i, 0)
    const = lambda i: (0, 0)
    return pl.pallas_call(
        functools.partial(_inproj_body, tk=tk),
        grid=(n // tm,),
        in_specs=[
            pl.BlockSpec((tm, D_MODEL), row),
            pl.BlockSpec((D_MODEL, D_IN_PROJ), const),
            pl.BlockSpec((1, LANES), const),
            pl.BlockSpec((1, LANES), const),
            pl.BlockSpec((tm, LANES), pos_map),
            pl.BlockSpec((tm, LANES), pos_map),
            pl.BlockSpec((MXU_DIM, MXU_DIM), const),
        ],
        out_specs=[
            pl.BlockSpec((N_Q_HEADS, HEAD_DIM, tm), lambda i: (0, 0, i)),
            pl.BlockSpec((N_KV_HEADS, tm, HEAD_DIM), lambda i: (0, i, 0)),
            (pl.BlockSpec((N_KV_HEADS, tm // tk, HEAD_DIM, tk), lambda i: (0, i, 0, 0)) if tk <= tm else
             pl.BlockSpec((N_KV_HEADS, 1, HEAD_DIM, tm), lambda i: (0, i // (tk // tm), 0, i % (tk // tm)))),
            pl.BlockSpec((N_SSM_GROUPS, tm // SSM_CHUNK, CHUNK_W), lambda i: (0, i, 0)),
            pl.BlockSpec((tm, D_MODEL), row),
            pl.BlockSpec((tm, D_MODEL), row),
        ],
        out_shape=[
            jax.ShapeDtypeStruct((N_Q_HEADS, HEAD_DIM, n), BF16),
            jax.ShapeDtypeStruct((N_KV_HEADS, n, HEAD_DIM), BF16),
            jax.ShapeDtypeStruct((N_KV_HEADS, n // tk, HEAD_DIM, tk), BF16),
            jax.ShapeDtypeStruct((N_SSM_GROUPS, n // SSM_CHUNK, CHUNK_W), F32),
            jax.ShapeDtypeStruct((n, D_MODEL), BF16),
            jax.ShapeDtypeStruct((n, D_MODEL), BF16),
        ],
        scratch_shapes=[pltpu.VMEM((D_SSM // LANES, tm, LANES), F32)],
        compiler_params=_cparams(("parallel",)),
        name="in_projection",
    )(x, w_in_bf16, gq, gk, cos_t, sin_t, seg)


def _attn_body(qt_ref, k_ref, vt_ref, o_ref, m_sc, l_sc, acc_sc, s_sc, *, tq, tk, n_qt, n_kv, shift):
    n_blocks = n_qt * n_kv

    def scores(f):
        qo = pl.multiple_of((f // n_kv) * tq, tq)
        ko = pl.multiple_of((f % n_kv) * tk, tk)
        q4t = jnp.concatenate([qt_ref[h, :, pl.ds(qo, tq)] for h in range(Q_PER_KV)], axis=1)
        return _dot(k_ref[0, pl.ds(ko, tk), :], q4t)

    def row_groups(p):
        return jnp.sum(p.reshape(tk // 8, 8, p.shape[1]), axis=0)

    def softmax_pv(s, j):
        if shift:
            m_old = m_sc[...]
            m_new = jnp.maximum(m_old, jnp.max(s, axis=0, keepdims=True))
            alpha = jnp.exp2(m_old - m_new)
            p = jnp.exp2(s - m_new)
            l_sc[...] = alpha * l_sc[...] + row_groups(p)
            acc_sc[...] = alpha * acc_sc[...] + _dot(vt_ref[0, j], p.astype(BF16))
            m_sc[...] = m_new
        else:
            p = jnp.exp2(s)
            l_sc[...] += row_groups(p)
            acc_sc[...] += _dot(vt_ref[0, j], p.astype(BF16))

    s_sc[0] = scores(0)

    def query_tile(qi, carry):
        m_sc[...] = jnp.full(m_sc.shape, -jnp.inf, F32)
        l_sc[...] = jnp.zeros(l_sc.shape, F32)
        acc_sc[...] = jnp.zeros(acc_sc.shape, F32)

        per_trip = math.gcd(KV_BLOCKS_PER_TRIP, n_kv)

        def body(jj, c):
            for u in range(0, per_trip, 2):
                j = per_trip * jj + u
                f = qi * n_kv + j
                s_sc[1] = scores(f + 1)
                softmax_pv(s_sc[0], j)
                s_sc[0] = scores(jnp.minimum(f + 2, n_blocks - 1))
                softmax_pv(s_sc[1], j + 1)
            return c

        lax.fori_loop(0, n_kv // per_trip, body, 0)
        out = acc_sc[...] / jnp.sum(l_sc[...], axis=0, keepdims=True)
        qo = pl.multiple_of(qi * tq, tq)
        for h in range(Q_PER_KV):
            o_ref[h, :, pl.ds(qo, tq)] = out[:, h * tq:(h + 1) * tq].astype(o_ref.dtype)
        return carry

    lax.fori_loop(0, n_qt, query_tile, 0)


def _attention(qt, k_heads, vt_blocks, q_norm_g, k_norm_g, b, t, tq, tk):
    n_qt = t // tq
    n_kv = t // tk
    bound = QK_SCALE * HEAD_DIM * jnp.max(jnp.abs(q_norm_g.astype(F32))) * jnp.max(jnp.abs(k_norm_g.astype(F32)))
    call = functools.partial(_attention_call, b=b, t=t, tq=tq, tk=tk, n_qt=n_qt, n_kv=n_kv)
    return lax.cond(bound < MAX_UNSHIFTED_EXPONENT,
                    functools.partial(call, shift=False), functools.partial(call, shift=True),
                    qt, k_heads, vt_blocks)


def _attention_call(qt, k_heads, vt_blocks, *, b, t, tq, tk, n_qt, n_kv, shift):
    body = functools.partial(_attn_body, tq=tq, tk=tk, n_qt=n_qt, n_kv=n_kv, shift=shift)
    return pl.pallas_call(
        body,
        grid=(b, N_KV_HEADS),
        in_specs=[
            pl.BlockSpec((Q_PER_KV, HEAD_DIM, t), lambda bi, g: (g, 0, bi)),
            pl.BlockSpec((1, t, HEAD_DIM), lambda bi, g: (g, bi, 0)),
            pl.BlockSpec((1, n_kv, HEAD_DIM, tk), lambda bi, g: (g, bi, 0, 0)),
        ],
        out_specs=pl.BlockSpec((Q_PER_KV, HEAD_DIM, t), lambda bi, g: (g, 0, bi)),
        out_shape=jax.ShapeDtypeStruct((N_Q_HEADS, HEAD_DIM, b * t), BF16),
        scratch_shapes=[
            pltpu.VMEM((1, Q_PER_KV * tq), F32),
            pltpu.VMEM((8, Q_PER_KV * tq), F32),
            pltpu.VMEM((HEAD_DIM, Q_PER_KV * tq), F32),
            pltpu.VMEM((2, tk, Q_PER_KV * tq), F32),
        ],
        compiler_params=_cparams(("parallel", "parallel")),
        name="attention",
    )(qt, k_heads, vt_blocks)


def _s5_matrices(a_re, a_im, log_dt, b_re, b_im, c_re, c_im, d_skip):
    hp = lax.Precision.HIGHEST
    L, G, P, H = SSM_CHUNK, N_SSM_GROUPS, SSM_STATE, SSM_GROUP
    f = lambda v: v.astype(F32)
    lr, li = f(a_re), f(a_im)
    dt = jnp.exp(f(log_dt))[..., None]
    mag = jnp.exp(lr * dt)
    ar = mag * jnp.cos(li * dt)
    ai = mag * jnp.sin(li * dt)
    den = lr * lr + li * li
    kr = ((ar - 1.0) * lr + ai * li) / den
    ki = (ai * lr - (ar - 1.0) * li) / den
    br, bi = f(b_re), f(b_im)
    bbr = kr[..., None] * br - ki[..., None] * bi
    bbi = kr[..., None] * bi + ki[..., None] * br
    kk = jnp.arange(L + 1, dtype=F32)[:, None, None, None]
    pmag = jnp.exp(kk * (lr * dt)[None])
    ph = kk * (li * dt)[None]
    pr = pmag * jnp.cos(ph)
    pi = pmag * jnp.sin(ph)
    cr, ci = f(c_re), f(c_im)
    car = cr[None] * pr[:, :, :, None, :] - ci[None] * pi[:, :, :, None, :]
    cai = cr[None] * pi[:, :, :, None, :] + ci[None] * pr[:, :, :, None, :]
    kern = jnp.einsum('kdgoq,dgqi->kdgoi', jnp.concatenate([car, -cai], axis=-1),
                      jnp.concatenate([bbr, bbi], axis=2), precision=hp)
    jj = jnp.arange(L)[:, None]
    ii = jnp.arange(L)[None, :]

    def toeplitz(kd, lag, valid):
        blocks = jnp.where(valid[:, :, None, None, None], kd[jnp.clip(lag, 0, L)], 0.0)
        return blocks.transpose(2, 0, 4, 1, 3).reshape(G, CHUNK_W, CHUNK_W)

    msum = toeplitz(kern[:, 0], ii - jj, ii >= jj) + toeplitz(kern[:, 1], jj - ii, jj >= ii)

    def in_op(d, powers):
        prj, pij = pr[powers, d], pi[powers, d]
        re = prj[..., None] * bbr[d][None] - pij[..., None] * bbi[d][None]
        im = prj[..., None] * bbi[d][None] + pij[..., None] * bbr[d][None]
        to = lambda m: m.transpose(1, 0, 3, 2).reshape(G, CHUNK_W, P)
        return to(re), to(im)

    def out_op(d, powers):
        to = lambda m: m.transpose(1, 3, 0, 2).reshape(G, P, CHUNK_W)
        return to(car[powers, d]), to(-cai[powers, d])

    steps = jnp.arange(L)
    bm = jnp.stack(in_op(0, L - 1 - steps) + in_op(1, steps), axis=1)
    cm = jnp.stack(out_op(0, steps + 1) + out_op(1, L - steps), axis=1)
    al = jnp.stack([pr[L, 0], pi[L, 0], pr[L, 1], pi[L, 1]], axis=1)[:, :, None, :]
    dvec = jnp.tile(f(d_skip).reshape(G, 1, H), (1, 1, L))
    z_in = jnp.zeros((G // 2, 4, CHUNK_W, P), F32)
    z_out = jnp.zeros((G // 2, 4, P, CHUNK_W), F32)
    bm_pair = jnp.concatenate([jnp.concatenate([bm[0::2], z_in], axis=3),
                               jnp.concatenate([z_in, bm[1::2]], axis=3)], axis=2)
    cm_pair = jnp.concatenate([jnp.concatenate([cm[0::2], z_out], axis=3),
                               jnp.concatenate([z_out, cm[1::2]], axis=3)], axis=2)
    al_pair = jnp.concatenate([al[0::2], al[1::2]], axis=3)
    bm_all = jnp.concatenate([bm_pair[:, d] for d in range(4)], axis=2)
    cm_all = jnp.concatenate([cm_pair[:, d] for d in range(4)], axis=1)
    return msum.astype(BF16), bm_all.astype(BF16), cm_all.astype(BF16), al_pair, dvec


def _s5_body(u_ref, m_ref, bm_ref, cm_ref, al_ref, d_ref, y_ref, st_sc, *, nb, nc):
    u0, u1 = u_ref[0], u_ref[1]
    ub = jnp.concatenate([u0.astype(BF16), u1.astype(BF16)], axis=1)
    local = _dot(ub, bm_ref[0])
    for d in range(4):
        st_sc[d] = local[:, d * LANES:(d + 1) * LANES]
    al = [al_ref[0, d] for d in range(4)]

    def advance(rows, d, sr, si):
        lr = st_sc[d, rows, :]
        li = st_sc[d + 1, rows, :]
        st_sc[d, rows, :] = sr
        st_sc[d + 1, rows, :] = si
        ar, ai = al[d], al[d + 1]
        return ar * sr - ai * si + lr, ar * si + ai * sr + li

    def step(k, st):
        fr, fi, br, bi = st
        fr, fi = advance(pl.ds(k, nb, stride=nc), 0, fr, fi)
        br, bi = advance(pl.ds(nc - 1 - k, nb, stride=nc), 2, br, bi)
        return fr, fi, br, bi

    lax.fori_loop(0, nc, step, tuple(jnp.zeros((nb, LANES), F32) for _ in range(4)), unroll=8)

    entering = jnp.concatenate([st_sc[d].astype(BF16) for d in range(4)], axis=1)
    y = _dot(entering, cm_ref[0])
    y_ref[0] = y[:, :CHUNK_W] + _dot(ub[:, :CHUNK_W], m_ref[0]) + u0 * d_ref[0]
    y_ref[1] = y[:, CHUNK_W:] + _dot(ub[:, CHUNK_W:], m_ref[1]) + u1 * d_ref[1]


def _s5_scan(ug, mats, nb, nc):
    msum, bm, cm, al, dvec = mats
    rows = nb * nc
    pair3 = lambda g: (g, 0, 0)
    pair4 = lambda g: (g, 0, 0, 0)
    return pl.pallas_call(
        functools.partial(_s5_body, nb=nb, nc=nc),
        grid=(N_SSM_GROUPS // 2,),
        in_specs=[
            pl.BlockSpec((2, rows, CHUNK_W), pair3),
            pl.BlockSpec((2, CHUNK_W, CHUNK_W), pair3),
            pl.BlockSpec((1, 2 * CHUNK_W, 4 * LANES), pair3),
            pl.BlockSpec((1, 4 * LANES, 2 * CHUNK_W), pair3),
            pl.BlockSpec((1, 4, 1, LANES), pair4),
            pl.BlockSpec((2, 1, CHUNK_W), pair3),
        ],
        out_specs=pl.BlockSpec((2, rows, CHUNK_W), pair3),
        out_shape=jax.ShapeDtypeStruct((N_SSM_GROUPS, rows, CHUNK_W), F32),
        scratch_shapes=[pltpu.VMEM((4, rows, LANES), F32)],
        compiler_params=_cparams(("parallel",)),
        name="s5_scan",
    )(ug, msum, bm, cm, al, dvec)


def _route(logits):
    lane_i = lax.broadcasted_iota(jnp.int32, logits.shape, 1)
    lane = lane_i.astype(F32)
    neg = -jnp.inf
    big = float(LANES)
    is_g = lane_i < N_EXPERT_GROUPS
    gl = jnp.where(is_g, logits, neg)
    gmax = jnp.max(gl, axis=-1, keepdims=True)
    gidx = jnp.min(jnp.where(gl == gmax, lane, big), axis=-1, keepdims=True)
    g_w = 1.0 / jnp.sum(jnp.where(is_g, jnp.exp(logits - gmax), 0.0), axis=-1, keepdims=True)
    e_lane = lane_i - ROUTE_E0
    group_of_lane = (e_lane // EXPERTS_PER_GROUP).astype(F32)
    in_group = (e_lane >= 0) & (e_lane < N_EXPERTS) & (group_of_lane == gidx)
    el = jnp.where(in_group, logits, neg)
    v1 = jnp.max(el, axis=-1, keepdims=True)
    i1 = jnp.min(jnp.where(el == v1, lane, big), axis=-1, keepdims=True)
    el2 = jnp.where(lane == i1, neg, el)
    v2 = jnp.max(el2, axis=-1, keepdims=True)
    i2 = jnp.min(jnp.where(el2 == v2, lane, big), axis=-1, keepdims=True)
    e2 = jnp.exp(v2 - v1)
    inv = g_w / (1.0 + e2)
    combine = jnp.where(lane == i1, inv, 0.0) + jnp.where(lane == i2, e2 * inv, 0.0)
    return combine, (lane == i1) | (lane == i2)


AUX_LANE = LANES - 1


def _slot_tables(cap):
    slot = jnp.arange(N_EXPERTS * cap)
    lane = jnp.arange(LANES)[:, None]
    expand = (lane == ROUTE_E0 + slot // cap).astype(F32)
    expand_aug = jnp.where(lane == AUX_LANE, -(slot % cap).astype(F32), expand)
    return expand.astype(BF16), expand_aug.astype(BF16)


def _merge_body(x_ref, ot_ref, yg_ref, ga_ref, gs_ref, wua_ref, wglu_ref, bglu_ref, wus_ref, wout_ref,
                g1_ref, b1_ref, wr_ref, br_ref, tri_ref, eaugt_ref,
                x1_ref, route_ref, rank_ref, xs_ref, y_sc, *, cap):
    tm = x_ref.shape[0]
    n_chunks = tm // SSM_CHUNK
    for gh in range(D_SSM // LANES):
        for ih in range(SSM_CHUNK // LANE_BLOCKS):
            blocks = [yg_ref[LANE_BLOCKS * gh + gl, :, ih * LANES:(ih + 1) * LANES] for gl in range(LANE_BLOCKS)]
            for a, blk in enumerate(_block_transpose(blocks)):
                y_sc[gh, pl.ds(LANE_BLOCKS * ih + a, n_chunks, stride=SSM_CHUNK), :] = blk

    def token_rows(r0, nr):
        rows = slice(r0, r0 + nr)
        y = jnp.concatenate([y_sc[gh, rows, :] for gh in range(D_SSM // LANES)], axis=1)
        o = ot_ref[:, :, rows].reshape(D_ATTN, nr).astype(F32).T.astype(BF16)
        z = jax.nn.gelu(y)
        z = z * _sigmoid(_dot(z.astype(BF16), wglu_ref[...]) + bglu_ref[...])
        ssm_up = _dot(z.astype(BF16), wus_ref[...])
        att_up = _dot(o, wua_ref[...])
        merged = ga_ref[rows, :].astype(F32) * att_up + gs_ref[rows, :].astype(F32) * ssm_up
        mix = _dot(merged.astype(BF16), wout_ref[...])
        x1 = _layer_norm(ALPHA * x_ref[rows, :] + mix, g1_ref[...], b1_ref[...])
        x1_ref[rows, :] = x1
        xh = x1.astype(BF16)
        combine, sel = _route(_dot(xh, wr_ref[...]) + br_ref[...])
        route_ref[rows, :] = combine
        return xh, jnp.where(sel, 1.0, 0.0)

    nr = tm // MERGE_ROW_BLOCKS
    parts = [token_rows(i * nr, nr) for i in range(MERGE_ROW_BLOCKS)]
    xh = jnp.concatenate([p[0] for p in parts], axis=0)
    sel_f = jnp.concatenate([p[1] for p in parts], axis=0)
    sel = sel_f > 0.0
    rank = _dot(tri_ref[...], sel_f.astype(BF16))
    lane = lax.broadcasted_iota(jnp.int32, rank.shape, 1)
    rank = jnp.where(sel, jnp.minimum(rank, float(cap)), float(cap))
    rank = jnp.where(lane == AUX_LANE, 1.0, rank).astype(BF16)
    rank_ref[...] = rank
    miss = lax.dot_general(eaugt_ref[...], rank, (((1,), (1,)), ((), ())), preferred_element_type=F32)
    gather = jnp.where(miss == 0.0, 1.0, 0.0).astype(BF16)
    xs_ref[0] = _dot(gather, xh).astype(BF16)


def _merge_weights(w, tm, cap):
    n_r = N_EXPERT_GROUPS + N_EXPERTS
    wr = jnp.zeros((D_MODEL, LANES), F32).at[:, :n_r].set(
        jnp.concatenate([w['w_router_group'], w['w_router_expert']], axis=1).astype(F32))
    br = jnp.zeros((1, LANES), F32).at[0, :n_r].set(
        jnp.concatenate([w['b_router_group'], w['b_router_expert']]).astype(F32))
    tri =(jnp.arange(tm)[:, None] > jnp.arange(tm)[None, :]).astype(BF16)
    _, expand_aug = _slot_tables(cap)
    return [w['w_up_attn'].astype(BF16), w['w_glu'].astype(BF16), w['b_glu'].astype(F32)[None, :],
            w['w_up_ssm'].astype(BF16), w['w_out'].astype(BF16), w['ln1_g'].astype(F32)[None, :],
            w['ln1_b'].astype(F32)[None, :], wr.astype(BF16), br, tri, expand_aug.T]


def _merge(x, ot, yg, ga, gs, weights, tm, cap):
    n = x.shape[0]
    n_slot = N_EXPERTS * cap
    row = lambda i: (i, 0)
    const = lambda i: (0, 0)
    full = lambda a: pl.BlockSpec(a.shape, const)
    return pl.pallas_call(
        functools.partial(_merge_body, cap=cap),
        grid=(n // tm,),
        in_specs=[pl.BlockSpec((tm, D_MODEL), row),
                  pl.BlockSpec((N_Q_HEADS, HEAD_DIM, tm), lambda i: (0, 0, i)),
                  pl.BlockSpec((N_SSM_GROUPS, tm // SSM_CHUNK, CHUNK_W), lambda i: (0, i, 0)),
                  pl.BlockSpec((tm, D_MODEL), row), pl.BlockSpec((tm, D_MODEL), row)] + [full(a) for a in weights],
        out_specs=[pl.BlockSpec((tm, D_MODEL), row), pl.BlockSpec((tm, LANES), row), pl.BlockSpec((tm, LANES), row),
                   pl.BlockSpec((1, n_slot, D_MODEL), lambda i: (i, 0, 0))],
        out_shape=[jax.ShapeDtypeStruct((n, D_MODEL), F32), jax.ShapeDtypeStruct((n, LANES), F32),
                   jax.ShapeDtypeStruct((n, LANES), BF16), jax.ShapeDtypeStruct((n // tm, n_slot, D_MODEL), BF16)],
        scratch_shapes=[pltpu.VMEM((D_SSM // LANES, tm, LANES), F32)],
        compiler_params=_cparams(("parallel",)),
        name="merge_ln1_router",
    )(x, ot, yg, ga, gs, *weights)


def _expert_mlp(xb, wg, wu, wd):
    h = jax.nn.silu(_dot(xb, wg)) * _dot(xb, wu)
    return _dot(h.astype(BF16), wd)


def _experts_body(xs_ref, wg_ref, wu_ref, wd_ref, ys_ref, wg_sc, wu_sc, wd_sc):
    @pl.when(pl.program_id(1) == 0)
    def _():
        wg_sc[...] = wg_ref[0].astype(BF16)
        wu_sc[...] = wu_ref[0].astype(BF16)
        wd_sc[...] = wd_ref[0].astype(BF16)

    shp = xs_ref.shape
    xb = xs_ref[...].reshape(shp[0] * shp[2], D_MODEL)
    ys_ref[...] = _expert_mlp(xb, wg_sc[...], wu_sc[...], wd_sc[...]).astype(BF16).reshape(shp)


def _experts(xs, w_gate, w_up, w_down, cap, tiles_per_step):
    n_tiles = xs.shape[0]
    xs4 = xs.reshape(n_tiles, N_EXPERTS, cap, D_MODEL)
    slots = pl.BlockSpec((tiles_per_step, 1, cap, D_MODEL), lambda e, t: (t, e, 0, 0))
    exp3 = lambda e, t: (e, 0, 0)
    ys = pl.pallas_call(
        _experts_body,
        grid=(N_EXPERTS, n_tiles // tiles_per_step),
        in_specs=[slots, pl.BlockSpec((1, D_MODEL, D_FF_EXPERT), exp3), pl.BlockSpec((1, D_MODEL, D_FF_EXPERT), exp3),
                  pl.BlockSpec((1, D_FF_EXPERT, D_MODEL), exp3)],
        out_specs=slots,
        out_shape=jax.ShapeDtypeStruct(xs4.shape, BF16),
        scratch_shapes=[pltpu.VMEM((D_MODEL, D_FF_EXPERT), BF16), pltpu.VMEM((D_MODEL, D_FF_EXPERT), BF16),
                        pltpu.VMEM((D_FF_EXPERT, D_MODEL), BF16)],
        compiler_params=_cparams(("parallel", "arbitrary")),
        name="experts",
    )(xs4, w_gate, w_up, w_down)
    return ys.reshape(xs.shape)


def _combine_body(x1_ref, route_ref, rank_ref, ys_ref, eaug_ref, e_ref, g2_ref, b2_ref, wg_hbm, wu_hbm, wd_hbm,
                  o_ref, acc_sc, wg_sc, wu_sc, wd_sc, *, cap):
    route = route_ref[...]
    rank = rank_ref[...]
    hit = _dot(rank, eaug_ref[...]) == 0.0
    spread = _dot(route.astype(BF16), e_ref[...])
    moe = _dot(jnp.where(hit, spread, 0.0).astype(BF16), ys_ref[0])
    o_ref[...] = _layer_norm(ALPHA * x1_ref[...] + moe, g2_ref[...], b2_ref[...])
    lane = lax.broadcasted_iota(jnp.int32, route.shape, 1)
    spilled = jnp.where((rank.astype(F32) == float(cap)) & (lane != AUX_LANE), route, 0.0)

    @pl.when(jnp.max(jnp.abs(spilled)) > 0.0)
    def _():
        acc_sc[...] = moe
        xb = x1_ref[...].astype(BF16)

        def one_expert(e, carry):
            pltpu.sync_copy(wg_hbm.at[e], wg_sc)
            pltpu.sync_copy(wu_hbm.at[e], wu_sc)
            pltpu.sync_copy(wd_hbm.at[e], wd_sc)
            w_e = jnp.sum(jnp.where(lane == e + ROUTE_E0, spilled, 0.0), axis=-1, keepdims=True)
            acc_sc[...] += w_e * _expert_mlp(xb, wg_sc[...].astype(BF16), wu_sc[...].astype(BF16),
                                             wd_sc[...].astype(BF16))
            return carry

        lax.fori_loop(0, N_EXPERTS, one_expert, 0)
        o_ref[...] = _layer_norm(ALPHA * x1_ref[...] + acc_sc[...], g2_ref[...], b2_ref[...])


def _combine(x1, route, rank, ys, w_gate, w_up, w_down, ln2_g, ln2_b, tm, cap):
    n = x1.shape[0]
    n_slot = N_EXPERTS * cap
    expand, expand_aug = _slot_tables(cap)
    row = lambda i: (i, 0)
    const = lambda i: (0, 0)
    hbm = pl.BlockSpec(memory_space=pl.ANY)
    return pl.pallas_call(
        functools.partial(_combine_body, cap=cap),
        grid=(n // tm,),
        in_specs=[pl.BlockSpec((tm, D_MODEL), row), pl.BlockSpec((tm, LANES), row), pl.BlockSpec((tm, LANES), row),
                  pl.BlockSpec((1, n_slot, D_MODEL), lambda i: (i, 0, 0)),
                  pl.BlockSpec((LANES, n_slot), const), pl.BlockSpec((LANES, n_slot), const),
                  pl.BlockSpec((1, D_MODEL), const), pl.BlockSpec((1, D_MODEL), const), hbm, hbm, hbm],
        out_specs=pl.BlockSpec((tm, D_MODEL), row),
        out_shape=jax.ShapeDtypeStruct((n, D_MODEL), F32),
        scratch_shapes=[pltpu.VMEM((tm, D_MODEL), F32), pltpu.VMEM((D_MODEL, D_FF_EXPERT), w_gate.dtype),
                        pltpu.VMEM((D_MODEL, D_FF_EXPERT), w_up.dtype), pltpu.VMEM((D_FF_EXPERT, D_MODEL), w_down.dtype)],
        compiler_params=_cparams(("parallel",)),
        name="combine_ln2",
    )(x1, route, rank, ys, expand_aug, expand, ln2_g.astype(F32)[None, :], ln2_b.astype(F32)[None, :],
      w_gate, w_up, w_down)


def _encode(x, w, prep, cap):
    b, t, _ = x.shape
    n = b * t
    tm, tq, tk = math.gcd(512, t), math.gcd(256, t), math.gcd(1024, t // 8)
    x2 = x.reshape(n, D_MODEL).astype(F32)
    qt, kh, vt, ug, ga, gs = _in_projection(x2, t, prep['w_in'], w['q_norm_g'], w['k_norm_g'], tm, tk)
    ot = _attention(qt, kh, vt, w['q_norm_g'], w['k_norm_g'], b, t, tq, tk)
    yg = _s5_scan(ug, prep['s5'], b, t // SSM_CHUNK)
    x1, route, rank, xs = _merge(x2, ot, yg, ga, gs, _merge_weights(w, tm, cap), tm, cap)
    wg, wu, wd = prep['experts']
    ys = _experts(xs, wg, wu, wd, cap, math.gcd(16, n // tm))
    out = _combine(x1, route, rank, ys, wg, wu, wd, w['ln2_g'], w['ln2_b'], tm, cap)
    return out.reshape(b, t, D_MODEL)


def kernel(x_prompt, x_sample, w_in, q_norm_g, k_norm_g, ssm_a_re, ssm_a_im, ssm_log_dt, ssm_b_re, ssm_b_im, ssm_c_re, ssm_c_im, ssm_d, w_glu, b_glu, w_up_attn, w_up_ssm, w_out, ln1_g, ln1_b, w_router_group, b_router_group, w_router_expert, b_router_expert, w_exp_gate, w_exp_up, w_exp_down, ln2_g, ln2_b):
    w = dict(w_in=w_in, q_norm_g=q_norm_g, k_norm_g=k_norm_g, ssm_a_re=ssm_a_re, ssm_a_im=ssm_a_im,
             ssm_log_dt=ssm_log_dt, ssm_b_re=ssm_b_re, ssm_b_im=ssm_b_im, ssm_c_re=ssm_c_re, ssm_c_im=ssm_c_im,
             ssm_d=ssm_d, w_glu=w_glu, b_glu=b_glu, w_up_attn=w_up_attn, w_up_ssm=w_up_ssm, w_out=w_out,
             ln1_g=ln1_g, ln1_b=ln1_b, w_router_group=w_router_group, b_router_group=b_router_group,
             w_router_expert=w_router_expert, b_router_expert=b_router_expert, w_exp_gate=w_exp_gate,
             w_exp_up=w_exp_up, w_exp_down=w_exp_down, ln2_g=ln2_g, ln2_b=ln2_b)
    w = {name: val[0] for name, val in w.items()}
    prep = dict(
        w_in=w['w_in'].astype(BF16),
        s5=_s5_matrices(w['ssm_a_re'], w['ssm_a_im'], w['ssm_log_dt'], w['ssm_b_re'], w['ssm_b_im'],
                        w['ssm_c_re'], w['ssm_c_im'], w['ssm_d']),
        experts=tuple(w[name] for name in ('w_exp_gate', 'w_exp_up', 'w_exp_down')))
    return _encode(x_prompt, w, prep, EXPERT_CAP), _encode(x_sample, w, prep, EXPERT_CAP)
```
